```python
import jax, jax.numpy as jnp
from jax import lax
import numpy as np

D_MODEL = 4096
BATCH = 8
SEQ = 4096
DEPTH = 1

MLA_HEADS = 16
MLA_Q_RANK = 1024
MLA_KV_RANK = 512
MLA_NOPE = 128
MLA_ROPE = 64
MLA_V = 128
MLA_QK = MLA_NOPE + MLA_ROPE
ROPE_THETA = 10000.0
SWA_Q_HEADS = 32
SWA_KV_HEADS = 8
SWA_HEAD_DIM = 64
SWA_GROUP = SWA_Q_HEADS // SWA_KV_HEADS
SWA_WINDOW = 128
BLOCK = 128
MIX_A = MLA_HEADS * MLA_V
MIX_B = SWA_Q_HEADS * SWA_HEAD_DIM
MIX_WIDTH = MIX_A + MIX_B
D_FF = -(-8 * D_MODEL // (3 * 256)) * 256
EPS = 1e-6
IN_SPLITS = (MLA_Q_RANK, MLA_KV_RANK, MLA_ROPE, MIX_B, SWA_KV_HEADS * SWA_HEAD_DIM, SWA_KV_HEADS * SWA_HEAD_DIM)
IN_WIDTH = sum(IN_SPLITS)
SPLIT_IDX = tuple(int(v) for v in np.cumsum(IN_SPLITS)[:-1])

kernel_name = "hybrid_mla_swa_sink_alibi_swiglu_sandwich"


def rms_norm(x, g):
    xf = x.astype(jnp.float32)
    y = xf * lax.rsqrt(jnp.mean(xf * xf, axis=-1, keepdims=True) + EPS)
    return (y * g.astype(jnp.float32)).astype(x.dtype)


def rope_tables(positions):
    inv = 1.0 / (ROPE_THETA ** (jnp.arange(0, MLA_ROPE, 2, dtype=jnp.float32) / MLA_ROPE))
    ang = positions.astype(jnp.float32)[..., None] * inv
    return jnp.cos(ang), jnp.sin(ang)


def apply_rope(x, cos, sin):
    xf = x.astype(jnp.float32)
    x1, x2 = jnp.split(xf, 2, axis=-1)
    return jnp.concatenate([x1 * cos - x2 * sin, x2 * cos + x1 * sin], axis=-1).astype(x.dtype)


def mla_attention(c_q, c_kv, k_rope, cos, sin, q_norm_g, w_uq, kv_norm_g, w_ukv):
    B, S, _ = c_q.shape
    q = (rms_norm(c_q, q_norm_g) @ w_uq).reshape(B, S, MLA_HEADS, MLA_QK)
    q_nope, q_pe = q[..., :MLA_NOPE], q[..., MLA_NOPE:]
    q_pe = apply_rope(q_pe, cos[:, :, None, :], sin[:, :, None, :])
    q = jnp.concatenate([q_nope, q_pe], axis=-1)
    kv = (rms_norm(c_kv, kv_norm_g) @ w_ukv).reshape(B, S, MLA_HEADS, MLA_NOPE + MLA_V)
    k_nope, v = kv[..., :MLA_NOPE], kv[..., MLA_NOPE:]
    k_pe = apply_rope(k_rope, cos, sin)[:, :, None, :]
    k = jnp.concatenate([k_nope, jnp.broadcast_to(k_pe, (B, S, MLA_HEADS, MLA_ROPE))], axis=-1)
    scale = MLA_QK ** -0.5
    nb = S // BLOCK
    qb = q.reshape(B, nb, BLOCK, MLA_HEADS, MLA_QK).transpose(1, 0, 2, 3, 4)
    k_idx = jnp.arange(S)

    def one_block(args):
        qi, i = args
        s = jnp.einsum('bqhd,bkhd->bhqk', qi, k, preferred_element_type=jnp.float32) * scale
        q_idx = i * BLOCK + jnp.arange(BLOCK)
        causal = k_idx[None, :] <= q_idx[:, None]
        s = jnp.where(causal[None, None], s, jnp.finfo(jnp.float32).min)
        p = jax.nn.softmax(s, axis=-1).astype(v.dtype)
        return jnp.einsum('bhqk,bkhd->bqhd', p, v)

    o = lax.map(one_block, (qb, jnp.arange(nb)))
    return o.transpose(1, 0, 2, 3, 4).reshape(B, S, MIX_A)


def swa_attention(q, k, v, positions, sinks):
    B, S, _ = q.shape
    nb = S // BLOCK
    q = q.reshape(B, nb, BLOCK, SWA_KV_HEADS, SWA_GROUP, SWA_HEAD_DIM)
    k = k.reshape(B, S, SWA_KV_HEADS, SWA_HEAD_DIM)
    v = v.reshape(B, S, SWA_KV_HEADS, SWA_HEAD_DIM)

    def band(t):
        pad = [(0, 0), (BLOCK, 0)] + [(0, 0)] * (t.ndim - 2)
        tb = jnp.pad(t, pad).reshape((B, nb + 1, BLOCK) + t.shape[2:])
        return jnp.concatenate([tb[:, :-1], tb[:, 1:]], axis=2)

    k_band, v_band = band(k), band(v)
    k_pos = band(positions)
    q_pos = positions.reshape(B, nb, BLOCK)
    q_idx = jnp.arange(S).reshape(nb, BLOCK)
    k_idx = jnp.arange(-BLOCK, S).reshape(nb + 1, BLOCK)
    k_idx = jnp.concatenate([k_idx[:-1], k_idx[1:]], axis=1)
    delta = q_idx[:, :, None] - k_idx[:, None, :]
    valid = (delta >= 0) & (delta < SWA_WINDOW) & (k_idx[:, None, :] >= 0)
    dist = jnp.abs(q_pos[..., :, None] - k_pos[..., None, :]).astype(jnp.float32)
    slopes = jnp.exp2(-8.0 * jnp.arange(1, SWA_Q_HEADS + 1, dtype=jnp.float32) / SWA_Q_HEADS)
    slopes = slopes.reshape(SWA_KV_HEADS, SWA_GROUP)
    scale = SWA_HEAD_DIM ** -0.5
    s = jnp.einsum('bnqkgd,bnskd->bnkgqs', q, k_band, preferred_element_type=jnp.float32) * scale
    s = s - slopes[None, None, :, :, None, None] * dist[:, :, None, None]
    s = jnp.where(valid[None, :, None, None], s, jnp.finfo(jnp.float32).min)
    sink = sinks.astype(jnp.float32).reshape(SWA_KV_HEADS, SWA_GROUP)[None, None, :, :, None, None]
    m = jnp.maximum(jnp.max(s, axis=-1, keepdims=True), sink)
    e = jnp.exp(s - m)
    p = e / (jnp.sum(e, axis=-1, keepdims=True) + jnp.exp(sink - m))
    o = jnp.einsum('bnkgqs,bnskd->bnqkgd', p.astype(v.dtype), v_band)
    return o.reshape(B, S, MIX_B)


def _fwd_setup_inputs(seed: int = 0) -> dict:
    key = jax.random.key(seed)
    ks = jax.random.split(key, 20)
    f32 = jnp.float32

    def w(k, shape, fan_in):
        return jax.random.normal(k, shape, f32) * (fan_in ** -0.5)

    def gain(k, n):
        return 1.0 + 0.02 * jax.random.normal(k, (DEPTH, n), f32)

    return {
        "x": jax.random.normal(ks[0], (BATCH, SEQ, D_MODEL), f32),
        "positions": jnp.broadcast_to(jnp.arange(SEQ, dtype=jnp.int32), (BATCH, SEQ)),
        "attn_pre_g": gain(ks[1], D_MODEL),
        "w_in": w(ks[2], (DEPTH, D_MODEL, IN_WIDTH), D_MODEL),
        "q_norm_g": gain(ks[3], MLA_Q_RANK),
        "w_uq": w(ks[4], (DEPTH, MLA_Q_RANK, MLA_HEADS * MLA_QK), MLA_Q_RANK),
        "kv_norm_g": gain(ks[5], MLA_KV_RANK),
        "w_ukv": w(ks[6], (DEPTH, MLA_KV_RANK, MLA_HEADS * (MLA_NOPE + MLA_V)), MLA_KV_RANK),
        "swa_sinks": jax.random.normal(ks[7], (DEPTH, SWA_Q_HEADS), f32),
        "grp_a_g": gain(ks[8], MIX_A),
        "grp_b_g": gain(ks[9], MIX_B),
        "w_o": w(ks[10], (DEPTH, MIX_WIDTH, D_MODEL), MIX_WIDTH),
        "attn_post_g": gain(ks[11], D_MODEL),
        "ffn_pre_g": gain(ks[12], D_MODEL),
        "w_gate": w(ks[13], (DEPTH, D_MODEL, D_FF), D_MODEL),
        "w_up": w(ks[14], (DEPTH, D_MODEL, D_FF), D_MODEL),
        "w_down": w(ks[15], (DEPTH, D_FF, D_MODEL), D_FF),
        "ffn_post_g": gain(ks[16], D_MODEL),
    }


def _fwd_reference(x, positions, attn_pre_g, w_in, q_norm_g, w_uq, kv_norm_g, w_ukv, swa_sinks,
              grp_a_g, grp_b_g, w_o, attn_post_g, ffn_pre_g, w_gate, w_up, w_down, ffn_post_g):
    cos, sin = rope_tables(positions)
    h = x
    for l in range(DEPTH):
        a = rms_norm(h, attn_pre_g[l])
        proj = a @ w_in[l]
        c_q, c_kv, k_rope, q_s, k_s, v_s = jnp.split(proj, SPLIT_IDX, axis=-1)
        o_a = mla_attention(c_q, c_kv, k_rope, cos, sin, q_norm_g[l], w_uq[l], kv_norm_g[l], w_ukv[l])
        o_b = swa_attention(q_s, k_s, v_s, positions, swa_sinks[l])
        mix = jnp.concatenate([rms_norm(o_a, grp_a_g[l]), rms_norm(o_b, grp_b_g[l])], axis=-1)
        h = h + rms_norm(mix @ w_o[l], attn_post_g[l])
        f = rms_norm(h, ffn_pre_g[l])
        f = (jax.nn.silu(f @ w_gate[l]) * (f @ w_up[l])) @ w_down[l]
        h = h + rms_norm(f, ffn_post_g[l])
    return h


import jax as _jax
import jax.numpy as _jnp

TWIN_FORMAT = 'train_step'
FWD_PARAMS = ['x', 'positions', 'attn_pre_g', 'w_in', 'q_norm_g', 'w_uq', 'kv_norm_g', 'w_ukv', 'swa_sinks', 'grp_a_g', 'grp_b_g', 'w_o', 'attn_post_g', 'ffn_pre_g', 'w_gate', 'w_up', 'w_down', 'ffn_post_g']
TWIN_WEIGHTS = ['attn_pre_g', 'w_in', 'q_norm_g', 'w_uq', 'kv_norm_g', 'w_ukv', 'swa_sinks', 'grp_a_g', 'grp_b_g', 'w_o', 'attn_post_g', 'ffn_pre_g', 'w_gate', 'w_up', 'w_down', 'ffn_post_g']
TWIN_DIFF_INPUT = 'x'
TWIN_INPUTS = ['x', 'positions', 'attn_pre_g', 'w_in', 'q_norm_g', 'w_uq', 'kv_norm_g', 'w_ukv', 'swa_sinks', 'grp_a_g', 'grp_b_g', 'w_o', 'attn_post_g', 'ffn_pre_g', 'w_gate', 'w_up', 'w_down', 'ffn_post_g', 'loss_target', 'm_attn_pre_g', 'm_w_in', 'm_q_norm_g', 'm_w_uq', 'm_kv_norm_g', 'm_w_ukv', 'm_swa_sinks', 'm_grp_a_g', 'm_grp_b_g', 'm_w_o', 'm_attn_post_g', 'm_ffn_pre_g', 'm_w_gate', 'm_w_up', 'm_w_down', 'm_ffn_post_g', 'v_attn_pre_g', 'v_w_in', 'v_q_norm_g', 'v_w_uq', 'v_kv_norm_g', 'v_w_ukv', 'v_swa_sinks', 'v_grp_a_g', 'v_grp_b_g', 'v_w_o', 'v_attn_post_g', 'v_ffn_pre_g', 'v_w_gate', 'v_w_up', 'v_w_down', 'v_ffn_post_g']
TWIN_OUTPUTS = ['loss', 'grad_x', 'grad_attn_pre_g', 'grad_w_in', 'grad_q_norm_g', 'grad_w_uq', 'grad_kv_norm_g', 'grad_w_ukv', 'grad_swa_sinks', 'grad_grp_a_g', 'grad_grp_b_g', 'grad_w_o', 'grad_attn_post_g', 'grad_ffn_pre_g', 'grad_w_gate', 'grad_w_up', 'grad_w_down', 'grad_ffn_post_g', 'delta_attn_pre_g', 'delta_w_in', 'delta_q_norm_g', 'delta_w_uq', 'delta_kv_norm_g', 'delta_w_ukv', 'delta_swa_sinks', 'delta_grp_a_g', 'delta_grp_b_g', 'delta_w_o', 'delta_attn_post_g', 'delta_ffn_pre_g', 'delta_w_gate', 'delta_w_up', 'delta_w_down', 'delta_ffn_post_g', 'new_m_attn_pre_g', 'new_m_w_in', 'new_m_q_norm_g', 'new_m_w_uq', 'new_m_kv_norm_g', 'new_m_w_ukv', 'new_m_swa_sinks', 'new_m_grp_a_g', 'new_m_grp_b_g', 'new_m_w_o', 'new_m_attn_post_g', 'new_m_ffn_pre_g', 'new_m_w_gate', 'new_m_w_up', 'new_m_w_down', 'new_m_ffn_post_g', 'new_v_attn_pre_g', 'new_v_w_in', 'new_v_q_norm_g', 'new_v_w_uq', 'new_v_kv_norm_g', 'new_v_w_ukv', 'new_v_swa_sinks', 'new_v_grp_a_g', 'new_v_grp_b_g', 'new_v_w_o', 'new_v_attn_post_g', 'new_v_ffn_pre_g', 'new_v_w_gate', 'new_v_w_up', 'new_v_w_down', 'new_v_ffn_post_g']
TWIN_LEAF_KINDS = {'loss': 'loss', 'grad_x': 'grad_x', 'grad_attn_pre_g': 'grad_w', 'grad_w_in': 'grad_w', 'grad_q_norm_g': 'grad_w', 'grad_w_uq': 'grad_w', 'grad_kv_norm_g': 'grad_w', 'grad_w_ukv': 'grad_w', 'grad_swa_sinks': 'grad_w', 'grad_grp_a_g': 'grad_w', 'grad_grp_b_g': 'grad_w', 'grad_w_o': 'grad_w', 'grad_attn_post_g': 'grad_w', 'grad_ffn_pre_g': 'grad_w', 'grad_w_gate': 'grad_w', 'grad_w_up': 'grad_w', 'grad_w_down': 'grad_w', 'grad_ffn_post_g': 'grad_w', 'delta_attn_pre_g': 'delta_w', 'delta_w_in': 'delta_w', 'delta_q_norm_g': 'delta_w', 'delta_w_uq': 'delta_w', 'delta_kv_norm_g': 'delta_w', 'delta_w_ukv': 'delta_w', 'delta_swa_sinks': 'delta_w', 'delta_grp_a_g': 'delta_w', 'delta_grp_b_g': 'delta_w', 'delta_w_o': 'delta_w', 'delta_attn_post_g': 'delta_w', 'delta_ffn_pre_g': 'delta_w', 'delta_w_gate': 'delta_w', 'delta_w_up': 'delta_w', 'delta_w_down': 'delta_w', 'delta_ffn_post_g': 'delta_w', 'new_m_attn_pre_g': 'new_m', 'new_m_w_in': 'new_m', 'new_m_q_norm_g': 'new_m', 'new_m_w_uq': 'new_m', 'new_m_kv_norm_g': 'new_m', 'new_m_w_ukv': 'new_m', 'new_m_swa_sinks': 'new_m', 'new_m_grp_a_g': 'new_m', 'new_m_grp_b_g': 'new_m', 'new_m_w_o': 'new_m', 'new_m_attn_post_g': 'new_m', 'new_m_ffn_pre_g': 'new_m', 'new_m_w_gate': 'new_m', 'new_m_w_up': 'new_m', 'new_m_w_down': 'new_m', 'new_m_ffn_post_g': 'new_m', 'new_v_attn_pre_g': 'new_v', 'new_v_w_in': 'new_v', 'new_v_q_norm_g': 'new_v', 'new_v_w_uq': 'new_v', 'new_v_kv_norm_g': 'new_v', 'new_v_w_ukv': 'new_v', 'new_v_swa_sinks': 'new_v', 'new_v_grp_a_g': 'new_v', 'new_v_grp_b_g': 'new_v', 'new_v_w_o': 'new_v', 'new_v_attn_post_g': 'new_v', 'new_v_ffn_pre_g': 'new_v', 'new_v_w_gate': 'new_v', 'new_v_w_up': 'new_v', 'new_v_w_down': 'new_v', 'new_v_ffn_post_g': 'new_v'}


def _forward(args):
    return _fwd_reference(*[args[k] for k in FWD_PARAMS])


def _output_shape():
    out = _jax.eval_shape(lambda: _forward(_fwd_setup_inputs(0)))
    return out.shape, out.dtype

N_MICROBATCH = 1
ADAM_LR = 0.001
ADAM_B1 = 0.9
ADAM_B2 = 0.999
ADAM_EPS = 1e-08
ADAM_WD = 0.01
ADAM_STEP = 10
PER_EXAMPLE_BATCH_AXIS = {'x': 0, 'positions': 0, 'loss_target': 0}
SHARED_INPUTS = []
_WEIGHT_DTYPES = {'attn_pre_g': _jnp.float32, 'w_in': _jnp.float32, 'q_norm_g': _jnp.float32, 'w_uq': _jnp.float32, 'kv_norm_g': _jnp.float32, 'w_ukv': _jnp.float32, 'swa_sinks': _jnp.float32, 'grp_a_g': _jnp.float32, 'grp_b_g': _jnp.float32, 'w_o': _jnp.float32, 'attn_post_g': _jnp.float32, 'ffn_pre_g': _jnp.float32, 'w_gate': _jnp.float32, 'w_up': _jnp.float32, 'w_down': _jnp.float32, 'ffn_post_g': _jnp.float32}
MOMENT_SCALE = {'attn_pre_g': 1.946724e-01, 'w_in': 1.818402e-01, 'q_norm_g': 1.953950e-01, 'w_uq': 9.871556e-02, 'kv_norm_g': 5.092084e-01, 'w_ukv': 1.493075e-01, 'swa_sinks': 1.737853e-01, 'grp_a_g': 1.972417e-01, 'grp_b_g': 1.251184e-01, 'w_o': 1.569287e-01, 'attn_post_g': 7.955087e+00, 'ffn_pre_g': 1.398595e-01, 'w_gate': 5.316673e-02, 'w_up': 6.762617e-02, 'w_down': 1.114799e-01, 'ffn_post_g': 7.973731e+00}


def _to_microbatches(a, axis):
    t = _jnp.moveaxis(a, axis, 0)
    t = t.reshape((N_MICROBATCH, t.shape[0] // N_MICROBATCH) + t.shape[1:])
    return _jnp.moveaxis(t, 1, axis + 1)


def setup_inputs(seed: int = 0) -> dict:
    inp = _fwd_setup_inputs(seed)
    key = _jax.random.fold_in(_jax.random.key(seed), 7919)
    shape, _ = _output_shape()
    out = dict(inp)
    out["loss_target"] = _jax.random.normal(_jax.random.fold_in(key, 0), shape, _jnp.float32)
    for i, name in enumerate(TWIN_WEIGHTS):
        w = inp[name].astype(_jnp.float32)
        if MOMENT_SCALE is None:
            s = _jnp.sqrt(_jnp.mean(_jnp.square(w)) + 1e-30)
        else:
            s = MOMENT_SCALE[name]
        km, kv = _jax.random.split(_jax.random.fold_in(key, i + 1))
        out[name] = w
        out["m_" + name] = s * _jax.random.normal(km, w.shape, _jnp.float32)
        out["v_" + name] = (s * s) * _jax.random.uniform(kv, w.shape, _jnp.float32, 0.5, 1.5)
    if N_MICROBATCH > 1:
        for name, axis in PER_EXAMPLE_BATCH_AXIS.items():
            out[name] = _to_microbatches(out[name], axis)
    return {'x': out['x'], 'positions': out['positions'], 'attn_pre_g': out['attn_pre_g'], 'w_in': out['w_in'], 'q_norm_g': out['q_norm_g'], 'w_uq': out['w_uq'], 'kv_norm_g': out['kv_norm_g'], 'w_ukv': out['w_ukv'], 'swa_sinks': out['swa_sinks'], 'grp_a_g': out['grp_a_g'], 'grp_b_g': out['grp_b_g'], 'w_o': out['w_o'], 'attn_post_g': out['attn_post_g'], 'ffn_pre_g': out['ffn_pre_g'], 'w_gate': out['w_gate'], 'w_up': out['w_up'], 'w_down': out['w_down'], 'ffn_post_g': out['ffn_post_g'], 'loss_target': out['loss_target'], 'm_attn_pre_g': out['m_attn_pre_g'], 'm_w_in': out['m_w_in'], 'm_q_norm_g': out['m_q_norm_g'], 'm_w_uq': out['m_w_uq'], 'm_kv_norm_g': out['m_kv_norm_g'], 'm_w_ukv': out['m_w_ukv'], 'm_swa_sinks': out['m_swa_sinks'], 'm_grp_a_g': out['m_grp_a_g'], 'm_grp_b_g': out['m_grp_b_g'], 'm_w_o': out['m_w_o'], 'm_attn_post_g': out['m_attn_post_g'], 'm_ffn_pre_g': out['m_ffn_pre_g'], 'm_w_gate': out['m_w_gate'], 'm_w_up': out['m_w_up'], 'm_w_down': out['m_w_down'], 'm_ffn_post_g': out['m_ffn_post_g'], 'v_attn_pre_g': out['v_attn_pre_g'], 'v_w_in': out['v_w_in'], 'v_q_norm_g': out['v_q_norm_g'], 'v_w_uq': out['v_w_uq'], 'v_kv_norm_g': out['v_kv_norm_g'], 'v_w_ukv': out['v_w_ukv'], 'v_swa_sinks': out['v_swa_sinks'], 'v_grp_a_g': out['v_grp_a_g'], 'v_grp_b_g': out['v_grp_b_g'], 'v_w_o': out['v_w_o'], 'v_attn_post_g': out['v_attn_post_g'], 'v_ffn_pre_g': out['v_ffn_pre_g'], 'v_w_gate': out['v_w_gate'], 'v_w_up': out['v_w_up'], 'v_w_down': out['v_w_down'], 'v_ffn_post_g': out['v_ffn_post_g']}


def _loss(weights, diff, rest, loss_target):
    with _jax.named_scope("forward"):
        args = {**rest, TWIN_DIFF_INPUT: diff, **{k: w.astype(_WEIGHT_DTYPES[k]) for k, w in weights.items()}}
        y = _forward(args)
    with _jax.named_scope("loss_head"):
        err = _jnp.square(y.astype(_jnp.float32) - loss_target)
        return 0.5 * _jnp.sum(_jnp.mean(err, axis=-1)) if err.ndim else 0.5 * err


def _adamw(w, g, m, v):
    m = ADAM_B1 * m + (1.0 - ADAM_B1) * g
    v = ADAM_B2 * v + (1.0 - ADAM_B2) * _jnp.square(g)
    m_hat = m / (1.0 - ADAM_B1 ** ADAM_STEP)
    v_hat = v / (1.0 - ADAM_B2 ** ADAM_STEP)
    delta = -ADAM_LR * (m_hat / (_jnp.sqrt(v_hat) + ADAM_EPS) + ADAM_WD * w)
    return delta, m, v


def reference(x, positions, attn_pre_g, w_in, q_norm_g, w_uq, kv_norm_g, w_ukv, swa_sinks, grp_a_g, grp_b_g, w_o, attn_post_g, ffn_pre_g, w_gate, w_up, w_down, ffn_post_g, loss_target, m_attn_pre_g, m_w_in, m_q_norm_g, m_w_uq, m_kv_norm_g, m_w_ukv, m_swa_sinks, m_grp_a_g, m_grp_b_g, m_w_o, m_attn_post_g, m_ffn_pre_g, m_w_gate, m_w_up, m_w_down, m_ffn_post_g, v_attn_pre_g, v_w_in, v_q_norm_g, v_w_uq, v_kv_norm_g, v_w_ukv, v_swa_sinks, v_grp_a_g, v_grp_b_g, v_w_o, v_attn_post_g, v_ffn_pre_g, v_w_gate, v_w_up, v_w_down, v_ffn_post_g):
    given = dict(x=x, positions=positions, attn_pre_g=attn_pre_g, w_in=w_in, q_norm_g=q_norm_g, w_uq=w_uq, kv_norm_g=kv_norm_g, w_ukv=w_ukv, swa_sinks=swa_sinks, grp_a_g=grp_a_g, grp_b_g=grp_b_g, w_o=w_o, attn_post_g=attn_post_g, ffn_pre_g=ffn_pre_g, w_gate=w_gate, w_up=w_up, w_down=w_down, ffn_post_g=ffn_post_g, loss_target=loss_target, m_attn_pre_g=m_attn_pre_g, m_w_in=m_w_in, m_q_norm_g=m_q_norm_g, m_w_uq=m_w_uq, m_kv_norm_g=m_kv_norm_g, m_w_ukv=m_w_ukv, m_swa_sinks=m_swa_sinks, m_grp_a_g=m_grp_a_g, m_grp_b_g=m_grp_b_g, m_w_o=m_w_o, m_attn_post_g=m_attn_post_g, m_ffn_pre_g=m_ffn_pre_g, m_w_gate=m_w_gate, m_w_up=m_w_up, m_w_down=m_w_down, m_ffn_post_g=m_ffn_post_g, v_attn_pre_g=v_attn_pre_g, v_w_in=v_w_in, v_q_norm_g=v_q_norm_g, v_w_uq=v_w_uq, v_kv_norm_g=v_kv_norm_g, v_w_ukv=v_w_ukv, v_swa_sinks=v_swa_sinks, v_grp_a_g=v_grp_a_g, v_grp_b_g=v_grp_b_g, v_w_o=v_w_o, v_attn_post_g=v_attn_post_g, v_ffn_pre_g=v_ffn_pre_g, v_w_gate=v_w_gate, v_w_up=v_w_up, v_w_down=v_w_down, v_ffn_post_g=v_ffn_post_g)
    weights = {n: given[n] for n in TWIN_WEIGHTS}
    shared = {n: given[n] for n in SHARED_INPUTS}
    per_example = {n: given[n] for n in ['x', 'positions']}
    grad_fn = _jax.value_and_grad(_loss, argnums=(0, 1))

    def one_microbatch(ex, loss_target):
        ex = dict(ex)
        diff = ex.pop(TWIN_DIFF_INPUT)
        return grad_fn(weights, diff, {**shared, **ex}, loss_target)

    if N_MICROBATCH == 1:
        loss, (grad_w, grad_x) = one_microbatch(per_example, given["loss_target"])
    else:
        def body(carry, xs):
            loss_sum, grad_sum = carry
            l_k, (gw_k, gx_k) = one_microbatch(xs[0], xs[1])
            with _jax.named_scope("update"):
                return (loss_sum + l_k, _jax.tree.map(_jnp.add, grad_sum, gw_k)), gx_k

        init = (_jnp.zeros((), _jnp.float32), _jax.tree.map(_jnp.zeros_like, weights))
        (loss, grad_w), grad_x = _jax.lax.scan(body, init, (per_example, given["loss_target"]))
    with _jax.named_scope("update"):
        delta_w, new_m, new_v = {}, {}, {}
        for n in TWIN_WEIGHTS:
            delta_w[n], new_m[n], new_v[n] = _adamw(weights[n], grad_w[n], given["m_" + n], given["v_" + n])
    return (loss, grad_x, *[grad_w[n] for n in TWIN_WEIGHTS], *[delta_w[n] for n in TWIN_WEIGHTS],
            *[new_m[n] for n in TWIN_WEIGHTS], *[new_v[n] for n in TWIN_WEIGHTS])
```

```python
import functools
import math

import jax
import jax.numpy as jnp
from jax import lax
from jax.experimental import pallas as pl
from jax.experimental.pallas import tpu as pltpu

F32, BF16 = jnp.float32, jnp.bfloat16
MESH = pl.DeviceIdType.MESH
ANY = pl.BlockSpec(memory_space=pl.ANY)

N_CHIPS = 4
EPS = 1e-6
MLA_HEADS, MLA_NOPE, MLA_ROPE, MLA_V = 16, 128, 64, 128
MLA_QK = MLA_NOPE + MLA_ROPE
HEADS_PER_CHIP = MLA_HEADS // N_CHIPS
ROPE_THETA = 10000.0
SWA_Q_HEADS, SWA_KV_HEADS, SWA_HEAD_DIM, SWA_WINDOW = 32, 8, 64, 128
SWA_GROUP = SWA_Q_HEADS // SWA_KV_HEADS
MIX_A, MIX_B = MLA_HEADS * MLA_V, SWA_Q_HEADS * SWA_HEAD_DIM
MASK_VALUE = float(jnp.finfo(jnp.float32).min)
ADAM_LR, ADAM_B1, ADAM_B2, ADAM_EPS, ADAM_WD, ADAM_STEP = 0.001, 0.9, 0.999, 1e-08, 0.01, 10

LANES = 128
VMEM_LIMIT = 56 << 20
ATTN_TILE = 512
ROW_TILE = 256
MM_TILE = 1024


def _fit(dim, pref, mult=LANES):
    if dim <= pref:
        return dim
    for t in range(pref - pref % mult, 0, -mult):
        if dim % t == 0:
            return t
    return dim


def _params(*semantics):
    return pltpu.CompilerParams(dimension_semantics=semantics, vmem_limit_bytes=VMEM_LIMIT)


def _dot(a, b, ca, cb):
    return lax.dot_general(a, b, (((ca,), (cb,)), ((), ())), preferred_element_type=F32)


def _matmul(a, b, *, name, ta=False, tb=False, reduce_b=False, out_dtype=F32, tm=MM_TILE, tn=MM_TILE, tk=MM_TILE):
    a3, b3 = a.ndim == 3, b.ndim == 3
    nb = a.shape[0] if a3 else (b.shape[0] if b3 else 1)
    (K, M) = a.shape[-2:] if ta else a.shape[-2:][::-1]
    (N, K2) = b.shape[-2:] if tb else b.shape[-2:][::-1]
    assert K == K2, (a.shape, b.shape)
    tm, tn, tk = _fit(M, tm), _fit(N, tn), _fit(K, tk)
    batched_out = (a3 or b3) and not reduce_b
    n_bo = nb if batched_out else 1
    n_br = nb if reduce_b else 1
    nk = K // tk

    def sel(bo, br):
        return br if reduce_b else bo

    def a_map(bo, i, j, br, k):
        t = (k, i) if ta else (i, k)
        return (sel(bo, br),) + t if a3 else t

    def b_map(bo, i, j, br, k):
        t = (j, k) if tb else (k, j)
        return (sel(bo, br),) + t if b3 else t

    def o_map(bo, i, j, br, k):
        return (bo, i, j) if batched_out else (i, j)

    a_blk = (tk, tm) if ta else (tm, tk)
    b_blk = (tn, tk) if tb else (tk, tn)
    a_blk = (None,) + a_blk if a3 else a_blk
    b_blk = (None,) + b_blk if b3 else b_blk
    o_blk = (None, tm, tn) if batched_out else (tm, tn)
    o_shape = (nb, M, N) if batched_out else (M, N)

    def body(a_ref, b_ref, o_ref, acc_ref):
        br, k = pl.program_id(3), pl.program_id(4)

        @pl.when((br == 0) & (k == 0))
        def _():
            acc_ref[...] = jnp.zeros_like(acc_ref)

        acc_ref[...] += _dot(a_ref[...], b_ref[...], 0 if ta else 1, 1 if tb else 0)

        @pl.when((br == n_br - 1) & (k == nk - 1))
        def _():
            o_ref[...] = acc_ref[...].astype(o_ref.dtype)

    return pl.pallas_call(
        body, name=name, out_shape=jax.ShapeDtypeStruct(o_shape, out_dtype),
        grid=(n_bo, M // tm, N // tn, n_br, nk),
        in_specs=[pl.BlockSpec(a_blk, a_map), pl.BlockSpec(b_blk, b_map)],
        out_specs=pl.BlockSpec(o_blk, o_map),
        scratch_shapes=[pltpu.VMEM((tm, tn), F32)],
        compiler_params=_params("parallel", "parallel", "parallel", "arbitrary", "arbitrary"),
    )(a, b)


def _inv_rms(u):
    return lax.rsqrt(jnp.mean(u * u, axis=-1, keepdims=True) + EPS)


def _norm_bwd_math(u, g, dz):
    r = _inv_rms(u)
    w = dz * g
    du = r * w - u * (r * r * r * jnp.mean(w * u, axis=-1, keepdims=True))
    dg = jnp.sum(dz * (u * r), axis=0, keepdims=True)
    return du, dg


def _row_spec(tr, n):
    return pl.BlockSpec((tr, n), lambda i: (i, 0))


def _gain_spec(n):
    return pl.BlockSpec((1, n), lambda i: (0, 0))


def _norm_fwd(u, g, out_dtype, name):
    rows, n = u.shape
    tr = _fit(rows, ROW_TILE, 16)

    def body(u_ref, g_ref, o_ref):
        x = u_ref[...]
        o_ref[...] = (x * _inv_rms(x) * g_ref[...]).astype(o_ref.dtype)

    return pl.pallas_call(
        body, name=name, out_shape=jax.ShapeDtypeStruct((rows, n), out_dtype), grid=(rows // tr,),
        in_specs=[_row_spec(tr, n), _gain_spec(n)], out_specs=_row_spec(tr, n), compiler_params=_params("parallel"),
    )(u, g)


def _norm_bwd(u, g, dzs, adds, out_dtype, name):
    rows, n = u.shape
    tr = _fit(rows, ROW_TILE, 16)
    n_dz, n_add = len(dzs), len(adds)

    def body(*refs):
        u_ref, g_ref = refs[:2]
        dz_refs = refs[2:2 + n_dz]
        add_refs = refs[2 + n_dz:2 + n_dz + n_add]
        du_ref, dg_ref = refs[2 + n_dz + n_add:]
        dz = dz_refs[0][...].astype(F32)
        for r in dz_refs[1:]:
            dz = dz + r[...].astype(F32)
        du, dg = _norm_bwd_math(u_ref[...], g_ref[...], dz)
        for r in add_refs:
            du = du + r[...]
        du_ref[...] = du.astype(du_ref.dtype)

        @pl.when(pl.program_id(0) == 0)
        def _():
            dg_ref[...] = jnp.zeros_like(dg_ref)

        dg_ref[...] += dg

    return pl.pallas_call(
        body, name=name,
        out_shape=(jax.ShapeDtypeStruct((rows, n), out_dtype), jax.ShapeDtypeStruct((1, n), F32)), grid=(rows // tr,),
        in_specs=[_row_spec(tr, n), _gain_spec(n)] + [_row_spec(tr, n)] * (n_dz + n_add),
        out_specs=(_row_spec(tr, n), _gain_spec(n)), compiler_params=_params("arbitrary"),
    )(u, g, *dzs, *adds)


def _mix_fwd(o_a, o_b, g_a, g_b):
    rows = o_a.shape[0]
    tr = _fit(rows, ROW_TILE, 16)

    def body(a_ref, b_ref, ga_ref, gb_ref, o_ref):
        a, b = a_ref[...], b_ref[...]
        o_ref[:, :MIX_A] = (a * _inv_rms(a) * ga_ref[...]).astype(BF16)
        o_ref[:, MIX_A:] = (b * _inv_rms(b) * gb_ref[...]).astype(BF16)

    return pl.pallas_call(
        body, name="mix_fwd", out_shape=jax.ShapeDtypeStruct((rows, MIX_A + MIX_B), BF16), grid=(rows // tr,),
        in_specs=[_row_spec(tr, MIX_A), _row_spec(tr, MIX_B), _gain_spec(MIX_A), _gain_spec(MIX_B)],
        out_specs=_row_spec(tr, MIX_A + MIX_B), compiler_params=_params("parallel"),
    )(o_a, o_b, g_a, g_b)


def _mix_bwd(o_a, o_b, g_a, g_b, dmix):
    rows = o_a.shape[0]
    tr = _fit(rows, ROW_TILE, 16)

    def body(a_ref, b_ref, ga_ref, gb_ref, dm_ref, doa_ref, dob_ref, dga_ref, dgb_ref, dla_ref, dlb_ref):
        a, b = a_ref[...], b_ref[...]
        doa, dga = _norm_bwd_math(a, ga_ref[...], dm_ref[:, :MIX_A])
        dob, dgb = _norm_bwd_math(b, gb_ref[...], dm_ref[:, MIX_A:])
        doa_ref[...] = doa.astype(BF16)
        dob_ref[...] = dob.astype(BF16)
        pa, pb = doa * a, dob * b
        for h in range(MLA_HEADS):
            dla_ref[:, h:h + 1] = jnp.sum(pa[:, h * MLA_V:(h + 1) * MLA_V], axis=-1, keepdims=True)
        for h in range(SWA_Q_HEADS):
            dlb_ref[:, h:h + 1] = jnp.sum(pb[:, h * SWA_HEAD_DIM:(h + 1) * SWA_HEAD_DIM], axis=-1, keepdims=True)

        @pl.when(pl.program_id(0) == 0)
        def _():
            dga_ref[...] = jnp.zeros_like(dga_ref)
            dgb_ref[...] = jnp.zeros_like(dgb_ref)

        dga_ref[...] += dga
        dgb_ref[...] += dgb

    return pl.pallas_call(
        body, name="mix_bwd",
        out_shape=(jax.ShapeDtypeStruct((rows, MIX_A), BF16), jax.ShapeDtypeStruct((rows, MIX_B), BF16),
                   jax.ShapeDtypeStruct((1, MIX_A), F32), jax.ShapeDtypeStruct((1, MIX_B), F32),
                   jax.ShapeDtypeStruct((rows, MLA_HEADS), F32), jax.ShapeDtypeStruct((rows, SWA_Q_HEADS), F32)),
        grid=(rows // tr,),
        in_specs=[_row_spec(tr, MIX_A), _row_spec(tr, MIX_B), _gain_spec(MIX_A), _gain_spec(MIX_B),
                  _row_spec(tr, MIX_A + MIX_B)],
        out_specs=(_row_spec(tr, MIX_A), _row_spec(tr, MIX_B), _gain_spec(MIX_A), _gain_spec(MIX_B),
                   _row_spec(tr, MLA_HEADS), _row_spec(tr, SWA_Q_HEADS)),
        compiler_params=_params("arbitrary"),
    )(o_a, o_b, g_a, g_b, dmix)


def _post_pre_fwd(x, ao, g_post, g_pre):
    rows, n = x.shape
    tr = _fit(rows, ROW_TILE, 16)

    def body(x_ref, ao_ref, g1_ref, g2_ref, h_ref, f_ref):
        u = ao_ref[...]
        h = x_ref[...] + u * _inv_rms(u) * g1_ref[...]
        h_ref[...] = h
        f_ref[...] = (h * _inv_rms(h) * g2_ref[...]).astype(BF16)

    return pl.pallas_call(
        body, name="post_pre_fwd",
        out_shape=(jax.ShapeDtypeStruct((rows, n), F32), jax.ShapeDtypeStruct((rows, n), BF16)), grid=(rows // tr,),
        in_specs=[_row_spec(tr, n), _row_spec(tr, n), _gain_spec(n), _gain_spec(n)],
        out_specs=(_row_spec(tr, n), _row_spec(tr, n)), compiler_params=_params("parallel"),
    )(x, ao, g_post, g_pre)


def _loss_bwd(h1, dn, g_post, target):
    rows, n = h1.shape
    tr = _fit(rows, ROW_TILE, 16)

    def body(h_ref, u_ref, g_ref, t_ref, loss_ref, dy_ref, du_ref, dg_ref):
        u, g = u_ref[...], g_ref[...]
        err = h_ref[...] + u * _inv_rms(u) * g - t_ref[...]
        dy = err / n
        dy_ref[...] = dy
        du, dg = _norm_bwd_math(u, g, dy)
        du_ref[...] = du.astype(BF16)

        @pl.when(pl.program_id(0) == 0)
        def _():
            loss_ref[...] = jnp.zeros_like(loss_ref)
            dg_ref[...] = jnp.zeros_like(dg_ref)

        loss_ref[...] += jnp.full((1, LANES), 0.5 * jnp.sum(jnp.mean(err * err, axis=-1)), F32)
        dg_ref[...] += dg

    return pl.pallas_call(
        body, name="loss_bwd",
        out_shape=(jax.ShapeDtypeStruct((1, LANES), F32), jax.ShapeDtypeStruct((rows, n), F32),
                   jax.ShapeDtypeStruct((rows, n), BF16), jax.ShapeDtypeStruct((1, n), F32)),
        grid=(rows // tr,),
        in_specs=[_row_spec(tr, n), _row_spec(tr, n), _gain_spec(n), _row_spec(tr, n)],
        out_specs=(_gain_spec(LANES), _row_spec(tr, n), _row_spec(tr, n), _gain_spec(n)),
        compiler_params=_params("arbitrary"),
    )(h1, dn, g_post, target)


def _blk3_spec(tr, n):
    return pl.BlockSpec((None, tr, n), lambda b, i: (b, i, 0))


def _swiglu_fwd(gate, up):
    nb, rows, n = gate.shape
    tr = _fit(rows, ROW_TILE, 16)

    def body(g_ref, u_ref, o_ref):
        g = g_ref[...]
        o_ref[...] = (g * jax.nn.sigmoid(g) * u_ref[...]).astype(BF16)

    return pl.pallas_call(
        body, name="swiglu_fwd", out_shape=jax.ShapeDtypeStruct(gate.shape, BF16), grid=(nb, rows // tr),
        in_specs=[_blk3_spec(tr, n)] * 2, out_specs=_blk3_spec(tr, n), compiler_params=_params("parallel", "parallel"),
    )(gate, up)


def _swiglu_bwd(gate, up, dact):
    nb, rows, n = gate.shape
    tr = _fit(rows, ROW_TILE, 16)

    def body(g_ref, u_ref, d_ref, dg_ref, du_ref):
        g, d = g_ref[...], d_ref[...]
        sig = jax.nn.sigmoid(g)
        silu = g * sig
        du_ref[...] = (d * silu).astype(BF16)
        dg_ref[...] = (d * u_ref[...] * (sig * (1.0 + g * (1.0 - sig)))).astype(BF16)

    return pl.pallas_call(
        body, name="swiglu_bwd",
        out_shape=(jax.ShapeDtypeStruct(gate.shape, BF16), jax.ShapeDtypeStruct(gate.shape, BF16)), grid=(nb, rows // tr),
        in_specs=[_blk3_spec(tr, n)] * 3, out_specs=(_blk3_spec(tr, n),) * 2,
        compiler_params=_params("parallel", "parallel"),
    )(gate, up, dact)


def _rope(x, cos, sin):
    half = MLA_ROPE // 2
    x1, x2 = x[:, :half], x[:, half:]
    return jnp.concatenate([x1 * cos - x2 * sin, x2 * cos + x1 * sin], axis=-1)


def _rope_t(d, cos, sin):
    half = MLA_ROPE // 2
    d1, d2 = d[:, :half], d[:, half:]
    return jnp.concatenate([d1 * cos + d2 * sin, d2 * cos - d1 * sin], axis=-1)


def _mla_prep(q4, kv4, k_rope, cos, sin):
    nb, rows, _ = q4.shape
    tr = _fit(rows, ROW_TILE, 16)
    hpc, half = HEADS_PER_CHIP, MLA_ROPE // 2

    def body(q_ref, kv_ref, kr_ref, c_ref, s_ref, qo_ref, ko_ref, vo_ref):
        cos, sin = c_ref[...], s_ref[...]
        k_pe = _rope(kr_ref[...], cos, sin)
        q_all, kv_all = q_ref[...], kv_ref[...]
        for h in range(hpc):
            q = q_all[:, h * MLA_QK:(h + 1) * MLA_QK]
            qo_ref[h] = jnp.concatenate([q[:, :MLA_NOPE], _rope(q[:, MLA_NOPE:], cos, sin)], axis=-1).astype(BF16)
            kv = kv_all[:, h * (MLA_NOPE + MLA_V):(h + 1) * (MLA_NOPE + MLA_V)]
            ko_ref[h] = jnp.concatenate([kv[:, :MLA_NOPE], k_pe], axis=-1).astype(BF16)
            vo_ref[h] = kv[:, MLA_NOPE:].astype(BF16)

    def head_spec(n):
        return pl.BlockSpec((hpc, tr, n), lambda b, i: (b, i, 0))

    def row_spec(n):
        return pl.BlockSpec((tr, n), lambda b, i: (i, 0))

    return pl.pallas_call(
        body, name="mla_prep",
        out_shape=(jax.ShapeDtypeStruct((MLA_HEADS, rows, MLA_QK), BF16), jax.ShapeDtypeStruct((MLA_HEADS, rows, MLA_QK), BF16),
                   jax.ShapeDtypeStruct((MLA_HEADS, rows, MLA_V), BF16)),
        grid=(nb, rows // tr),
        in_specs=[_blk3_spec(tr, hpc * MLA_QK), _blk3_spec(tr, hpc * (MLA_NOPE + MLA_V)), row_spec(MLA_ROPE),
                  row_spec(half), row_spec(half)],
        out_specs=(head_spec(MLA_QK), head_spec(MLA_QK), head_spec(MLA_V)),
        compiler_params=_params("parallel", "parallel"),
    )(q4, kv4, k_rope, cos, sin)


def _mla_unprep(dq, dk, dv, cos, sin):
    _, rows, _ = dq.shape
    tr = _fit(rows, ROW_TILE, 16)
    hpc, half = HEADS_PER_CHIP, MLA_ROPE // 2

    def body(dq_ref, dk_ref, dv_ref, c_ref, s_ref, q4_ref, kv4_ref, kr_ref):
        cos, sin = c_ref[...], s_ref[...]
        d_pe = jnp.zeros((tr, MLA_ROPE), F32)
        q_parts, kv_parts = [], []
        for h in range(hpc):
            g, gk = dq_ref[h], dk_ref[h]
            q_parts += [g[:, :MLA_NOPE], _rope_t(g[:, MLA_NOPE:], cos, sin)]
            kv_parts += [gk[:, :MLA_NOPE], dv_ref[h]]
            d_pe = d_pe + gk[:, MLA_NOPE:]
        q4_ref[...] = jnp.concatenate(q_parts, axis=-1).astype(BF16)
        kv4_ref[...] = jnp.concatenate(kv_parts, axis=-1).astype(BF16)

        @pl.when(pl.program_id(1) == 0)
        def _():
            kr_ref[...] = jnp.zeros_like(kr_ref)

        kr_ref[...] += _rope_t(d_pe, cos, sin)

    def head_spec(n):
        return pl.BlockSpec((hpc, tr, n), lambda i, b: (b, i, 0))

    def row_spec(n):
        return pl.BlockSpec((tr, n), lambda i, b: (i, 0))

    def blk_spec(n):
        return pl.BlockSpec((None, tr, n), lambda i, b: (b, i, 0))

    return pl.pallas_call(
        body, name="mla_unprep",
        out_shape=(jax.ShapeDtypeStruct((N_CHIPS, rows, hpc * MLA_QK), BF16),
                   jax.ShapeDtypeStruct((N_CHIPS, rows, hpc * (MLA_NOPE + MLA_V)), BF16),
                   jax.ShapeDtypeStruct((rows, MLA_ROPE), F32)),
        grid=(rows // tr, N_CHIPS),
        in_specs=[head_spec(MLA_QK), head_spec(MLA_QK), head_spec(MLA_V), row_spec(half), row_spec(half)],
        out_specs=(blk_spec(hpc * MLA_QK), blk_spec(hpc * (MLA_NOPE + MLA_V)), row_spec(MLA_ROPE)),
        compiler_params=_params("parallel", "arbitrary"),
    )(dq, dk, dv, cos, sin)


def _causal_keep(t):
    return lax.broadcasted_iota(jnp.int32, (t, t), 1) <= lax.broadcasted_iota(jnp.int32, (t, t), 0)


def _mla_fwd(q, k, v):
    nh, rows, _ = q.shape
    t = _fit(rows, ATTN_TILE)
    scale = MLA_QK ** -0.5

    def body(q_ref, k_ref, v_ref, o_ref, lse_ref):
        i = pl.program_id(1)
        qb = q_ref[...]

        def step(j, carry, diagonal):
            m, l, acc = carry
            rows_j = pl.ds(pl.multiple_of(j * t, t), t)
            s = _dot(qb, k_ref[rows_j, :], 1, 1) * scale
            if diagonal:
                s = jnp.where(_causal_keep(t), s, MASK_VALUE)
            m_new = jnp.maximum(m, jnp.max(s, axis=-1, keepdims=True))
            alpha = jnp.exp(m - m_new)
            p = jnp.exp(s - m_new)
            l = alpha * l + jnp.sum(p, axis=-1, keepdims=True)
            acc = alpha * acc + _dot(p.astype(BF16), v_ref[rows_j, :], 1, 0)
            return m_new, l, acc

        init = (jnp.full((t, 1), MASK_VALUE, F32), jnp.zeros((t, 1), F32), jnp.zeros((t, MLA_V), F32))
        carry = lax.fori_loop(0, i, lambda j, c: step(j, c, False), init)
        m, l, acc = step(i, carry, True)
        o_ref[...] = acc / l
        lse_ref[...] = m + jnp.log(l)

    return pl.pallas_call(
        body, name="mla_fwd",
        out_shape=(jax.ShapeDtypeStruct((rows, nh * MLA_V), F32), jax.ShapeDtypeStruct((nh, rows, 1), F32)),
        grid=(nh, rows // t),
        in_specs=[pl.BlockSpec((None, t, MLA_QK), lambda h, i: (h, i, 0)),
                  pl.BlockSpec((None, rows, MLA_QK), lambda h, i: (h, 0, 0)),
                  pl.BlockSpec((None, rows, MLA_V), lambda h, i: (h, 0, 0))],
        out_specs=(pl.BlockSpec((t, MLA_V), lambda h, i: (i, h)), pl.BlockSpec((None, t, 1), lambda h, i: (h, i, 0))),
        compiler_params=_params("parallel", "arbitrary"),
    )(q, k, v)


def _mla_bwd_dq(q, k, v, do, lse, delta):
    nh, rows, _ = q.shape
    t = _fit(rows, ATTN_TILE)
    scale = MLA_QK ** -0.5

    def body(q_ref, k_ref, v_ref, do_ref, lse_ref, dl_ref, dq_ref):
        i = pl.program_id(1)
        qb, dob, lse_b, dl_b = q_ref[...], do_ref[...], lse_ref[...], dl_ref[...]

        def step(j, dq, diagonal):
            rows_j = pl.ds(pl.multiple_of(j * t, t), t)
            kb = k_ref[rows_j, :]
            s = _dot(qb, kb, 1, 1) * scale
            if diagonal:
                s = jnp.where(_causal_keep(t), s, MASK_VALUE)
            p = jnp.exp(s - lse_b)
            dp = _dot(dob, v_ref[rows_j, :], 1, 1)
            ds = p * (dp - dl_b) * scale
            return dq + _dot(ds.astype(BF16), kb, 1, 0)

        dq = lax.fori_loop(0, i, lambda j, c: step(j, c, False), jnp.zeros((t, MLA_QK), F32))
        dq_ref[...] = step(i, dq, True)

    return pl.pallas_call(
        body, name="mla_bwd_dq", out_shape=jax.ShapeDtypeStruct((nh, rows, MLA_QK), F32), grid=(nh, rows // t),
        in_specs=[pl.BlockSpec((None, t, MLA_QK), lambda h, i: (h, i, 0)),
                  pl.BlockSpec((None, rows, MLA_QK), lambda h, i: (h, 0, 0)),
                  pl.BlockSpec((None, rows, MLA_V), lambda h, i: (h, 0, 0)),
                  pl.BlockSpec((t, MLA_V), lambda h, i: (i, h)),
                  pl.BlockSpec((None, t, 1), lambda h, i: (h, i, 0)),
                  pl.BlockSpec((None, t, 1), lambda h, i: (h, i, 0))],
        out_specs=pl.BlockSpec((None, t, MLA_QK), lambda h, i: (h, i, 0)),
        compiler_params=_params("parallel", "arbitrary"),
    )(q, k, v, do, lse, delta)


def _mla_bwd_dkv(q, k, v, do, lse, delta):
    nh, rows, _ = q.shape
    t = _fit(rows, ATTN_TILE)
    n_t = rows // t
    scale = MLA_QK ** -0.5

    def body(q_ref, k_ref, v_ref, do_ref, lse_ref, dl_ref, dk_ref, dv_ref):
        j = pl.program_id(1)
        kb, vb = k_ref[...], v_ref[...]

        def step(i, carry, diagonal):
            dk, dv = carry
            rows_i = pl.ds(pl.multiple_of(i * t, t), t)
            qb, dob = q_ref[rows_i, :], do_ref[rows_i, :]
            s = _dot(qb, kb, 1, 1) * scale
            if diagonal:
                s = jnp.where(_causal_keep(t), s, MASK_VALUE)
            p = jnp.exp(s - lse_ref[rows_i, :])
            dv = dv + _dot(p.astype(BF16), dob, 0, 0)
            dp = _dot(dob, vb, 1, 1)
            ds = p * (dp - dl_ref[rows_i, :]) * scale
            dk = dk + _dot(ds.astype(BF16), qb, 0, 0)
            return dk, dv

        carry = step(j, (jnp.zeros((t, MLA_QK), F32), jnp.zeros((t, MLA_V), F32)), True)
        dk, dv = lax.fori_loop(j + 1, n_t, lambda i, c: step(i, c, False), carry)
        dk_ref[...] = dk
        dv_ref[...] = dv

    return pl.pallas_call(
        body, name="mla_bwd_dkv",
        out_shape=(jax.ShapeDtypeStruct((nh, rows, MLA_QK), F32), jax.ShapeDtypeStruct((nh, rows, MLA_V), F32)),
        grid=(nh, n_t),
        in_specs=[pl.BlockSpec((None, rows, MLA_QK), lambda h, j: (h, 0, 0)),
                  pl.BlockSpec((None, t, MLA_QK), lambda h, j: (h, j, 0)),
                  pl.BlockSpec((None, t, MLA_V), lambda h, j: (h, j, 0)),
                  pl.BlockSpec((rows, MLA_V), lambda h, j: (0, h)),
                  pl.BlockSpec((None, rows, 1), lambda h, j: (h, 0, 0)),
                  pl.BlockSpec((None, rows, 1), lambda h, j: (h, 0, 0))],
        out_specs=(pl.BlockSpec((None, t, MLA_QK), lambda h, j: (h, j, 0)), pl.BlockSpec((None, t, MLA_V), lambda h, j: (h, j, 0))),
        compiler_params=_params("parallel", "arbitrary"),
    )(q, k, v, do, lse, delta)


def _swa_slope(h):
    return 2.0 ** (-8.0 * (h + 1) / SWA_Q_HEADS)


def _swa_band_specs(rows):
    w = SWA_WINDOW
    kvw = SWA_KV_HEADS * SWA_HEAD_DIM

    def prev(i):
        return jnp.maximum(i - 1, 0)

    return [pl.BlockSpec((w, kvw), lambda i: (prev(i), 0)), pl.BlockSpec((w, kvw), lambda i: (i, 0)),
            pl.BlockSpec((w, kvw), lambda i: (prev(i), 0)), pl.BlockSpec((w, kvw), lambda i: (i, 0)),
            pl.BlockSpec((w, 1), lambda i: (i, 0)),
            pl.BlockSpec((1, w), lambda i: (0, prev(i))), pl.BlockSpec((1, w), lambda i: (0, i)),
            pl.BlockSpec(memory_space=pltpu.SMEM)]


def _swa_bias(i, pc_ref, pp_ref, pn_ref):
    w = SWA_WINDOW
    k_pos = jnp.concatenate([pp_ref[...], pn_ref[...]], axis=1)
    dist = jnp.abs(pc_ref[...] - k_pos).astype(F32)
    r = lax.broadcasted_iota(jnp.int32, (w, 2 * w), 0)
    col = lax.broadcasted_iota(jnp.int32, (w, 2 * w), 1)
    delta = r + w - col
    valid = (delta >= 0) & (delta < w) & ((col >= w) | (i > 0))
    return dist, valid


def _swa_fwd(q, k, v, pos_col, pos_row, sinks):
    rows = q.shape[0]
    w, hd = SWA_WINDOW, SWA_HEAD_DIM
    scale = hd ** -0.5

    def body(q_ref, kp_ref, kc_ref, vp_ref, vc_ref, pc_ref, pp_ref, pn_ref, sink_ref, o_ref, lse_ref):
        dist, valid = _swa_bias(pl.program_id(0), pc_ref, pp_ref, pn_ref)
        for kvh in range(SWA_KV_HEADS):
            cols = slice(kvh * hd, (kvh + 1) * hd)
            kb = jnp.concatenate([kp_ref[:, cols], kc_ref[:, cols]], axis=0)
            vb = jnp.concatenate([vp_ref[:, cols], vc_ref[:, cols]], axis=0)
            for g in range(SWA_GROUP):
                h = kvh * SWA_GROUP + g
                sink = sink_ref[h]
                s = _dot(q_ref[:, h * hd:(h + 1) * hd], kb, 1, 1) * scale - _swa_slope(h) * dist
                s = jnp.where(valid, s, MASK_VALUE)
                m = jnp.maximum(jnp.max(s, axis=-1, keepdims=True), sink)
                e = jnp.exp(s - m)
                den = jnp.sum(e, axis=-1, keepdims=True) + jnp.exp(sink - m)
                o_ref[:, h * hd:(h + 1) * hd] = _dot((e / den).astype(BF16), vb, 1, 0)
                lse_ref[:, h:h + 1] = m + jnp.log(den)

    return pl.pallas_call(
        body, name="swa_fwd",
        out_shape=(jax.ShapeDtypeStruct((rows, MIX_B), F32), jax.ShapeDtypeStruct((rows, SWA_Q_HEADS), F32)),
        grid=(rows // w,),
        in_specs=[pl.BlockSpec((w, MIX_B), lambda i: (i, 0))] + _swa_band_specs(rows),
        out_specs=(pl.BlockSpec((w, MIX_B), lambda i: (i, 0)), pl.BlockSpec((w, SWA_Q_HEADS), lambda i: (i, 0))),
        compiler_params=_params("parallel"),
    )(q, k, k, v, v, pos_col, pos_row, pos_row, sinks)


def _swa_bwd(q, k, v, pos_col, pos_row, sinks, do, lse, delta):
    rows = q.shape[0]
    w, hd = SWA_WINDOW, SWA_HEAD_DIM
    kvw = SWA_KV_HEADS * hd
    scale = hd ** -0.5

    def body(q_ref, kp_ref, kc_ref, vp_ref, vc_ref, pc_ref, pp_ref, pn_ref, sink_ref, do_ref, lse_ref, dl_ref,
             dq_ref, dkc_ref, dkp_ref, dvc_ref, dvp_ref, dsink_ref):
        dist, valid = _swa_bias(pl.program_id(0), pc_ref, pp_ref, pn_ref)

        @pl.when(pl.program_id(0) == 0)
        def _():
            dsink_ref[...] = jnp.zeros_like(dsink_ref)

        for kvh in range(SWA_KV_HEADS):
            cols = slice(kvh * hd, (kvh + 1) * hd)
            kb = jnp.concatenate([kp_ref[:, cols], kc_ref[:, cols]], axis=0)
            vb = jnp.concatenate([vp_ref[:, cols], vc_ref[:, cols]], axis=0)
            dk = jnp.zeros((2 * w, hd), F32)
            dv = jnp.zeros((2 * w, hd), F32)
            for g in range(SWA_GROUP):
                h = kvh * SWA_GROUP + g
                hc = slice(h * hd, (h + 1) * hd)
                qb, dob = q_ref[:, hc], do_ref[:, hc]
                lse_h, dl_h = lse_ref[:, h:h + 1], dl_ref[:, h:h + 1]
                s = _dot(qb, kb, 1, 1) * scale - _swa_slope(h) * dist
                s = jnp.where(valid, s, MASK_VALUE)
                p = jnp.exp(s - lse_h)
                ds = (p * (_dot(dob, vb, 1, 1) - dl_h) * scale).astype(BF16)
                dq_ref[:, hc] = _dot(ds, kb, 1, 0).astype(BF16)
                dk = dk + _dot(ds, qb, 0, 0)
                dv = dv + _dot(p.astype(BF16), dob, 0, 0)
                dsink_ref[:, h:h + 1] += -jnp.sum(jnp.exp(sink_ref[h] - lse_h) * dl_h, axis=0, keepdims=True)
            dkp_ref[:, cols] = dk[:w]
            dkc_ref[:, cols] = dk[w:]
            dvp_ref[:, cols] = dv[:w]
            dvc_ref[:, cols] = dv[w:]

    def blk(n):
        return pl.BlockSpec((w, n), lambda i: (i, 0))

    return pl.pallas_call(
        body, name="swa_bwd",
        out_shape=(jax.ShapeDtypeStruct((rows, MIX_B), BF16),) + (jax.ShapeDtypeStruct((rows, kvw), F32),) * 4
        + (jax.ShapeDtypeStruct((1, SWA_Q_HEADS), F32),),
        grid=(rows // w,),
        in_specs=[blk(MIX_B)] + _swa_band_specs(rows) + [blk(MIX_B), blk(SWA_Q_HEADS), blk(SWA_Q_HEADS)],
        out_specs=(blk(MIX_B), blk(kvw), blk(kvw), blk(kvw), blk(kvw), pl.BlockSpec((1, SWA_Q_HEADS), lambda i: (0, 0))),
        compiler_params=_params("arbitrary"),
    )(q, k, k, v, v, pos_col, pos_row, pos_row, sinks, do, lse, delta)


def _band_merge(cur, prev, name):
    rows, n = cur.shape
    w = SWA_WINDOW
    last = rows // w - 1

    def body(c_ref, p_ref, o_ref):
        nxt = jnp.where(pl.program_id(0) < last, p_ref[...], 0.0)
        o_ref[...] = (c_ref[...] + nxt).astype(BF16)

    return pl.pallas_call(
        body, name=name, out_shape=jax.ShapeDtypeStruct((rows, n), BF16), grid=(rows // w,),
        in_specs=[pl.BlockSpec((w, n), lambda j: (j, 0)), pl.BlockSpec((w, n), lambda j: (jnp.minimum(j + 1, last), 0))],
        out_specs=pl.BlockSpec((w, n), lambda j: (j, 0)), compiler_params=_params("parallel"),
    )(cur, prev)


def _adamw(w, g, m, v, name):
    rows, cols = w.shape
    tr, tc = _fit(rows, ROW_TILE, 8), _fit(cols, MM_TILE)
    c1 = 1.0 - ADAM_B1 ** ADAM_STEP
    c2 = 1.0 - ADAM_B2 ** ADAM_STEP

    def body(w_ref, g_ref, m_ref, v_ref, d_ref, mo_ref, vo_ref):
        gr = g_ref[...]
        m_new = ADAM_B1 * m_ref[...] + (1.0 - ADAM_B1) * gr
        v_new = ADAM_B2 * v_ref[...] + (1.0 - ADAM_B2) * jnp.square(gr)
        mo_ref[...] = m_new
        vo_ref[...] = v_new
        d_ref[...] = -ADAM_LR * ((m_new / c1) / (jnp.sqrt(v_new / c2) + ADAM_EPS) + ADAM_WD * w_ref[...])

    spec = pl.BlockSpec((tr, tc), lambda i, j: (i, j))
    return pl.pallas_call(
        body, name=name, out_shape=(jax.ShapeDtypeStruct(w.shape, F32),) * 3, grid=(rows // tr, cols // tc),
        in_specs=[spec] * 4, out_specs=(spec,) * 3, compiler_params=_params("parallel", "parallel"),
    )(w, g, m, v)


OTHER_CHIPS = ((1, 0), (0, 1), (1, 1))


def _place():
    x, y, c = lax.axis_index("x"), lax.axis_index("y"), lax.axis_index("c")
    return x, y, c


def _flip(v, f):
    return 1 - v if f else v


def _all_gather_weights(shards):
    n = len(shards)

    def body(*refs):
        src, dst = refs[:n], refs[n:2 * n]
        send, recv, fsend, frecv, local = refs[2 * n:]
        x, y, c = _place()
        me = 2 * x + y
        own = [pltpu.make_async_copy(src[a], dst[a].at[me], local.at[a]) for a in range(n)]
        for cp in own:
            cp.start()
        sent = []
        for a in range(n):
            half = shards[a].shape[0] // 2
            rows = pl.ds(c * half, half)
            for j, (fx, fy) in enumerate(OTHER_CHIPS):
                cp = pltpu.make_async_remote_copy(
                    src_ref=src[a].at[rows], dst_ref=dst[a].at[me, rows], send_sem=send.at[a * 3 + j],
                    recv_sem=recv.at[a * 3 + j], device_id=(_flip(x, fx), _flip(y, fy), c), device_id_type=MESH)
                cp.start()
                sent.append(cp)
        passed = []
        for a in range(n):
            half = shards[a].shape[0] // 2
            rows = pl.ds(c * half, half)
            for j, (fx, fy) in enumerate(OTHER_CHIPS):
                blk = 2 * _flip(x, fx) + _flip(y, fy)
                pltpu.make_async_remote_copy(
                    src_ref=src[a].at[rows], dst_ref=dst[a].at[blk, rows], send_sem=send.at[a * 3 + j],
                    recv_sem=recv.at[a * 3 + j], device_id=(x, y, c), device_id_type=MESH).wait_recv()
                cp = pltpu.make_async_remote_copy(
                    src_ref=dst[a].at[blk, rows], dst_ref=dst[a].at[blk, rows], send_sem=fsend.at[a * 3 + j],
                    recv_sem=frecv.at[a * 3 + j], device_id=(x, y, 1 - c), device_id_type=MESH)
                cp.start()
                passed.append(cp)
        for a in range(n):
            half = shards[a].shape[0] // 2
            theirs = pl.ds((1 - c) * half, half)
            for j, (fx, fy) in enumerate(OTHER_CHIPS):
                blk = 2 * _flip(x, fx) + _flip(y, fy)
                pltpu.make_async_remote_copy(
                    src_ref=dst[a].at[blk, theirs], dst_ref=dst[a].at[blk, theirs], send_sem=fsend.at[a * 3 + j],
                    recv_sem=frecv.at[a * 3 + j], device_id=(x, y, 1 - c), device_id_type=MESH).wait_recv()
        for cp in sent + passed:
            cp.wait_send()
        for cp in own:
            cp.wait()

    return pl.pallas_call(
        body, name="all_gather_weights",
        out_shape=tuple(jax.ShapeDtypeStruct((N_CHIPS,) + s.shape, s.dtype) for s in shards),
        in_specs=[ANY] * n, out_specs=(ANY,) * n,
        scratch_shapes=[pltpu.SemaphoreType.DMA((3 * n,))] * 4 + [pltpu.SemaphoreType.DMA((n,))],
    )(*shards)


def _pair_exchange(grads):
    n = len(grads)

    def body(*refs):
        src, dst = refs[:n], refs[n:2 * n]
        send, recv = refs[2 * n:]
        x, y, c = _place()
        cps = []
        for a in range(n):
            half = grads[a].shape[1] // 2
            cp = pltpu.make_async_remote_copy(
                src_ref=src[a].at[:, pl.ds((1 - c) * half, half)], dst_ref=dst[a], send_sem=send.at[a],
                recv_sem=recv.at[a], device_id=(x, y, 1 - c), device_id_type=MESH)
            cp.start()
            cps.append(cp)
        for cp in cps:
            cp.wait()

    return pl.pallas_call(
        body, name="grad_pair_exchange",
        out_shape=tuple(jax.ShapeDtypeStruct((g.shape[0], g.shape[1] // 2, g.shape[2]), g.dtype) for g in grads),
        in_specs=[ANY] * n, out_specs=(ANY,) * n, scratch_shapes=[pltpu.SemaphoreType.DMA((n,))] * 2,
    )(*grads)


def _pair_add(place, mine, theirs, name):
    nb, rows, cols = theirs.shape
    tr, tc = _fit(rows, 704, 16), _fit(cols, MM_TILE)
    n_r = rows // tr

    def body(place_ref, a_ref, b_ref, lo_ref, own_ref):
        s = a_ref[...] + b_ref[...]
        lo_ref[...] = s.astype(BF16)

        @pl.when(pl.program_id(2) == place_ref[0])
        def _():
            own_ref[...] = s

    grid_spec = pltpu.PrefetchScalarGridSpec(
        num_scalar_prefetch=1, grid=(n_r, cols // tc, nb),
        in_specs=[pl.BlockSpec((None, tr, tc), lambda i, j, b, p: (b, p[1] * n_r + i, j)),
                  pl.BlockSpec((None, tr, tc), lambda i, j, b, p: (b, i, j))],
        out_specs=(pl.BlockSpec((None, tr, tc), lambda i, j, b, p: (b, i, j)), pl.BlockSpec((tr, tc), lambda i, j, b, p: (i, j))))
    return pl.pallas_call(
        body, name=name, grid_spec=grid_spec,
        out_shape=(jax.ShapeDtypeStruct(theirs.shape, BF16), jax.ShapeDtypeStruct((rows, cols), F32)),
        compiler_params=_params("parallel", "parallel", "arbitrary"),
    )(place, mine, theirs)


def _chip_exchange(sums):
    n = len(sums)

    def body(*refs):
        src, dst = refs[:n], refs[n:2 * n]
        send, recv = refs[2 * n:]
        x, y, c = _place()
        cps = []
        for a in range(n):
            for j, (fx, fy) in enumerate(OTHER_CHIPS):
                px, py = _flip(x, fx), _flip(y, fy)
                cp = pltpu.make_async_remote_copy(
                    src_ref=src[a].at[2 * px + py], dst_ref=dst[a].at[j], send_sem=send.at[a * 3 + j],
                    recv_sem=recv.at[a * 3 + j], device_id=(px, py, c), device_id_type=MESH)
                cp.start()
                cps.append(cp)
        for cp in cps:
            cp.wait()

    return pl.pallas_call(
        body, name="grad_chip_exchange",
        out_shape=tuple(jax.ShapeDtypeStruct((3,) + s.shape[1:], s.dtype) for s in sums),
        in_specs=[ANY] * n, out_specs=(ANY,) * n, scratch_shapes=[pltpu.SemaphoreType.DMA((3 * n,))] * 2,
    )(*sums)


def _chip_add(own, got, name):
    rows, cols = own.shape
    tr, tc = _fit(rows, 704, 16), _fit(cols, MM_TILE)

    def body(o_ref, g_ref, out_ref):
        out_ref[...] = ((o_ref[...] + g_ref[0].astype(F32)) + g_ref[1].astype(F32)) + g_ref[2].astype(F32)

    return pl.pallas_call(
        body, name=name, out_shape=jax.ShapeDtypeStruct(own.shape, F32), grid=(rows // tr, cols // tc),
        in_specs=[pl.BlockSpec((tr, tc), lambda i, j: (i, j)), pl.BlockSpec((3, tr, tc), lambda i, j: (0, i, j))],
        out_specs=pl.BlockSpec((tr, tc), lambda i, j: (i, j)), compiler_params=_params("parallel", "parallel"),
    )(own, got)


def _pair_share(halves):
    n = len(halves)

    def body(*refs):
        src, dst = refs[:n], refs[n:2 * n]
        send, recv, local = refs[2 * n:]
        x, y, c = _place()
        cps, own = [], []
        for a in range(n):
            half = halves[a].shape[0]
            rows = pl.ds(c * half, half)
            lc = pltpu.make_async_copy(src[a], dst[a].at[rows], local.at[a])
            lc.start()
            own.append(lc)
            cp = pltpu.make_async_remote_copy(
                src_ref=src[a], dst_ref=dst[a].at[rows], send_sem=send.at[a], recv_sem=recv.at[a],
                device_id=(x, y, 1 - c), device_id_type=MESH)
            cp.start()
            cps.append(cp)
        for a in range(n):
            half = halves[a].shape[0]
            theirs = pl.ds((1 - c) * half, half)
            pltpu.make_async_remote_copy(
                src_ref=src[a], dst_ref=dst[a].at[theirs], send_sem=send.at[a], recv_sem=recv.at[a],
                device_id=(x, y, 1 - c), device_id_type=MESH).wait_recv()
        for cp in cps:
            cp.wait_send()
        for lc in own:
            lc.wait()

    return pl.pallas_call(
        body, name="grad_pair_share",
        out_shape=tuple(jax.ShapeDtypeStruct((2 * h.shape[0], h.shape[1]), h.dtype) for h in halves),
        in_specs=[ANY] * n, out_specs=(ANY,) * n, scratch_shapes=[pltpu.SemaphoreType.DMA((n,))] * 3,
    )(*halves)


def _all_sum_small(vec):
    r, n = vec.shape
    flips = [(a, b, d) for a in (0, 1) for b in (0, 1) for d in (0, 1)][1:]

    def body(v_ref, o_ref, buf, send, recv):
        x, y, c = _place()
        me = 4 * x + 2 * y + c
        cps = []
        for k, (fx, fy, fc) in enumerate(flips):
            cp = pltpu.make_async_remote_copy(
                src_ref=v_ref, dst_ref=buf.at[me], send_sem=send.at[k], recv_sem=recv.at[k],
                device_id=(_flip(x, fx), _flip(y, fy), _flip(c, fc)), device_id_type=MESH)
            cp.start()
            cps.append(cp)
        buf[me] = v_ref[...]
        for k, (fx, fy, fc) in enumerate(flips):
            peer = 4 * _flip(x, fx) + 2 * _flip(y, fy) + _flip(c, fc)
            pltpu.make_async_remote_copy(
                src_ref=v_ref, dst_ref=buf.at[peer], send_sem=send.at[k], recv_sem=recv.at[k],
                device_id=(x, y, c), device_id_type=MESH).wait_recv()
        for cp in cps:
            cp.wait_send()
        acc = buf[0]
        for d in range(1, 8):
            acc = acc + buf[d]
        o_ref[...] = acc

    return pl.pallas_call(
        body, name="all_sum_small", out_shape=jax.ShapeDtypeStruct((r, n), F32),
        in_specs=[pl.BlockSpec(memory_space=pltpu.VMEM)], out_specs=pl.BlockSpec(memory_space=pltpu.VMEM),
        scratch_shapes=[pltpu.VMEM((8, r, n), F32), pltpu.SemaphoreType.DMA((7,)), pltpu.SemaphoreType.DMA((7,))],
    )(vec)


SMALL = ("attn_pre_g", "q_norm_g", "kv_norm_g", "swa_sinks", "grp_a_g", "grp_b_g", "attn_post_g", "ffn_pre_g", "ffn_post_g")
BIG = ("w_in", "w_uq", "w_ukv", "w_o", "w_gate", "w_up", "w_down")
ORDER = ("attn_pre_g", "w_in", "q_norm_g", "w_uq", "kv_norm_g", "w_ukv", "swa_sinks", "grp_a_g", "grp_b_g", "w_o",
         "attn_post_g", "ffn_pre_g", "w_gate", "w_up", "w_down", "ffn_post_g")


def _pad_lanes(v):
    n = v.shape[1]
    return jnp.pad(v, ((0, 0), (0, -n % LANES)))


def kernel(x, positions, attn_pre_g, w_in, q_norm_g, w_uq, kv_norm_g, w_ukv, swa_sinks, grp_a_g, grp_b_g, w_o, attn_post_g, ffn_pre_g, w_gate, w_up, w_down, ffn_post_g, loss_target, m_attn_pre_g, m_w_in, m_q_norm_g, m_w_uq, m_kv_norm_g, m_w_ukv, m_swa_sinks, m_grp_a_g, m_grp_b_g, m_w_o, m_attn_post_g, m_ffn_pre_g, m_w_gate, m_w_up, m_w_down, m_ffn_post_g, v_attn_pre_g, v_w_in, v_q_norm_g, v_w_uq, v_kv_norm_g, v_w_ukv, v_swa_sinks, v_grp_a_g, v_grp_b_g, v_w_o, v_attn_post_g, v_ffn_pre_g, v_w_gate, v_w_up, v_w_down, v_ffn_post_g):
    given = dict(locals())
    w32 = {k: given[k][0] for k in BIG}
    gains = {k: given[k] for k in SMALL}
    xs, tgt = x[0], loss_target[0]
    seq, d_model = xs.shape
    q_rank, kv_rank = q_norm_g.shape[1], kv_norm_g.shape[1]
    kvw = SWA_KV_HEADS * SWA_HEAD_DIM

    full = dict(zip(BIG, _all_gather_weights([w32[k].astype(BF16) for k in BIG])))

    pos = positions[0]
    inv = 1.0 / (ROPE_THETA ** (jnp.arange(0, MLA_ROPE, 2, dtype=F32) / MLA_ROPE))
    ang = pos.astype(F32)[:, None] * inv
    cos, sin = jnp.cos(ang), jnp.sin(ang)
    pos_col, pos_row = pos[:, None], pos[None, :]
    sinks = swa_sinks[0]

    a = _norm_fwd(xs, attn_pre_g, BF16, "attn_pre_norm")
    proj4 = _matmul(a, full["w_in"], name="in_proj")
    proj = jnp.concatenate([proj4[b] for b in range(N_CHIPS)], axis=1)
    cuts = (0, q_rank, q_rank + kv_rank, q_rank + kv_rank + MLA_ROPE)
    cuts = cuts + (cuts[3] + MIX_B, cuts[3] + MIX_B + kvw, cuts[3] + MIX_B + 2 * kvw)
    c_q, c_kv, k_rope, q_s, k_s, v_s = (proj[:, lo:hi] for lo, hi in zip(cuts[:-1], cuts[1:]))
    cqn = _norm_fwd(c_q, q_norm_g, BF16, "q_norm")
    ckvn = _norm_fwd(c_kv, kv_norm_g, BF16, "kv_norm")
    q4 = _matmul(cqn, full["w_uq"], name="q_up")
    kv4 = _matmul(ckvn, full["w_ukv"], name="kv_up")
    qh, kh, vh = _mla_prep(q4, kv4, k_rope, cos, sin)
    o_a, lse_a = _mla_fwd(qh, kh, vh)
    q_sb, k_sb, v_sb = q_s.astype(BF16), k_s.astype(BF16), v_s.astype(BF16)
    o_b, lse_b = _swa_fwd(q_sb, k_sb, v_sb, pos_col, pos_row, sinks)
    mix = _mix_fwd(o_a, o_b, grp_a_g, grp_b_g)
    w_o_full = full["w_o"].reshape(N_CHIPS * full["w_o"].shape[1], d_model)
    ao = _matmul(mix, w_o_full, name="out_proj")
    h1, f = _post_pre_fwd(xs, ao, attn_post_g, ffn_pre_g)
    gate = _matmul(f, full["w_gate"], name="ffn_gate")
    up = _matmul(f, full["w_up"], name="ffn_up")
    act = _swiglu_fwd(gate, up)
    dn = _matmul(act, full["w_down"], reduce_b=True, name="ffn_down")
    loss_row, dy, ddn, d_ffn_post = _loss_bwd(h1, dn, ffn_post_g, tgt)

    dact = _matmul(ddn, full["w_down"], tb=True, name="ffn_down_dx")
    dw_down = _matmul(act, ddn, ta=True, name="ffn_down_dw", tn=512)
    dgate, dup = _swiglu_bwd(gate, up, dact)
    dw_gate = _matmul(f, dgate, ta=True, name="ffn_gate_dw", tm=512)
    dw_up = _matmul(f, dup, ta=True, name="ffn_up_dw", tm=512)
    df_g = _matmul(dgate, full["w_gate"], tb=True, reduce_b=True, name="ffn_gate_dx")
    df_u = _matmul(dup, full["w_up"], tb=True, reduce_b=True, name="ffn_up_dx")
    dh1, d_ffn_pre = _norm_bwd(h1, ffn_pre_g, [df_g, df_u], [dy], F32, "ffn_pre_norm_bwd")
    dao, d_attn_post = _norm_bwd(ao, attn_post_g, [dh1], [], BF16, "attn_post_norm_bwd")
    dmix = _matmul(dao, w_o_full, tb=True, name="out_proj_dx")
    dw_o = _matmul(mix, dao, ta=True, name="out_proj_dw")
    do_a, do_b, d_grp_a, d_grp_b, dl_a, dl_b = _mix_bwd(o_a, o_b, grp_a_g, grp_b_g, dmix)
    dl_a = dl_a.T[:, :, None]
    dqh = _mla_bwd_dq(qh, kh, vh, do_a, lse_a, dl_a)
    dkh, dvh = _mla_bwd_dkv(qh, kh, vh, do_a, lse_a, dl_a)
    dq4, dkv4, dk_rope = _mla_unprep(dqh, dkh, dvh, cos, sin)
    dw_uq = _matmul(cqn, dq4, ta=True, name="q_up_dw")
    dcqn = _matmul(dq4, full["w_uq"], tb=True, reduce_b=True, name="q_up_dx")
    dw_ukv = _matmul(ckvn, dkv4, ta=True, name="kv_up_dw")
    dckvn = _matmul(dkv4, full["w_ukv"], tb=True, reduce_b=True, name="kv_up_dx")
    dc_q, d_q_norm = _norm_bwd(c_q, q_norm_g, [dcqn], [], BF16, "q_norm_bwd")
    dc_kv, d_kv_norm = _norm_bwd(c_kv, kv_norm_g, [dckvn], [], BF16, "kv_norm_bwd")
    dq_s, dk_cur, dk_prev, dv_cur, dv_prev, d_sinks = _swa_bwd(q_sb, k_sb, v_sb, pos_col, pos_row, sinks, do_b, lse_b, dl_b)
    dk_s = _band_merge(dk_cur, dk_prev, "swa_dk_merge")
    dv_s = _band_merge(dv_cur, dv_prev, "swa_dv_merge")
    dproj = jnp.concatenate([dc_q, dc_kv, dk_rope.astype(BF16), dq_s, dk_s, dv_s], axis=1)
    blk_w = dproj.shape[1] // N_CHIPS
    dproj4 = jnp.stack([dproj[:, b * blk_w:(b + 1) * blk_w] for b in range(N_CHIPS)])
    dw_in = _matmul(a, dproj4, ta=True, name="in_proj_dw", tm=512)
    da = _matmul(dproj4, full["w_in"], tb=True, reduce_b=True, name="in_proj_dx")
    dx, d_attn_pre = _norm_bwd(xs, attn_pre_g, [da], [dh1], F32, "attn_pre_norm_bwd")

    small_grads = dict(attn_pre_g=d_attn_pre, q_norm_g=d_q_norm, kv_norm_g=d_kv_norm, swa_sinks=d_sinks, grp_a_g=d_grp_a,
                       grp_b_g=d_grp_b, attn_post_g=d_attn_post, ffn_pre_g=d_ffn_pre, ffn_post_g=d_ffn_post)
    parts = [loss_row] + [_pad_lanes(small_grads[k]) for k in SMALL]
    packed = jnp.concatenate(parts, axis=1)
    n_packed = packed.shape[1]
    packed = jnp.pad(packed, ((0, 0), (0, -n_packed % (8 * LANES)))).reshape(8, -1)
    total = _all_sum_small(packed).reshape(1, -1)
    loss = total[0, 0]
    g_small, off = {}, LANES
    for k in SMALL:
        n = gains[k].shape[1]
        g_small[k] = total[:, off:off + n]
        off += n + (-n % LANES)

    def pack_small(prefix):
        return jnp.concatenate([_pad_lanes(given[prefix + k]) for k in SMALL], axis=1)

    d_sm, m_sm, v_sm = _adamw(pack_small(""), total[:, LANES:n_packed], pack_small("m_"), pack_small("v_"), "adamw_small")
    delta, new_m, new_v, off = {}, {}, {}, 0
    for k in SMALL:
        n = gains[k].shape[1]
        delta[k], new_m[k], new_v[k] = d_sm[:, off:off + n], m_sm[:, off:off + n], v_sm[:, off:off + n]
        off += n + (-n % LANES)

    grads4 = dict(w_in=dw_in, w_uq=dw_uq, w_ukv=dw_ukv, w_o=dw_o.reshape(N_CHIPS, -1, d_model), w_gate=dw_gate,
                  w_up=dw_up, w_down=dw_down)
    x_i, y_i, c_i = _place()
    place = jnp.stack([2 * x_i + y_i, c_i]).astype(jnp.int32)
    theirs = _pair_exchange([grads4[k] for k in BIG])
    lows, owns = zip(*[_pair_add(place, grads4[k], t, "pair_add_" + k) for k, t in zip(BIG, theirs)])
    got = _chip_exchange(list(lows))
    halves = [_chip_add(o, g, "chip_add_" + k) for k, o, g in zip(BIG, owns, got)]
    g_big = dict(zip(BIG, _pair_share(halves)))
    for k in BIG:
        delta[k], new_m[k], new_v[k] = _adamw(w32[k], g_big[k], given["m_" + k][0], given["v_" + k][0], "adamw_" + k)

    def out(d, k):
        return d[k][None] if k in BIG else d[k]

    grads = {**g_small, **g_big}
    return (loss, dx[None], *[out(grads, k) for k in ORDER], *[out(delta, k) for k in ORDER],
            *[out(new_m, k) for k in ORDER], *[out(new_v, k) for k in ORDER])
```

```python
import functools
import math

import jax
import jax.numpy as jnp
from jax import lax
from jax.experimental import pallas as pl
from jax.experimental.pallas import tpu as pltpu

F32, BF16 = jnp.float32, jnp.bfloat16
MESH = pl.DeviceIdType.MESH
ANY = pl.BlockSpec(memory_space=pl.ANY)

N_CHIPS = 4
EPS = 1e-6
MLA_HEADS, MLA_NOPE, MLA_ROPE, MLA_V = 16, 128, 64, 128
MLA_QK = MLA_NOPE + MLA_ROPE
HEADS_PER_CHIP = MLA_HEADS // N_CHIPS
ROPE_THETA = 10000.0
SWA_Q_HEADS, SWA_KV_HEADS, SWA_HEAD_DIM, SWA_WINDOW = 32, 8, 64, 128
SWA_GROUP = SWA_Q_HEADS // SWA_KV_HEADS
MIX_A, MIX_B = MLA_HEADS * MLA_V, SWA_Q_HEADS * SWA_HEAD_DIM
MASK_VALUE = float(jnp.finfo(jnp.float32).min)
ADAM_LR, ADAM_B1, ADAM_B2, ADAM_EPS, ADAM_WD, ADAM_STEP = 0.001, 0.9, 0.999, 1e-08, 0.01, 10

LANES = 128
VMEM_LIMIT = 56 << 20
ATTN_TILE = 512
ROW_TILE = 256
MM_TILE = 1024


def _fit(dim, pref, mult=LANES):
    if dim <= pref:
        return dim
    for t in range(pref - pref % mult, 0, -mult):
        if dim % t == 0:
            return t
    return dim


def _params(*semantics):
    return pltpu.CompilerParams(dimension_semantics=semantics, vmem_limit_bytes=VMEM_LIMIT)


def _dot(a, b, ca, cb):
    return lax.dot_general(a, b, (((ca,), (cb,)), ((), ())), preferred_element_type=F32)


def _matmul(a, b, *, name, ta=False, tb=False, reduce_b=False, out_dtype=F32, tm=MM_TILE, tn=MM_TILE, tk=MM_TILE):
    a3, b3 = a.ndim == 3, b.ndim == 3
    nb = a.shape[0] if a3 else (b.shape[0] if b3 else 1)
    (K, M) = a.shape[-2:] if ta else a.shape[-2:][::-1]
    (N, K2) = b.shape[-2:] if tb else b.shape[-2:][::-1]
    assert K == K2, (a.shape, b.shape)
    tm, tn, tk = _fit(M, tm), _fit(N, tn), _fit(K, tk)
    batched_out = (a3 or b3) and not reduce_b
    n_bo = nb if batched_out else 1
    n_br = nb if reduce_b else 1
    nk = K // tk

    def sel(bo, br):
        return br if reduce_b else bo

    def a_map(bo, i, j, br, k):
        t = (k, i) if ta else (i, k)
        return (sel(bo, br),) + t if a3 else t

    def b_map(bo, i, j, br, k):
        t = (j, k) if tb else (k, j)
        return (sel(bo, br),) + t if b3 else t

    def o_map(bo, i, j, br, k):
        return (bo, i, j) if batched_out else (i, j)

    a_blk = (tk, tm) if ta else (tm, tk)
    b_blk = (tn, tk) if tb else (tk, tn)
    a_blk = (None,) + a_blk if a3 else a_blk
    b_blk = (None,) + b_blk if b3 else b_blk
    o_blk = (None, tm, tn) if batched_out else (tm, tn)
    o_shape = (nb, M, N) if batched_out else (M, N)

    def body(a_ref, b_ref, o_ref, acc_ref):
        br, k = pl.program_id(3), pl.program_id(4)

        @pl.when((br == 0) & (k == 0))
        def _():
            acc_ref[...] = jnp.zeros_like(acc_ref)

        acc_ref[...] += _dot(a_ref[...], b_ref[...], 0 if ta else 1, 1 if tb else 0)

        @pl.when((br == n_br - 1) & (k == nk - 1))
        def _():
            o_ref[...] = acc_ref[...].astype(o_ref.dtype)

    return pl.pallas_call(
        body, name=name, out_shape=jax.ShapeDtypeStruct(o_shape, out_dtype),
        grid=(n_bo, M // tm, N // tn, n_br, nk),
        in_specs=[pl.BlockSpec(a_blk, a_map), pl.BlockSpec(b_blk, b_map)],
        out_specs=pl.BlockSpec(o_blk, o_map),
        scratch_shapes=[pltpu.VMEM((tm, tn), F32)],
        compiler_params=_params("parallel", "parallel", "parallel", "arbitrary", "arbitrary"),
    )(a, b)


def _inv_rms(u):
    return lax.rsqrt(jnp.mean(u * u, axis=-1, keepdims=True) + EPS)


def _norm_bwd_math(u, g, dz):
    r = _inv_rms(u)
    w = dz * g
    du = r * w - u * (r * r * r * jnp.mean(w * u, axis=-1, keepdims=True))
    dg = jnp.sum(dz * (u * r), axis=0, keepdims=True)
    return du, dg


def _row_spec(tr, n):
    return pl.BlockSpec((tr, n), lambda i: (i, 0))


def _gain_spec(n):
    return pl.BlockSpec((1, n), lambda i: (0, 0))


def _norm_fwd(u, g, out_dtype, name):
    rows, n = u.shape
    tr = _fit(rows, ROW_TILE, 16)

    def body(u_ref, g_ref, o_ref):
        x = u_ref[...]
        o_ref[...] = (x * _inv_rms(x) * g_ref[...]).astype(o_ref.dtype)

    return pl.pallas_call(
        body, name=name, out_shape=jax.ShapeDtypeStruct((rows, n), out_dtype), grid=(rows // tr,),
        in_specs=[_row_spec(tr, n), _gain_spec(n)], out_specs=_row_spec(tr, n), compiler_params=_params("parallel"),
    )(u, g)


def _norm_bwd(u, g, dzs, adds, out_dtype, name):
    rows, n = u.shape
    tr = _fit(rows, ROW_TILE, 16)
    n_dz, n_add = len(dzs), len(adds)

    def body(*refs):
        u_ref, g_ref = refs[:2]
        dz_refs = refs[2:2 + n_dz]
        add_refs = refs[2 + n_dz:2 + n_dz + n_add]
        du_ref, dg_ref = refs[2 + n_dz + n_add:]
        dz = dz_refs[0][...].astype(F32)
        for r in dz_refs[1:]:
            dz = dz + r[...].astype(F32)
        du, dg = _norm_bwd_math(u_ref[...], g_ref[...], dz)
        for r in add_refs:
            du = du + r[...]
        du_ref[...] = du.astype(du_ref.dtype)

        @pl.when(pl.program_id(0) == 0)
        def _():
            dg_ref[...] = jnp.zeros_like(dg_ref)

        dg_ref[...] += dg

    return pl.pallas_call(
        body, name=name,
        out_shape=(jax.ShapeDtypeStruct((rows, n), out_dtype), jax.ShapeDtypeStruct((1, n), F32)), grid=(rows // tr,),
        in_specs=[_row_spec(tr, n), _gain_spec(n)] + [_row_spec(tr, n)] * (n_dz + n_add),
        out_specs=(_row_spec(tr, n), _gain_spec(n)), compiler_params=_params("arbitrary"),
    )(u, g, *dzs, *adds)


def _mix_fwd(o_a, o_b, g_a, g_b):
    rows = o_a.shape[0]
    tr = _fit(rows, ROW_TILE, 16)

    def body(a_ref, b_ref, ga_ref, gb_ref, o_ref):
        a, b = a_ref[...], b_ref[...]
        o_ref[:, :MIX_A] = (a * _inv_rms(a) * ga_ref[...]).astype(BF16)
        o_ref[:, MIX_A:] = (b * _inv_rms(b) * gb_ref[...]).astype(BF16)

    return pl.pallas_call(
        body, name="mix_fwd", out_shape=jax.ShapeDtypeStruct((rows, MIX_A + MIX_B), BF16), grid=(rows // tr,),
        in_specs=[_row_spec(tr, MIX_A), _row_spec(tr, MIX_B), _gain_spec(MIX_A), _gain_spec(MIX_B)],
        out_specs=_row_spec(tr, MIX_A + MIX_B), compiler_params=_params("parallel"),
    )(o_a, o_b, g_a, g_b)


def _mix_bwd(o_a, o_b, g_a, g_b, dmix):
    rows = o_a.shape[0]
    tr = _fit(rows, ROW_TILE, 16)

    def body(a_ref, b_ref, ga_ref, gb_ref, dm_ref, doa_ref, dob_ref, dga_ref, dgb_ref, dla_ref, dlb_ref):
        a, b = a_ref[...], b_ref[...]
        doa, dga = _norm_bwd_math(a, ga_ref[...], dm_ref[:, :MIX_A])
        dob, dgb = _norm_bwd_math(b, gb_ref[...], dm_ref[:, MIX_A:])
        doa_ref[...] = doa.astype(BF16)
        dob_ref[...] = dob.astype(BF16)
        pa, pb = doa * a, dob * b
        for h in range(MLA_HEADS):
            dla_ref[:, h:h + 1] = jnp.sum(pa[:, h * MLA_V:(h + 1) * MLA_V], axis=-1, keepdims=True)
        for h in range(SWA_Q_HEADS):
            dlb_ref[:, h:h + 1] = jnp.sum(pb[:, h * SWA_HEAD_DIM:(h + 1) * SWA_HEAD_DIM], axis=-1, keepdims=True)

        @pl.when(pl.program_id(0) == 0)
        def _():
            dga_ref[...] = jnp.zeros_like(dga_ref)
            dgb_ref[...] = jnp.zeros_like(dgb_ref)

        dga_ref[...] += dga
        dgb_ref[...] += dgb

    return pl.pallas_call(
        body, name="mix_bwd",
        out_shape=(jax.ShapeDtypeStruct((rows, MIX_A), BF16), jax.ShapeDtypeStruct((rows, MIX_B), BF16),
                   jax.ShapeDtypeStruct((1, MIX_A), F32), jax.ShapeDtypeStruct((1, MIX_B), F32),
                   jax.ShapeDtypeStruct((rows, MLA_HEADS), F32), jax.ShapeDtypeStruct((rows, SWA_Q_HEADS), F32)),
        grid=(rows // tr,),
        in_specs=[_row_spec(tr, MIX_A), _row_spec(tr, MIX_B), _gain_spec(MIX_A), _gain_spec(MIX_B),
                  _row_spec(tr, MIX_A + MIX_B)],
        out_specs=(_row_spec(tr, MIX_A), _row_spec(tr, MIX_B), _gain_spec(MIX_A), _gain_spec(MIX_B),
                   _row_spec(tr, MLA_HEADS), _row_spec(tr, SWA_Q_HEADS)),
        compiler_params=_params("arbitrary"),
    )(o_a, o_b, g_a, g_b, dmix)


def _post_pre_fwd(x, ao, g_post, g_pre):
    rows, n = x.shape
    tr = _fit(rows, ROW_TILE, 16)

    def body(x_ref, ao_ref, g1_ref, g2_ref, h_ref, f_ref):
        u = ao_ref[...]
        h = x_ref[...] + u * _inv_rms(u) * g1_ref[...]
        h_ref[...] = h
        f_ref[...] = (h * _inv_rms(h) * g2_ref[...]).astype(BF16)

    return pl.pallas_call(
        body, name="post_pre_fwd",
        out_shape=(jax.ShapeDtypeStruct((rows, n), F32), jax.ShapeDtypeStruct((rows, n), BF16)), grid=(rows // tr,),
        in_specs=[_row_spec(tr, n), _row_spec(tr, n), _gain_spec(n), _gain_spec(n)],
        out_specs=(_row_spec(tr, n), _row_spec(tr, n)), compiler_params=_params("parallel"),
    )(x, ao, g_post, g_pre)


def _loss_bwd(h1, dn, g_post, target):
    rows, n = h1.shape
    tr = _fit(rows, ROW_TILE, 16)

    def body(h_ref, u_ref, g_ref, t_ref, loss_ref, dy_ref, du_ref, dg_ref):
        u, g = u_ref[...], g_ref[...]
        err = h_ref[...] + u * _inv_rms(u) * g - t_ref[...]
        dy = err / n
        dy_ref[...] = dy
        du, dg = _norm_bwd_math(u, g, dy)
        du_ref[...] = du.astype(BF16)

        @pl.when(pl.program_id(0) == 0)
        def _():
            loss_ref[...] = jnp.zeros_like(loss_ref)
            dg_ref[...] = jnp.zeros_like(dg_ref)

        loss_ref[...] += jnp.full((1, LANES), 0.5 * jnp.sum(jnp.mean(err * err, axis=-1)), F32)
        dg_ref[...] += dg

    return pl.pallas_call(
        body, name="loss_bwd",
        out_shape=(jax.ShapeDtypeStruct((1, LANES), F32), jax.ShapeDtypeStruct((rows, n), F32),
                   jax.ShapeDtypeStruct((rows, n), BF16), jax.ShapeDtypeStruct((1, n), F32)),
        grid=(rows // tr,),
        in_specs=[_row_spec(tr, n), _row_spec(tr, n), _gain_spec(n), _row_spec(tr, n)],
        out_specs=(_gain_spec(LANES), _row_spec(tr, n), _row_spec(tr, n), _gain_spec(n)),
        compiler_params=_params("arbitrary"),
    )(h1, dn, g_post, target)


def _blk3_spec(tr, n):
    return pl.BlockSpec((None, tr, n), lambda b, i: (b, i, 0))


def _swiglu_fwd(gate, up):
    nb, rows, n = gate.shape
    tr = _fit(rows, ROW_TILE, 16)

    def body(g_ref, u_ref, o_ref):
        g = g_ref[...]
        o_ref[...] = (g * jax.nn.sigmoid(g) * u_ref[...]).astype(BF16)

    return pl.pallas_call(
        body, name="swiglu_fwd", out_shape=jax.ShapeDtypeStruct(gate.shape, BF16), grid=(nb, rows // tr),
        in_specs=[_blk3_spec(tr, n)] * 2, out_specs=_blk3_spec(tr, n), compiler_params=_params("parallel", "parallel"),
    )(gate, up)


def _swiglu_bwd(gate, up, dact):
    nb, rows, n = gate.shape
    tr = _fit(rows, ROW_TILE, 16)

    def body(g_ref, u_ref, d_ref, dg_ref, du_ref):
        g, d = g_ref[...], d_ref[...]
        sig = jax.nn.sigmoid(g)
        silu = g * sig
        du_ref[...] = (d * silu).astype(BF16)
        dg_ref[...] = (d * u_ref[...] * (sig * (1.0 + g * (1.0 - sig)))).astype(BF16)

    return pl.pallas_call(
        body, name="swiglu_bwd",
        out_shape=(jax.ShapeDtypeStruct(gate.shape, BF16), jax.ShapeDtypeStruct(gate.shape, BF16)), grid=(nb, rows // tr),
        in_specs=[_blk3_spec(tr, n)] * 3, out_specs=(_blk3_spec(tr, n),) * 2,
        compiler_params=_params("parallel", "parallel"),
    )(gate, up, dact)


def _rope(x, cos, sin):
    half = MLA_ROPE // 2
    x1, x2 = x[:, :half], x[:, half:]
    return jnp.concatenate([x1 * cos - x2 * sin, x2 * cos + x1 * sin], axis=-1)


def _rope_t(d, cos, sin):
    half = MLA_ROPE // 2
    d1, d2 = d[:, :half], d[:, half:]
    return jnp.concatenate([d1 * cos + d2 * sin, d2 * cos - d1 * sin], axis=-1)


def _mla_prep(q4, kv4, k_rope, cos, sin):
    nb, rows, _ = q4.shape
    tr = _fit(rows, ROW_TILE, 16)
    hpc, half = HEADS_PER_CHIP, MLA_ROPE // 2

    def body(q_ref, kv_ref, kr_ref, c_ref, s_ref, qo_ref, ko_ref, vo_ref):
        cos, sin = c_ref[...], s_ref[...]
        k_pe = _rope(kr_ref[...], cos, sin)
        q_all, kv_all = q_ref[...], kv_ref[...]
        for h in range(hpc):
            q = q_all[:, h * MLA_QK:(h + 1) * MLA_QK]
            qo_ref[h] = jnp.concatenate([q[:, :MLA_NOPE], _rope(q[:, MLA_NOPE:], cos, sin)], axis=-1).astype(BF16)
            kv = kv_all[:, h * (MLA_NOPE + MLA_V):(h + 1) * (MLA_NOPE + MLA_V)]
            ko_ref[h] = jnp.concatenate([kv[:, :MLA_NOPE], k_pe], axis=-1).astype(BF16)
            vo_ref[h] = kv[:, MLA_NOPE:].astype(BF16)

    def head_spec(n):
        return pl.BlockSpec((hpc, tr, n), lambda b, i: (b, i, 0))

    def row_spec(n):
        return pl.BlockSpec((tr, n), lambda b, i: (i, 0))

    return pl.pallas_call(
        body, name="mla_prep",
        out_shape=(jax.ShapeDtypeStruct((MLA_HEADS, rows, MLA_QK), BF16), jax.ShapeDtypeStruct((MLA_HEADS, rows, MLA_QK), BF16),
                   jax.ShapeDtypeStruct((MLA_HEADS, rows, MLA_V), BF16)),
        grid=(nb, rows // tr),
        in_specs=[_blk3_spec(tr, hpc * MLA_QK), _blk3_spec(tr, hpc * (MLA_NOPE + MLA_V)), row_spec(MLA_ROPE),
                  row_spec(half), row_spec(half)],
        out_specs=(head_spec(MLA_QK), head_spec(MLA_QK), head_spec(MLA_V)),
        compiler_params=_params("parallel", "parallel"),
    )(q4, kv4, k_rope, cos, sin)


def _mla_unprep(dq, dk, dv, cos, sin):
    _, rows, _ = dq.shape
    tr = _fit(rows, ROW_TILE, 16)
    hpc, half = HEADS_PER_CHIP, MLA_ROPE // 2

    def body(dq_ref, dk_ref, dv_ref, c_ref, s_ref, q4_ref, kv4_ref, kr_ref):
        cos, sin = c_ref[...], s_ref[...]
        d_pe = jnp.zeros((tr, MLA_ROPE), F32)
        q_parts, kv_parts = [], []
        for h in range(hpc):
            g, gk = dq_ref[h], dk_ref[h]
            q_parts += [g[:, :MLA_NOPE], _rope_t(g[:, MLA_NOPE:], cos, sin)]
            kv_parts += [gk[:, :MLA_NOPE], dv_ref[h]]
            d_pe = d_pe + gk[:, MLA_NOPE:]
        q4_ref[...] = jnp.concatenate(q_parts, axis=-1).astype(BF16)
        kv4_ref[...] = jnp.concatenate(kv_parts, axis=-1).astype(BF16)

        @pl.when(pl.program_id(1) == 0)
        def _():
            kr_ref[...] = jnp.zeros_like(kr_ref)

        kr_ref[...] += _rope_t(d_pe, cos, sin)

    def head_spec(n):
        return pl.BlockSpec((hpc, tr, n), lambda i, b: (b, i, 0))

    def row_spec(n):
        return pl.BlockSpec((tr, n), lambda i, b: (i, 0))

    def blk_spec(n):
        return pl.BlockSpec((None, tr, n), lambda i, b: (b, i, 0))

    return pl.pallas_call(
        body, name="mla_unprep",
        out_shape=(jax.ShapeDtypeStruct((N_CHIPS, rows, hpc * MLA_QK), BF16),
                   jax.ShapeDtypeStruct((N_CHIPS, rows, hpc * (MLA_NOPE + MLA_V)), BF16),
                   jax.ShapeDtypeStruct((rows, MLA_ROPE), F32)),
        grid=(rows // tr, N_CHIPS),
        in_specs=[head_spec(MLA_QK), head_spec(MLA_QK), head_spec(MLA_V), row_spec(half), row_spec(half)],
        out_specs=(blk_spec(hpc * MLA_QK), blk_spec(hpc * (MLA_NOPE + MLA_V)), row_spec(MLA_ROPE)),
        compiler_params=_params("parallel", "arbitrary"),
    )(dq, dk, dv, cos, sin)


def _causal_keep(t):
    return lax.broadcasted_iota(jnp.int32, (t, t), 1) <= lax.broadcasted_iota(jnp.int32, (t, t), 0)


def _mla_fwd(q, k, v):
    nh, rows, _ = q.shape
    t = _fit(rows, ATTN_TILE)
    scale = MLA_QK ** -0.5

    def body(q_ref, k_ref, v_ref, o_ref, lse_ref):
        i = pl.program_id(1)
        qb = q_ref[...]

        def step(j, carry, diagonal):
            m, l, acc = carry
            rows_j = pl.ds(pl.multiple_of(j * t, t), t)
            s = _dot(qb, k_ref[rows_j, :], 1, 1) * scale
            if diagonal:
                s = jnp.where(_causal_keep(t), s, MASK_VALUE)
            m_new = jnp.maximum(m, jnp.max(s, axis=-1, keepdims=True))
            alpha = jnp.exp(m - m_new)
            p = jnp.exp(s - m_new)
            l = alpha * l + jnp.sum(p, axis=-1, keepdims=True)
            acc = alpha * acc + _dot(p.astype(BF16), v_ref[rows_j, :], 1, 0)
            return m_new, l, acc

        init = (jnp.full((t, 1), MASK_VALUE, F32), jnp.zeros((t, 1), F32), jnp.zeros((t, MLA_V), F32))
        carry = lax.fori_loop(0, i, lambda j, c: step(j, c, False), init)
        m, l, acc = step(i, carry, True)
        o_ref[...] = acc / l
        lse_ref[...] = m + jnp.log(l)

    return pl.pallas_call(
        body, name="mla_fwd",
        out_shape=(jax.ShapeDtypeStruct((rows, nh * MLA_V), F32), jax.ShapeDtypeStruct((nh, rows, 1), F32)),
        grid=(nh, rows // t),
        in_specs=[pl.BlockSpec((None, t, MLA_QK), lambda h, i: (h, i, 0)),
                  pl.BlockSpec((None, rows, MLA_QK), lambda h, i: (h, 0, 0)),
                  pl.BlockSpec((None, rows, MLA_V), lambda h, i: (h, 0, 0))],
        out_specs=(pl.BlockSpec((t, MLA_V), lambda h, i: (i, h)), pl.BlockSpec((None, t, 1), lambda h, i: (h, i, 0))),
        compiler_params=_params("parallel", "arbitrary"),
    )(q, k, v)


def _mla_bwd_dq(q, k, v, do, lse, delta):
    nh, rows, _ = q.shape
    t = _fit(rows, ATTN_TILE)
    scale = MLA_QK ** -0.5

    def body(q_ref, k_ref, v_ref, do_ref, lse_ref, dl_ref, dq_ref):
        i = pl.program_id(1)
        qb, dob, lse_b, dl_b = q_ref[...], do_ref[...], lse_ref[...], dl_ref[...]

        def step(j, dq, diagonal):
            rows_j = pl.ds(pl.multiple_of(j * t, t), t)
            kb = k_ref[rows_j, :]
            s = _dot(qb, kb, 1, 1) * scale
            if diagonal:
                s = jnp.where(_causal_keep(t), s, MASK_VALUE)
            p = jnp.exp(s - lse_b)
            dp = _dot(dob, v_ref[rows_j, :], 1, 1)
            ds = p * (dp - dl_b) * scale
            return dq + _dot(ds.astype(BF16), kb, 1, 0)

        dq = lax.fori_loop(0, i, lambda j, c: step(j, c, False), jnp.zeros((t, MLA_QK), F32))
        dq_ref[...] = step(i, dq, True)

    return pl.pallas_call(
        body, name="mla_bwd_dq", out_shape=jax.ShapeDtypeStruct((nh, rows, MLA_QK), F32), grid=(nh, rows // t),
        in_specs=[pl.BlockSpec((None, t, MLA_QK), lambda h, i: (h, i, 0)),
                  pl.BlockSpec((None, rows, MLA_QK), lambda h, i: (h, 0, 0)),
                  pl.BlockSpec((None, rows, MLA_V), lambda h, i: (h, 0, 0)),
                  pl.BlockSpec((t, MLA_V), lambda h, i: (i, h)),
                  pl.BlockSpec((None, t, 1), lambda h, i: (h, i, 0)),
                  pl.BlockSpec((None, t, 1), lambda h, i: (h, i, 0))],
        out_specs=pl.BlockSpec((None, t, MLA_QK), lambda h, i: (h, i, 0)),
        compiler_params=_params("parallel", "arbitrary"),
    )(q, k, v, do, lse, delta)


def _mla_bwd_dkv(q, k, v, do, lse, delta):
    nh, rows, _ = q.shape
    t = _fit(rows, ATTN_TILE)
    n_t = rows // t
    scale = MLA_QK ** -0.5

    def body(q_ref, k_ref, v_ref, do_ref, lse_ref, dl_ref, dk_ref, dv_ref):
        j = pl.program_id(1)
        kb, vb = k_ref[...], v_ref[...]

        def step(i, carry, diagonal):
            dk, dv = carry
            rows_i = pl.ds(pl.multiple_of(i * t, t), t)
            qb, dob = q_ref[rows_i, :], do_ref[rows_i, :]
            s = _dot(qb, kb, 1, 1) * scale
            if diagonal:
                s = jnp.where(_causal_keep(t), s, MASK_VALUE)
            p = jnp.exp(s - lse_ref[rows_i, :])
            dv = dv + _dot(p.astype(BF16), dob, 0, 0)
            dp = _dot(dob, vb, 1, 1)
            ds = p * (dp - dl_ref[rows_i, :]) * scale
            dk = dk + _dot(ds.astype(BF16), qb, 0, 0)
            return dk, dv

        carry = step(j, (jnp.zeros((t, MLA_QK), F32), jnp.zeros((t, MLA_V), F32)), True)
        dk, dv = lax.fori_loop(j + 1, n_t, lambda i, c: step(i, c, False), carry)
        dk_ref[...] = dk
        dv_ref[...] = dv

    return pl.pallas_call(
        body, name="mla_bwd_dkv",
        out_shape=(jax.ShapeDtypeStruct((nh, rows, MLA_QK), F32), jax.ShapeDtypeStruct((nh, rows, MLA_V), F32)),
        grid=(nh, n_t),
        in_specs=[pl.BlockSpec((None, rows, MLA_QK), lambda h, j: (h, 0, 0)),
                  pl.BlockSpec((None, t, MLA_QK), lambda h, j: (h, j, 0)),
                  pl.BlockSpec((None, t, MLA_V), lambda h, j: (h, j, 0)),
                  pl.BlockSpec((rows, MLA_V), lambda h, j: (0, h)),
                  pl.BlockSpec((None, rows, 1), lambda h, j: (h, 0, 0)),
                  pl.BlockSpec((None, rows, 1), lambda h, j: (h, 0, 0))],
        out_specs=(pl.BlockSpec((None, t, MLA_QK), lambda h, j: (h, j, 0)), pl.BlockSpec((None, t, MLA_V), lambda h, j: (h, j, 0))),
        compiler_params=_params("parallel", "arbitrary"),
    )(q, k, v, do, lse, delta)


def _swa_slope(h):
    return 2.0 ** (-8.0 * (h + 1) / SWA_Q_HEADS)


def _swa_band_specs(rows):
    w = SWA_WINDOW
    kvw = SWA_KV_HEADS * SWA_HEAD_DIM

    def prev(i):
        return jnp.maximum(i - 1, 0)

    return [pl.BlockSpec((w, kvw), lambda i: (prev(i), 0)), pl.BlockSpec((w, kvw), lambda i: (i, 0)),
            pl.BlockSpec((w, kvw), lambda i: (prev(i), 0)), pl.BlockSpec((w, kvw), lambda i: (i, 0)),
            pl.BlockSpec((w, 1), lambda i: (i, 0)),
            pl.BlockSpec((1, w), lambda i: (0, prev(i))), pl.BlockSpec((1, w), lambda i: (0, i)),
            pl.BlockSpec(memory_space=pltpu.SMEM)]


def _swa_bias(i, pc_ref, pp_ref, pn_ref):
    w = SWA_WINDOW
    k_pos = jnp.concatenate([pp_ref[...], pn_ref[...]], axis=1)
    dist = jnp.abs(pc_ref[...] - k_pos).astype(F32)
    r = lax.broadcasted_iota(jnp.int32, (w, 2 * w), 0)
    col = lax.broadcasted_iota(jnp.int32, (w, 2 * w), 1)
    delta = r + w - col
    valid = (delta >= 0) & (delta < w) & ((col >= w) | (i > 0))
    return dist, valid


def _swa_fwd(q, k, v, pos_col, pos_row, sinks):
    rows = q.shape[0]
    w, hd = SWA_WINDOW, SWA_HEAD_DIM
    scale = hd ** -0.5

    def body(q_ref, kp_ref, kc_ref, vp_ref, vc_ref, pc_ref, pp_ref, pn_ref, sink_ref, o_ref, lse_ref):
        dist, valid = _swa_bias(pl.program_id(0), pc_ref, pp_ref, pn_ref)
        for kvh in range(SWA_KV_HEADS):
            cols = slice(kvh * hd, (kvh + 1) * hd)
            kb = jnp.concatenate([kp_ref[:, cols], kc_ref[:, cols]], axis=0)
            vb = jnp.concatenate([vp_ref[:, cols], vc_ref[:, cols]], axis=0)
            for g in range(SWA_GROUP):
                h = kvh * SWA_GROUP + g
                sink = sink_ref[h]
                s = _dot(q_ref[:, h * hd:(h + 1) * hd], kb, 1, 1) * scale - _swa_slope(h) * dist
                s = jnp.where(valid, s, MASK_VALUE)
                m = jnp.maximum(jnp.max(s, axis=-1, keepdims=True), sink)
                e = jnp.exp(s - m)
                den = jnp.sum(e, axis=-1, keepdims=True) + jnp.exp(sink - m)
                o_ref[:, h * hd:(h + 1) * hd] = _dot((e / den).astype(BF16), vb, 1, 0)
                lse_ref[:, h:h + 1] = m + jnp.log(den)

    return pl.pallas_call(
        body, name="swa_fwd",
        out_shape=(jax.ShapeDtypeStruct((rows, MIX_B), F32), jax.ShapeDtypeStruct((rows, SWA_Q_HEADS), F32)),
        grid=(rows // w,),
        in_specs=[pl.BlockSpec((w, MIX_B), lambda i: (i, 0))] + _swa_band_specs(rows),
        out_specs=(pl.BlockSpec((w, MIX_B), lambda i: (i, 0)), pl.BlockSpec((w, SWA_Q_HEADS), lambda i: (i, 0))),
        compiler_params=_params("parallel"),
    )(q, k, k, v, v, pos_col, pos_row, pos_row, sinks)


def _swa_bwd(q, k, v, pos_col, pos_row, sinks, do, lse, delta):
    rows = q.shape[0]
    w, hd = SWA_WINDOW, SWA_HEAD_DIM
    kvw = SWA_KV_HEADS * hd
    scale = hd ** -0.5

    def body(q_ref, kp_ref, kc_ref, vp_ref, vc_ref, pc_ref, pp_ref, pn_ref, sink_ref, do_ref, lse_ref, dl_ref,
             dq_ref, dkc_ref, dkp_ref, dvc_ref, dvp_ref, dsink_ref):
        dist, valid = _swa_bias(pl.program_id(0), pc_ref, pp_ref, pn_ref)

        @pl.when(pl.program_id(0) == 0)
        def _():
            dsink_ref[...] = jnp.zeros_like(dsink_ref)

        for kvh in range(SWA_KV_HEADS):
            cols = slice(kvh * hd, (kvh + 1) * hd)
            kb = jnp.concatenate([kp_ref[:, cols], kc_ref[:, cols]], axis=0)
            vb = jnp.concatenate([vp_ref[:, cols], vc_ref[:, cols]], axis=0)
            dk = jnp.zeros((2 * w, hd), F32)
            dv = jnp.zeros((2 * w, hd), F32)
            for g in range(SWA_GROUP):
                h = kvh * SWA_GROUP + g
                hc = slice(h * hd, (h + 1) * hd)
                qb, dob = q_ref[:, hc], do_ref[:, hc]
                lse_h, dl_h = lse_ref[:, h:h + 1], dl_ref[:, h:h + 1]
                s = _dot(qb, kb, 1, 1) * scale - _swa_slope(h) * dist
                s = jnp.where(valid, s, MASK_VALUE)
                p = jnp.exp(s - lse_h)
                ds = (p * (_dot(dob, vb, 1, 1) - dl_h) * scale).astype(BF16)
                dq_ref[:, hc] = _dot(ds, kb, 1, 0).astype(BF16)
                dk = dk + _dot(ds, qb, 0, 0)
                dv = dv + _dot(p.astype(BF16), dob, 0, 0)
                dsink_ref[:, h:h + 1] += -jnp.sum(jnp.exp(sink_ref[h] - lse_h) * dl_h, axis=0, keepdims=True)
            dkp_ref[:, cols] = dk[:w]
            dkc_ref[:, cols] = dk[w:]
            dvp_ref[:, cols] = dv[:w]
            dvc_ref[:, cols] = dv[w:]

    def blk(n):
        return pl.BlockSpec((w, n), lambda i: (i, 0))

    return pl.pallas_call(
        body, name="swa_bwd",
        out_shape=(jax.ShapeDtypeStruct((rows, MIX_B), BF16),) + (jax.ShapeDtypeStruct((rows, kvw), F32),) * 4
        + (jax.ShapeDtypeStruct((1, SWA_Q_HEADS), F32),),
        grid=(rows // w,),
        in_specs=[blk(MIX_B)] + _swa_band_specs(rows) + [blk(MIX_B), blk(SWA_Q_HEADS), blk(SWA_Q_HEADS)],
        out_specs=(blk(MIX_B), blk(kvw), blk(kvw), blk(kvw), blk(kvw), pl.BlockSpec((1, SWA_Q_HEADS), lambda i: (0, 0))),
        compiler_params=_params("arbitrary"),
    )(q, k, k, v, v, pos_col, pos_row, pos_row, sinks, do, lse, delta)


def _band_merge(cur, prev, name):
    rows, n = cur.shape
    w = SWA_WINDOW
    last = rows // w - 1

    def body(c_ref, p_ref, o_ref):
        nxt = jnp.where(pl.program_id(0) < last, p_ref[...], 0.0)
        o_ref[...] = (c_ref[...] + nxt).astype(BF16)

    return pl.pallas_call(
        body, name=name, out_shape=jax.ShapeDtypeStruct((rows, n), BF16), grid=(rows // w,),
        in_specs=[pl.BlockSpec((w, n), lambda j: (j, 0)), pl.BlockSpec((w, n), lambda j: (jnp.minimum(j + 1, last), 0))],
        out_specs=pl.BlockSpec((w, n), lambda j: (j, 0)), compiler_params=_params("parallel"),
    )(cur, prev)


def _adamw(w, g, m, v, name):
    rows, cols = w.shape
    tr, tc = _fit(rows, ROW_TILE, 8), _fit(cols, MM_TILE)
    c1 = 1.0 - ADAM_B1 ** ADAM_STEP
    c2 = 1.0 - ADAM_B2 ** ADAM_STEP

    def body(w_ref, g_ref, m_ref, v_ref, d_ref, mo_ref, vo_ref):
        gr = g_ref[...]
        m_new = ADAM_B1 * m_ref[...] + (1.0 - ADAM_B1) * gr
        v_new = ADAM_B2 * v_ref[...] + (1.0 - ADAM_B2) * jnp.square(gr)
        mo_ref[...] = m_new
        vo_ref[...] = v_new
        d_ref[...] = -ADAM_LR * ((m_new / c1) / (jnp.sqrt(v_new / c2) + ADAM_EPS) + ADAM_WD * w_ref[...])

    spec = pl.BlockSpec((tr, tc), lambda i, j: (i, j))
    return pl.pallas_call(
        body, name=name, out_shape=(jax.ShapeDtypeStruct(w.shape, F32),) * 3, grid=(rows // tr, cols // tc),
        in_specs=[spec] * 4, out_specs=(spec,) * 3, compiler_params=_params("parallel", "parallel"),
    )(w, g, m, v)


OTHER_CHIPS = ((1, 0), (0, 1), (1, 1))


def _place():
    x, y, c = lax.axis_index("x"), lax.axis_index("y"), lax.axis_index("c")
    return x, y, c


def _flip(v, f):
    return 1 - v if f else v


def _cast_into_slot(w, name):
    rows, cols = w.shape
    tr, tc = _fit(rows, 704, 16), _fit(cols, MM_TILE)

    def body(w_ref, o_ref):
        o_ref[...] = w_ref[...].astype(BF16)

    return pl.pallas_call(
        body, name=name, out_shape=jax.ShapeDtypeStruct((N_CHIPS, rows, cols), BF16), grid=(rows // tr, cols // tc),
        in_specs=[pl.BlockSpec((tr, tc), lambda i, j: (i, j))],
        out_specs=pl.BlockSpec((None, tr, tc), lambda i, j: (2 * lax.axis_index("x") + lax.axis_index("y"), i, j)),
        compiler_params=_params("parallel", "parallel"),
    )(w)


def _all_gather_weights(slots):
    n = len(slots)

    def body(*refs):
        src, dst = refs[:n], refs[n:2 * n]
        send, recv, fsend, frecv = refs[2 * n:]
        x, y, c = _place()
        me = 2 * x + y
        sent = []
        for a in range(n):
            half = slots[a].shape[1] // 2
            rows = pl.ds(c * half, half)
            for j, (fx, fy) in enumerate(OTHER_CHIPS):
                cp = pltpu.make_async_remote_copy(
                    src_ref=src[a].at[me, rows], dst_ref=dst[a].at[me, rows], send_sem=send.at[a * 3 + j],
                    recv_sem=recv.at[a * 3 + j], device_id=(_flip(x, fx), _flip(y, fy), c), device_id_type=MESH)
                cp.start()
                sent.append(cp)
        passed = []
        for a in range(n):
            half = slots[a].shape[1] // 2
            rows = pl.ds(c * half, half)
            for j, (fx, fy) in enumerate(OTHER_CHIPS):
                blk = 2 * _flip(x, fx) + _flip(y, fy)
                pltpu.make_async_remote_copy(
                    src_ref=src[a].at[blk, rows], dst_ref=dst[a].at[blk, rows], send_sem=send.at[a * 3 + j],
                    recv_sem=recv.at[a * 3 + j], device_id=(x, y, c), device_id_type=MESH).wait_recv()
                cp = pltpu.make_async_remote_copy(
                    src_ref=dst[a].at[blk, rows], dst_ref=dst[a].at[blk, rows], send_sem=fsend.at[a * 3 + j],
                    recv_sem=frecv.at[a * 3 + j], device_id=(x, y, 1 - c), device_id_type=MESH)
                cp.start()
                passed.append(cp)
        for a in range(n):
            half = slots[a].shape[1] // 2
            theirs = pl.ds((1 - c) * half, half)
            for j, (fx, fy) in enumerate(OTHER_CHIPS):
                blk = 2 * _flip(x, fx) + _flip(y, fy)
                pltpu.make_async_remote_copy(
                    src_ref=dst[a].at[blk, theirs], dst_ref=dst[a].at[blk, theirs], send_sem=fsend.at[a * 3 + j],
                    recv_sem=frecv.at[a * 3 + j], device_id=(x, y, 1 - c), device_id_type=MESH).wait_recv()
        for cp in sent + passed:
            cp.wait_send()

    return pl.pallas_call(
        body, name="all_gather_weights",
        out_shape=tuple(jax.ShapeDtypeStruct(s.shape, s.dtype) for s in slots),
        in_specs=[ANY] * n, out_specs=(ANY,) * n, input_output_aliases={a: a for a in range(n)},
        scratch_shapes=[pltpu.SemaphoreType.DMA((3 * n,))] * 4,
    )(*slots)


def _pair_exchange(grads):
    n = len(grads)

    def body(*refs):
        src, dst = refs[:n], refs[n:2 * n]
        send, recv = refs[2 * n:]
        x, y, c = _place()
        cps = []
        for a in range(n):
            half = grads[a].shape[1] // 2
            cp = pltpu.make_async_remote_copy(
                src_ref=src[a].at[:, pl.ds((1 - c) * half, half)], dst_ref=dst[a], send_sem=send.at[a],
                recv_sem=recv.at[a], device_id=(x, y, 1 - c), device_id_type=MESH)
            cp.start()
            cps.append(cp)
        for cp in cps:
            cp.wait()

    return pl.pallas_call(
        body, name="grad_pair_exchange",
        out_shape=tuple(jax.ShapeDtypeStruct((g.shape[0], g.shape[1] // 2, g.shape[2]), g.dtype) for g in grads),
        in_specs=[ANY] * n, out_specs=(ANY,) * n, scratch_shapes=[pltpu.SemaphoreType.DMA((n,))] * 2,
    )(*grads)


def _pair_add(mine, theirs, name):
    nb, rows, cols = theirs.shape
    tr, tc = _fit(rows, 704, 16), _fit(cols, MM_TILE)
    n_r = rows // tr

    def body(a_ref, b_ref, lo_ref, own_ref):
        s = a_ref[...] + b_ref[...]
        lo_ref[...] = s.astype(BF16)

        @pl.when(pl.program_id(2) == 2 * lax.axis_index("x") + lax.axis_index("y"))
        def _():
            own_ref[...] = s

    return pl.pallas_call(
        body, name=name, grid=(n_r, cols // tc, nb),
        in_specs=[pl.BlockSpec((None, tr, tc), lambda i, j, b: (b, lax.axis_index("c") * n_r + i, j)),
                  pl.BlockSpec((None, tr, tc), lambda i, j, b: (b, i, j))],
        out_specs=(pl.BlockSpec((None, tr, tc), lambda i, j, b: (b, i, j)), pl.BlockSpec((tr, tc), lambda i, j, b: (i, j))),
        out_shape=(jax.ShapeDtypeStruct(theirs.shape, BF16), jax.ShapeDtypeStruct((rows, cols), F32)),
        compiler_params=_params("parallel", "parallel", "arbitrary"),
    )(mine, theirs)


def _chip_exchange(sums):
    n = len(sums)

    def body(*refs):
        src, dst = refs[:n], refs[n:2 * n]
        send, recv = refs[2 * n:]
        x, y, c = _place()
        cps = []
        for a in range(n):
            for j, (fx, fy) in enumerate(OTHER_CHIPS):
                px, py = _flip(x, fx), _flip(y, fy)
                cp = pltpu.make_async_remote_copy(
                    src_ref=src[a].at[2 * px + py], dst_ref=dst[a].at[j], send_sem=send.at[a * 3 + j],
                    recv_sem=recv.at[a * 3 + j], device_id=(px, py, c), device_id_type=MESH)
                cp.start()
                cps.append(cp)
        for cp in cps:
            cp.wait()

    return pl.pallas_call(
        body, name="grad_chip_exchange",
        out_shape=tuple(jax.ShapeDtypeStruct((3,) + s.shape[1:], s.dtype) for s in sums),
        in_specs=[ANY] * n, out_specs=(ANY,) * n, scratch_shapes=[pltpu.SemaphoreType.DMA((3 * n,))] * 2,
    )(*sums)


def _chip_add(own, got, name):
    rows, cols = own.shape
    tr, tc = _fit(rows, 704, 16), _fit(cols, MM_TILE)
    n_r = rows // tr

    def body(o_ref, g_ref, out_ref):
        out_ref[...] = ((o_ref[...] + g_ref[0].astype(F32)) + g_ref[1].astype(F32)) + g_ref[2].astype(F32)

    return pl.pallas_call(
        body, name=name, out_shape=jax.ShapeDtypeStruct((2 * rows, cols), F32), grid=(n_r, cols // tc),
        in_specs=[pl.BlockSpec((tr, tc), lambda i, j: (i, j)), pl.BlockSpec((3, tr, tc), lambda i, j: (0, i, j))],
        out_specs=pl.BlockSpec((tr, tc), lambda i, j: (lax.axis_index("c") * n_r + i, j)),
        compiler_params=_params("parallel", "parallel"),
    )(own, got)


def _pair_share(grads):
    n = len(grads)

    def body(*refs):
        src, dst = refs[:n], refs[n:2 * n]
        send, recv = refs[2 * n:]
        x, y, c = _place()
        cps = []
        for a in range(n):
            half = grads[a].shape[0] // 2
            rows = pl.ds(c * half, half)
            cp = pltpu.make_async_remote_copy(
                src_ref=src[a].at[rows], dst_ref=dst[a].at[rows], send_sem=send.at[a], recv_sem=recv.at[a],
                device_id=(x, y, 1 - c), device_id_type=MESH)
            cp.start()
            cps.append(cp)
        for a in range(n):
            half = grads[a].shape[0] // 2
            theirs = pl.ds((1 - c) * half, half)
            pltpu.make_async_remote_copy(
                src_ref=src[a].at[theirs], dst_ref=dst[a].at[theirs], send_sem=send.at[a], recv_sem=recv.at[a],
                device_id=(x, y, 1 - c), device_id_type=MESH).wait_recv()
        for cp in cps:
            cp.wait_send()

    return pl.pallas_call(
        body, name="grad_pair_share", out_shape=tuple(jax.ShapeDtypeStruct(g.shape, g.dtype) for g in grads),
        in_specs=[ANY] * n, out_specs=(ANY,) * n, input_output_aliases={a: a for a in range(n)},
        scratch_shapes=[pltpu.SemaphoreType.DMA((n,))] * 2,
    )(*grads)


def _all_sum_small(vec):
    r, n = vec.shape
    flips = [(a, b, d) for a in (0, 1) for b in (0, 1) for d in (0, 1)][1:]

    def body(v_ref, o_ref, buf, send, recv):
        x, y, c = _place()
        me = 4 * x + 2 * y + c
        cps = []
        for k, (fx, fy, fc) in enumerate(flips):
            cp = pltpu.make_async_remote_copy(
                src_ref=v_ref, dst_ref=buf.at[me], send_sem=send.at[k], recv_sem=recv.at[k],
                device_id=(_flip(x, fx), _flip(y, fy), _flip(c, fc)), device_id_type=MESH)
            cp.start()
            cps.append(cp)
        buf[me] = v_ref[...]
        for k, (fx, fy, fc) in enumerate(flips):
            peer = 4 * _flip(x, fx) + 2 * _flip(y, fy) + _flip(c, fc)
            pltpu.make_async_remote_copy(
                src_ref=v_ref, dst_ref=buf.at[peer], send_sem=send.at[k], recv_sem=recv.at[k],
                device_id=(x, y, c), device_id_type=MESH).wait_recv()
        for cp in cps:
            cp.wait_send()
        acc = buf[0]
        for d in range(1, 8):
            acc = acc + buf[d]
        o_ref[...] = acc

    return pl.pallas_call(
        body, name="all_sum_small", out_shape=jax.ShapeDtypeStruct((r, n), F32),
        in_specs=[pl.BlockSpec(memory_space=pltpu.VMEM)], out_specs=pl.BlockSpec(memory_space=pltpu.VMEM),
        scratch_shapes=[pltpu.VMEM((8, r, n), F32), pltpu.SemaphoreType.DMA((7,)), pltpu.SemaphoreType.DMA((7,))],
    )(vec)


SMALL = ("attn_pre_g", "q_norm_g", "kv_norm_g", "swa_sinks", "grp_a_g", "grp_b_g", "attn_post_g", "ffn_pre_g", "ffn_post_g")
BIG = ("w_in", "w_uq", "w_ukv", "w_o", "w_gate", "w_up", "w_down")
ORDER = ("attn_pre_g", "w_in", "q_norm_g", "w_uq", "kv_norm_g", "w_ukv", "swa_sinks", "grp_a_g", "grp_b_g", "w_o",
         "attn_post_g", "ffn_pre_g", "w_gate", "w_up", "w_down", "ffn_post_g")


def _pad_lanes(v):
    n = v.shape[1]
    return jnp.pad(v, ((0, 0), (0, -n % LANES)))


def kernel(x, positions, attn_pre_g, w_in, q_norm_g, w_uq, kv_norm_g, w_ukv, swa_sinks, grp_a_g, grp_b_g, w_o, attn_post_g, ffn_pre_g, w_gate, w_up, w_down, ffn_post_g, loss_target, m_attn_pre_g, m_w_in, m_q_norm_g, m_w_uq, m_kv_norm_g, m_w_ukv, m_swa_sinks, m_grp_a_g, m_grp_b_g, m_w_o, m_attn_post_g, m_ffn_pre_g, m_w_gate, m_w_up, m_w_down, m_ffn_post_g, v_attn_pre_g, v_w_in, v_q_norm_g, v_w_uq, v_kv_norm_g, v_w_ukv, v_swa_sinks, v_grp_a_g, v_grp_b_g, v_w_o, v_attn_post_g, v_ffn_pre_g, v_w_gate, v_w_up, v_w_down, v_ffn_post_g):
    given = dict(locals())
    w32 = {k: given[k][0] for k in BIG}
    gains = {k: given[k] for k in SMALL}
    xs, tgt = x[0], loss_target[0]
    seq, d_model = xs.shape
    q_rank, kv_rank = q_norm_g.shape[1], kv_norm_g.shape[1]
    kvw = SWA_KV_HEADS * SWA_HEAD_DIM

    full = dict(zip(BIG, _all_gather_weights([_cast_into_slot(w32[k], "cast_" + k) for k in BIG])))

    pos = positions[0]
    inv = 1.0 / (ROPE_THETA ** (jnp.arange(0, MLA_ROPE, 2, dtype=F32) / MLA_ROPE))
    ang = pos.astype(F32)[:, None] * inv
    cos, sin = jnp.cos(ang), jnp.sin(ang)
    pos_col, pos_row = pos[:, None], pos[None, :]
    sinks = swa_sinks[0]

    a = _norm_fwd(xs, attn_pre_g, BF16, "attn_pre_norm")
    proj4 = _matmul(a, full["w_in"], name="in_proj")
    proj = jnp.concatenate([proj4[b] for b in range(N_CHIPS)], axis=1)
    cuts = (0, q_rank, q_rank + kv_rank, q_rank + kv_rank + MLA_ROPE)
    cuts = cuts + (cuts[3] + MIX_B, cuts[3] + MIX_B + kvw, cuts[3] + MIX_B + 2 * kvw)
    c_q, c_kv, k_rope, q_s, k_s, v_s = (proj[:, lo:hi] for lo, hi in zip(cuts[:-1], cuts[1:]))
    cqn = _norm_fwd(c_q, q_norm_g, BF16, "q_norm")
    ckvn = _norm_fwd(c_kv, kv_norm_g, BF16, "kv_norm")
    q4 = _matmul(cqn, full["w_uq"], name="q_up")
    kv4 = _matmul(ckvn, full["w_ukv"], name="kv_up")
    qh, kh, vh = _mla_prep(q4, kv4, k_rope, cos, sin)
    o_a, lse_a = _mla_fwd(qh, kh, vh)
    q_sb, k_sb, v_sb = q_s.astype(BF16), k_s.astype(BF16), v_s.astype(BF16)
    o_b, lse_b = _swa_fwd(q_sb, k_sb, v_sb, pos_col, pos_row, sinks)
    mix = _mix_fwd(o_a, o_b, grp_a_g, grp_b_g)
    w_o_full = full["w_o"].reshape(N_CHIPS * full["w_o"].shape[1], d_model)
    ao = _matmul(mix, w_o_full, name="out_proj")
    h1, f = _post_pre_fwd(xs, ao, attn_post_g, ffn_pre_g)
    gate = _matmul(f, full["w_gate"], name="ffn_gate")
    up = _matmul(f, full["w_up"], name="ffn_up")
    act = _swiglu_fwd(gate, up)
    dn = _matmul(act, full["w_down"], reduce_b=True, name="ffn_down")
    loss_row, dy, ddn, d_ffn_post = _loss_bwd(h1, dn, ffn_post_g, tgt)

    dact = _matmul(ddn, full["w_down"], tb=True, name="ffn_down_dx")
    dw_down = _matmul(act, ddn, ta=True, name="ffn_down_dw", tn=512)
    dgate, dup = _swiglu_bwd(gate, up, dact)
    dw_gate = _matmul(f, dgate, ta=True, name="ffn_gate_dw", tm=512)
    dw_up = _matmul(f, dup, ta=True, name="ffn_up_dw", tm=512)
    df_g = _matmul(dgate, full["w_gate"], tb=True, reduce_b=True, name="ffn_gate_dx")
    df_u = _matmul(dup, full["w_up"], tb=True, reduce_b=True, name="ffn_up_dx")
    dh1, d_ffn_pre = _norm_bwd(h1, ffn_pre_g, [df_g, df_u], [dy], F32, "ffn_pre_norm_bwd")
    dao, d_attn_post = _norm_bwd(ao, attn_post_g, [dh1], [], BF16, "attn_post_norm_bwd")
    dmix = _matmul(dao, w_o_full, tb=True, name="out_proj_dx")
    dw_o = _matmul(mix, dao, ta=True, name="out_proj_dw")
    do_a, do_b, d_grp_a, d_grp_b, dl_a, dl_b = _mix_bwd(o_a, o_b, grp_a_g, grp_b_g, dmix)
    dl_a = dl_a.T[:, :, None]
    dqh = _mla_bwd_dq(qh, kh, vh, do_a, lse_a, dl_a)
    dkh, dvh = _mla_bwd_dkv(qh, kh, vh, do_a, lse_a, dl_a)
    dq4, dkv4, dk_rope = _mla_unprep(dqh, dkh, dvh, cos, sin)
    dw_uq = _matmul(cqn, dq4, ta=True, name="q_up_dw")
    dcqn = _matmul(dq4, full["w_uq"], tb=True, reduce_b=True, name="q_up_dx")
    dw_ukv = _matmul(ckvn, dkv4, ta=True, name="kv_up_dw")
    dckvn = _matmul(dkv4, full["w_ukv"], tb=True, reduce_b=True, name="kv_up_dx")
    dc_q, d_q_norm = _norm_bwd(c_q, q_norm_g, [dcqn], [], BF16, "q_norm_bwd")
    dc_kv, d_kv_norm = _norm_bwd(c_kv, kv_norm_g, [dckvn], [], BF16, "kv_norm_bwd")
    dq_s, dk_cur, dk_prev, dv_cur, dv_prev, d_sinks = _swa_bwd(q_sb, k_sb, v_sb, pos_col, pos_row, sinks, do_b, lse_b, dl_b)
    dk_s = _band_merge(dk_cur, dk_prev, "swa_dk_merge")
    dv_s = _band_merge(dv_cur, dv_prev, "swa_dv_merge")
    dproj = jnp.concatenate([dc_q, dc_kv, dk_rope.astype(BF16), dq_s, dk_s, dv_s], axis=1)
    blk_w = dproj.shape[1] // N_CHIPS
    dproj4 = jnp.stack([dproj[:, b * blk_w:(b + 1) * blk_w] for b in range(N_CHIPS)])
    dw_in = _matmul(a, dproj4, ta=True, name="in_proj_dw", tm=512)
    da = _matmul(dproj4, full["w_in"], tb=True, reduce_b=True, name="in_proj_dx")
    dx, d_attn_pre = _norm_bwd(xs, attn_pre_g, [da], [dh1], F32, "attn_pre_norm_bwd")

    small_grads = dict(attn_pre_g=d_attn_pre, q_norm_g=d_q_norm, kv_norm_g=d_kv_norm, swa_sinks=d_sinks, grp_a_g=d_grp_a,
                       grp_b_g=d_grp_b, attn_post_g=d_attn_post, ffn_pre_g=d_ffn_pre, ffn_post_g=d_ffn_post)
    parts = [loss_row] + [_pad_lanes(small_grads[k]) for k in SMALL]
    packed = jnp.concatenate(parts, axis=1)
    n_packed = packed.shape[1]
    packed = jnp.pad(packed, ((0, 0), (0, -n_packed % (8 * LANES)))).reshape(8, -1)
    total = _all_sum_small(packed).reshape(1, -1)
    loss = total[0, 0]
    g_small, off = {}, LANES
    for k in SMALL:
        n = gains[k].shape[1]
        g_small[k] = total[:, off:off + n]
        off += n + (-n % LANES)

    def pack_small(prefix):
        return jnp.concatenate([_pad_lanes(given[prefix + k]) for k in SMALL], axis=1)

    d_sm, m_sm, v_sm = _adamw(pack_small(""), total[:, LANES:n_packed], pack_small("m_"), pack_small("v_"), "adamw_small")
    delta, new_m, new_v, off = {}, {}, {}, 0
    for k in SMALL:
        n = gains[k].shape[1]
        delta[k], new_m[k], new_v[k] = d_sm[:, off:off + n], m_sm[:, off:off + n], v_sm[:, off:off + n]
        off += n + (-n % LANES)

    grads4 = dict(w_in=dw_in, w_uq=dw_uq, w_ukv=dw_ukv, w_o=dw_o.reshape(N_CHIPS, -1, d_model), w_gate=dw_gate,
                  w_up=dw_up, w_down=dw_down)
    theirs = _pair_exchange([grads4[k] for k in BIG])
    lows, owns = zip(*[_pair_add(grads4[k], t, "pair_add_" + k) for k, t in zip(BIG, theirs)])
    got = _chip_exchange(list(lows))
    halves = [_chip_add(o, g, "chip_add_" + k) for k, o, g in zip(BIG, owns, got)]
    g_big = dict(zip(BIG, _pair_share(halves)))
    for k in BIG:
        delta[k], new_m[k], new_v[k] = _adamw(w32[k], g_big[k], given["m_" + k][0], given["v_" + k][0], "adamw_" + k)

    def out(d, k):
        return d[k][None] if k in BIG else d[k]

    grads = {**g_small, **g_big}
    return (loss, dx[None], *[out(grads, k) for k in ORDER], *[out(delta, k) for k in ORDER],
            *[out(new_m, k) for k in ORDER], *[out(new_v, k) for k in ORDER])
```

```python
import functools
import math

import jax
import jax.numpy as jnp
from jax import lax
from jax.experimental import pallas as pl
from jax.experimental.pallas import tpu as pltpu

F32, BF16 = jnp.float32, jnp.bfloat16
MESH = pl.DeviceIdType.MESH
ANY = pl.BlockSpec(memory_space=pl.ANY)

N_CHIPS = 4
EPS = 1e-6
MLA_HEADS, MLA_NOPE, MLA_ROPE, MLA_V = 16, 128, 64, 128
MLA_QK = MLA_NOPE + MLA_ROPE
HEADS_PER_CHIP = MLA_HEADS // N_CHIPS
ROPE_THETA = 10000.0
SWA_Q_HEADS, SWA_KV_HEADS, SWA_HEAD_DIM, SWA_WINDOW = 32, 8, 64, 128
SWA_GROUP = SWA_Q_HEADS // SWA_KV_HEADS
MIX_A, MIX_B = MLA_HEADS * MLA_V, SWA_Q_HEADS * SWA_HEAD_DIM
MASK_VALUE = float(jnp.finfo(jnp.float32).min)
ADAM_LR, ADAM_B1, ADAM_B2, ADAM_EPS, ADAM_WD, ADAM_STEP = 0.001, 0.9, 0.999, 1e-08, 0.01, 10

LANES = 128
VMEM_LIMIT = 56 << 20
ATTN_TILE = 512
ROW_TILE = 256
MM_TILE = 1024


def _fit(dim, pref, mult=LANES):
    if dim <= pref:
        return dim
    for t in range(pref - pref % mult, 0, -mult):
        if dim % t == 0:
            return t
    return dim


def _params(*semantics):
    return pltpu.CompilerParams(dimension_semantics=semantics, vmem_limit_bytes=VMEM_LIMIT)


def _dot(a, b, ca, cb):
    return lax.dot_general(a, b, (((ca,), (cb,)), ((), ())), preferred_element_type=F32)


def _matmul(a, b, *, name, ta=False, tb=False, reduce_b=False, out_dtype=F32, tm=MM_TILE, tn=MM_TILE, tk=MM_TILE, comm=None):
    a3, b3 = a.ndim == 3, b.ndim == 3
    nb = a.shape[0] if a3 else (b.shape[0] if b3 else 1)
    (K, M) = a.shape[-2:] if ta else a.shape[-2:][::-1]
    (N, K2) = b.shape[-2:] if tb else b.shape[-2:][::-1]
    assert K == K2, (a.shape, b.shape)
    tm, tn, tk = _fit(M, tm), _fit(N, tn), _fit(K, tk)
    batched_out = (a3 or b3) and not reduce_b
    n_bo = nb if batched_out else 1
    n_br = nb if reduce_b else 1
    nk = K // tk

    def sel(bo, br):
        return br if reduce_b else bo

    def a_map(bo, i, j, br, k):
        t = (k, i) if ta else (i, k)
        return (sel(bo, br),) + t if a3 else t

    def b_map(bo, i, j, br, k):
        t = (j, k) if tb else (k, j)
        return (sel(bo, br),) + t if b3 else t

    def o_map(bo, i, j, br, k):
        return (bo, i, j) if batched_out else (i, j)

    a_blk = (tk, tm) if ta else (tm, tk)
    b_blk = (tn, tk) if tb else (tk, tn)
    a_blk = (None,) + a_blk if a3 else a_blk
    b_blk = (None,) + b_blk if b3 else b_blk
    o_blk = (None, tm, tn) if batched_out else (tm, tn)
    o_shape = (nb, M, N) if batched_out else (M, N)

    grid = (n_bo, M // tm, N // tn, n_br, nk)
    n_ci, n_co = (len(comm.ins), len(comm.out_shapes)) if comm else (0, 0)

    def body(*refs):
        a_ref, b_ref = refs[:2]
        c_in, o_ref, c_out = refs[2:2 + n_ci], refs[2 + n_ci], refs[3 + n_ci:3 + n_ci + n_co]
        acc_ref, sems = refs[3 + n_ci + n_co], refs[4 + n_ci + n_co:]
        br, k = pl.program_id(3), pl.program_id(4)
        first, last = _grid_ends(grid)
        if comm:
            pl.when(first)(lambda: comm.start(c_in, c_out, sems))

        @pl.when((br == 0) & (k == 0))
        def _():
            acc_ref[...] = jnp.zeros_like(acc_ref)

        acc_ref[...] += _dot(a_ref[...], b_ref[...], 0 if ta else 1, 1 if tb else 0)

        @pl.when((br == n_br - 1) & (k == nk - 1))
        def _():
            o_ref[...] = acc_ref[...].astype(o_ref.dtype)

        if comm:
            pl.when(last)(lambda: comm.finish(c_in, c_out, sems))

    res = pl.pallas_call(
        body, name=name, out_shape=(jax.ShapeDtypeStruct(o_shape, out_dtype),) + tuple(comm.out_shapes if comm else ()),
        grid=grid,
        in_specs=[pl.BlockSpec(a_blk, a_map), pl.BlockSpec(b_blk, b_map)] + [ANY] * n_ci,
        out_specs=(pl.BlockSpec(o_blk, o_map),) + (ANY,) * n_co,
        input_output_aliases={2 + i: 1 + o for i, o in comm.aliases.items()} if comm else {},
        scratch_shapes=[pltpu.VMEM((tm, tn), F32)] + (comm.scratch if comm else []),
        compiler_params=_params(*(["parallel"] * 3 + ["arbitrary"] * 2 if not comm else ["arbitrary"] * 5)),
    )(a, b, *(comm.ins if comm else ()))
    return (res[0], res[1:]) if comm else res[0]


def _inv_rms(u):
    return lax.rsqrt(jnp.mean(u * u, axis=-1, keepdims=True) + EPS)


def _norm_bwd_math(u, g, dz):
    r = _inv_rms(u)
    w = dz * g
    du = r * w - u * (r * r * r * jnp.mean(w * u, axis=-1, keepdims=True))
    dg = jnp.sum(dz * (u * r), axis=0, keepdims=True)
    return du, dg


def _row_spec(tr, n):
    return pl.BlockSpec((tr, n), lambda i: (i, 0))


def _gain_spec(n):
    return pl.BlockSpec((1, n), lambda i: (0, 0))


def _norm_fwd(u, g, out_dtype, name):
    rows, n = u.shape
    tr = _fit(rows, ROW_TILE, 16)

    def body(u_ref, g_ref, o_ref):
        x = u_ref[...]
        o_ref[...] = (x * _inv_rms(x) * g_ref[...]).astype(o_ref.dtype)

    return pl.pallas_call(
        body, name=name, out_shape=jax.ShapeDtypeStruct((rows, n), out_dtype), grid=(rows // tr,),
        in_specs=[_row_spec(tr, n), _gain_spec(n)], out_specs=_row_spec(tr, n), compiler_params=_params("parallel"),
    )(u, g)


def _norm_bwd(u, g, dzs, adds, out_dtype, name):
    rows, n = u.shape
    tr = _fit(rows, ROW_TILE, 16)
    n_dz, n_add = len(dzs), len(adds)

    def body(*refs):
        u_ref, g_ref = refs[:2]
        dz_refs = refs[2:2 + n_dz]
        add_refs = refs[2 + n_dz:2 + n_dz + n_add]
        du_ref, dg_ref = refs[2 + n_dz + n_add:]
        dz = dz_refs[0][...].astype(F32)
        for r in dz_refs[1:]:
            dz = dz + r[...].astype(F32)
        du, dg = _norm_bwd_math(u_ref[...], g_ref[...], dz)
        for r in add_refs:
            du = du + r[...]
        du_ref[...] = du.astype(du_ref.dtype)

        @pl.when(pl.program_id(0) == 0)
        def _():
            dg_ref[...] = jnp.zeros_like(dg_ref)

        dg_ref[...] += dg

    return pl.pallas_call(
        body, name=name,
        out_shape=(jax.ShapeDtypeStruct((rows, n), out_dtype), jax.ShapeDtypeStruct((1, n), F32)), grid=(rows // tr,),
        in_specs=[_row_spec(tr, n), _gain_spec(n)] + [_row_spec(tr, n)] * (n_dz + n_add),
        out_specs=(_row_spec(tr, n), _gain_spec(n)), compiler_params=_params("arbitrary"),
    )(u, g, *dzs, *adds)


def _mix_fwd(o_a, o_b, g_a, g_b):
    rows = o_a.shape[0]
    tr = _fit(rows, ROW_TILE, 16)

    def body(a_ref, b_ref, ga_ref, gb_ref, o_ref):
        a, b = a_ref[...], b_ref[...]
        o_ref[:, :MIX_A] = (a * _inv_rms(a) * ga_ref[...]).astype(BF16)
        o_ref[:, MIX_A:] = (b * _inv_rms(b) * gb_ref[...]).astype(BF16)

    return pl.pallas_call(
        body, name="mix_fwd", out_shape=jax.ShapeDtypeStruct((rows, MIX_A + MIX_B), BF16), grid=(rows // tr,),
        in_specs=[_row_spec(tr, MIX_A), _row_spec(tr, MIX_B), _gain_spec(MIX_A), _gain_spec(MIX_B)],
        out_specs=_row_spec(tr, MIX_A + MIX_B), compiler_params=_params("parallel"),
    )(o_a, o_b, g_a, g_b)


def _mix_bwd(o_a, o_b, g_a, g_b, dmix):
    rows = o_a.shape[0]
    tr = _fit(rows, ROW_TILE, 16)

    def body(a_ref, b_ref, ga_ref, gb_ref, dm_ref, doa_ref, dob_ref, dga_ref, dgb_ref, dla_ref, dlb_ref):
        a, b = a_ref[...], b_ref[...]
        doa, dga = _norm_bwd_math(a, ga_ref[...], dm_ref[:, :MIX_A])
        dob, dgb = _norm_bwd_math(b, gb_ref[...], dm_ref[:, MIX_A:])
        doa_ref[...] = doa.astype(BF16)
        dob_ref[...] = dob.astype(BF16)
        pa, pb = doa * a, dob * b
        for h in range(MLA_HEADS):
            dla_ref[:, h:h + 1] = jnp.sum(pa[:, h * MLA_V:(h + 1) * MLA_V], axis=-1, keepdims=True)
        for h in range(SWA_Q_HEADS):
            dlb_ref[:, h:h + 1] = jnp.sum(pb[:, h * SWA_HEAD_DIM:(h + 1) * SWA_HEAD_DIM], axis=-1, keepdims=True)

        @pl.when(pl.program_id(0) == 0)
        def _():
            dga_ref[...] = jnp.zeros_like(dga_ref)
            dgb_ref[...] = jnp.zeros_like(dgb_ref)

        dga_ref[...] += dga
        dgb_ref[...] += dgb

    return pl.pallas_call(
        body, name="mix_bwd",
        out_shape=(jax.ShapeDtypeStruct((rows, MIX_A), BF16), jax.ShapeDtypeStruct((rows, MIX_B), BF16),
                   jax.ShapeDtypeStruct((1, MIX_A), F32), jax.ShapeDtypeStruct((1, MIX_B), F32),
                   jax.ShapeDtypeStruct((rows, MLA_HEADS), F32), jax.ShapeDtypeStruct((rows, SWA_Q_HEADS), F32)),
        grid=(rows // tr,),
        in_specs=[_row_spec(tr, MIX_A), _row_spec(tr, MIX_B), _gain_spec(MIX_A), _gain_spec(MIX_B),
                  _row_spec(tr, MIX_A + MIX_B)],
        out_specs=(_row_spec(tr, MIX_A), _row_spec(tr, MIX_B), _gain_spec(MIX_A), _gain_spec(MIX_B),
                   _row_spec(tr, MLA_HEADS), _row_spec(tr, SWA_Q_HEADS)),
        compiler_params=_params("arbitrary"),
    )(o_a, o_b, g_a, g_b, dmix)


def _post_pre_fwd(x, ao, g_post, g_pre):
    rows, n = x.shape
    tr = _fit(rows, ROW_TILE, 16)

    def body(x_ref, ao_ref, g1_ref, g2_ref, h_ref, f_ref):
        u = ao_ref[...]
        h = x_ref[...] + u * _inv_rms(u) * g1_ref[...]
        h_ref[...] = h
        f_ref[...] = (h * _inv_rms(h) * g2_ref[...]).astype(BF16)

    return pl.pallas_call(
        body, name="post_pre_fwd",
        out_shape=(jax.ShapeDtypeStruct((rows, n), F32), jax.ShapeDtypeStruct((rows, n), BF16)), grid=(rows // tr,),
        in_specs=[_row_spec(tr, n), _row_spec(tr, n), _gain_spec(n), _gain_spec(n)],
        out_specs=(_row_spec(tr, n), _row_spec(tr, n)), compiler_params=_params("parallel"),
    )(x, ao, g_post, g_pre)


def _loss_bwd(h1, dn, g_post, target):
    rows, n = h1.shape
    tr = _fit(rows, ROW_TILE, 16)

    def body(h_ref, u_ref, g_ref, t_ref, loss_ref, dy_ref, du_ref, dg_ref):
        u, g = u_ref[...], g_ref[...]
        err = h_ref[...] + u * _inv_rms(u) * g - t_ref[...]
        dy = err / n
        dy_ref[...] = dy
        du, dg = _norm_bwd_math(u, g, dy)
        du_ref[...] = du.astype(BF16)

        @pl.when(pl.program_id(0) == 0)
        def _():
            loss_ref[...] = jnp.zeros_like(loss_ref)
            dg_ref[...] = jnp.zeros_like(dg_ref)

        loss_ref[...] += jnp.full((1, LANES), 0.5 * jnp.sum(jnp.mean(err * err, axis=-1)), F32)
        dg_ref[...] += dg

    return pl.pallas_call(
        body, name="loss_bwd",
        out_shape=(jax.ShapeDtypeStruct((1, LANES), F32), jax.ShapeDtypeStruct((rows, n), F32),
                   jax.ShapeDtypeStruct((rows, n), BF16), jax.ShapeDtypeStruct((1, n), F32)),
        grid=(rows // tr,),
        in_specs=[_row_spec(tr, n), _row_spec(tr, n), _gain_spec(n), _row_spec(tr, n)],
        out_specs=(_gain_spec(LANES), _row_spec(tr, n), _row_spec(tr, n), _gain_spec(n)),
        compiler_params=_params("arbitrary"),
    )(h1, dn, g_post, target)


def _blk3_spec(tr, n):
    return pl.BlockSpec((None, tr, n), lambda b, i: (b, i, 0))


def _swiglu_fwd(gate, up):
    nb, rows, n = gate.shape
    tr = _fit(rows, ROW_TILE, 16)

    def body(g_ref, u_ref, o_ref):
        g = g_ref[...]
        o_ref[...] = (g * jax.nn.sigmoid(g) * u_ref[...]).astype(BF16)

    return pl.pallas_call(
        body, name="swiglu_fwd", out_shape=jax.ShapeDtypeStruct(gate.shape, BF16), grid=(nb, rows // tr),
        in_specs=[_blk3_spec(tr, n)] * 2, out_specs=_blk3_spec(tr, n), compiler_params=_params("parallel", "parallel"),
    )(gate, up)


def _swiglu_bwd(gate, up, dact):
    nb, rows, n = gate.shape
    tr = _fit(rows, ROW_TILE, 16)

    def body(g_ref, u_ref, d_ref, dg_ref, du_ref):
        g, d = g_ref[...], d_ref[...]
        sig = jax.nn.sigmoid(g)
        silu = g * sig
        du_ref[...] = (d * silu).astype(BF16)
        dg_ref[...] = (d * u_ref[...] * (sig * (1.0 + g * (1.0 - sig)))).astype(BF16)

    return pl.pallas_call(
        body, name="swiglu_bwd",
        out_shape=(jax.ShapeDtypeStruct(gate.shape, BF16), jax.ShapeDtypeStruct(gate.shape, BF16)), grid=(nb, rows // tr),
        in_specs=[_blk3_spec(tr, n)] * 3, out_specs=(_blk3_spec(tr, n),) * 2,
        compiler_params=_params("parallel", "parallel"),
    )(gate, up, dact)


def _rope(x, cos, sin):
    half = MLA_ROPE // 2
    x1, x2 = x[:, :half], x[:, half:]
    return jnp.concatenate([x1 * cos - x2 * sin, x2 * cos + x1 * sin], axis=-1)


def _rope_t(d, cos, sin):
    half = MLA_ROPE // 2
    d1, d2 = d[:, :half], d[:, half:]
    return jnp.concatenate([d1 * cos + d2 * sin, d2 * cos - d1 * sin], axis=-1)


def _mla_prep(q4, kv4, k_rope, cos, sin):
    nb, rows, _ = q4.shape
    tr = _fit(rows, ROW_TILE, 16)
    hpc, half = HEADS_PER_CHIP, MLA_ROPE // 2

    def body(q_ref, kv_ref, kr_ref, c_ref, s_ref, qo_ref, ko_ref, vo_ref):
        cos, sin = c_ref[...], s_ref[...]
        k_pe = _rope(kr_ref[...], cos, sin)
        q_all, kv_all = q_ref[...], kv_ref[...]
        for h in range(hpc):
            q = q_all[:, h * MLA_QK:(h + 1) * MLA_QK]
            qo_ref[h] = jnp.concatenate([q[:, :MLA_NOPE], _rope(q[:, MLA_NOPE:], cos, sin)], axis=-1).astype(BF16)
            kv = kv_all[:, h * (MLA_NOPE + MLA_V):(h + 1) * (MLA_NOPE + MLA_V)]
            ko_ref[h] = jnp.concatenate([kv[:, :MLA_NOPE], k_pe], axis=-1).astype(BF16)
            vo_ref[h] = kv[:, MLA_NOPE:].astype(BF16)

    def head_spec(n):
        return pl.BlockSpec((hpc, tr, n), lambda b, i: (b, i, 0))

    def row_spec(n):
        return pl.BlockSpec((tr, n), lambda b, i: (i, 0))

    return pl.pallas_call(
        body, name="mla_prep",
        out_shape=(jax.ShapeDtypeStruct((MLA_HEADS, rows, MLA_QK), BF16), jax.ShapeDtypeStruct((MLA_HEADS, rows, MLA_QK), BF16),
                   jax.ShapeDtypeStruct((MLA_HEADS, rows, MLA_V), BF16)),
        grid=(nb, rows // tr),
        in_specs=[_blk3_spec(tr, hpc * MLA_QK), _blk3_spec(tr, hpc * (MLA_NOPE + MLA_V)), row_spec(MLA_ROPE),
                  row_spec(half), row_spec(half)],
        out_specs=(head_spec(MLA_QK), head_spec(MLA_QK), head_spec(MLA_V)),
        compiler_params=_params("parallel", "parallel"),
    )(q4, kv4, k_rope, cos, sin)


def _mla_unprep(dq, dk, dv, cos, sin):
    _, rows, _ = dq.shape
    tr = _fit(rows, ROW_TILE, 16)
    hpc, half = HEADS_PER_CHIP, MLA_ROPE // 2

    def body(dq_ref, dk_ref, dv_ref, c_ref, s_ref, q4_ref, kv4_ref, kr_ref):
        cos, sin = c_ref[...], s_ref[...]
        d_pe = jnp.zeros((tr, MLA_ROPE), F32)
        q_parts, kv_parts = [], []
        for h in range(hpc):
            g, gk = dq_ref[h], dk_ref[h]
            q_parts += [g[:, :MLA_NOPE], _rope_t(g[:, MLA_NOPE:], cos, sin)]
            kv_parts += [gk[:, :MLA_NOPE], dv_ref[h]]
            d_pe = d_pe + gk[:, MLA_NOPE:]
        q4_ref[...] = jnp.concatenate(q_parts, axis=-1).astype(BF16)
        kv4_ref[...] = jnp.concatenate(kv_parts, axis=-1).astype(BF16)

        @pl.when(pl.program_id(1) == 0)
        def _():
            kr_ref[...] = jnp.zeros_like(kr_ref)

        kr_ref[...] += _rope_t(d_pe, cos, sin)

    def head_spec(n):
        return pl.BlockSpec((hpc, tr, n), lambda i, b: (b, i, 0))

    def row_spec(n):
        return pl.BlockSpec((tr, n), lambda i, b: (i, 0))

    def blk_spec(n):
        return pl.BlockSpec((None, tr, n), lambda i, b: (b, i, 0))

    return pl.pallas_call(
        body, name="mla_unprep",
        out_shape=(jax.ShapeDtypeStruct((N_CHIPS, rows, hpc * MLA_QK), BF16),
                   jax.ShapeDtypeStruct((N_CHIPS, rows, hpc * (MLA_NOPE + MLA_V)), BF16),
                   jax.ShapeDtypeStruct((rows, MLA_ROPE), F32)),
        grid=(rows // tr, N_CHIPS),
        in_specs=[head_spec(MLA_QK), head_spec(MLA_QK), head_spec(MLA_V), row_spec(half), row_spec(half)],
        out_specs=(blk_spec(hpc * MLA_QK), blk_spec(hpc * (MLA_NOPE + MLA_V)), row_spec(MLA_ROPE)),
        compiler_params=_params("parallel", "arbitrary"),
    )(dq, dk, dv, cos, sin)


def _carry(body, comm, n_in, n_out, grid):
    n_ci, n_co = (len(comm.ins), len(comm.out_shapes)) if comm else (0, 0)

    def full(*refs):
        ins, c_in = refs[:n_in], refs[n_in:n_in + n_ci]
        outs = refs[n_in + n_ci:n_in + n_ci + n_out]
        c_out = refs[n_in + n_ci + n_out:n_in + n_ci + n_out + n_co]
        sems = refs[n_in + n_ci + n_out + n_co:]
        first, last = _grid_ends(grid)
        if comm:
            pl.when(first)(lambda: comm.start(c_in, c_out, sems))
        body(ins, outs)
        if comm:
            pl.when(last)(lambda: comm.finish(c_in, c_out, sems))

    extra = dict(
        operands=list(comm.ins) if comm else [], in_specs=[ANY] * n_ci, out_specs=(ANY,) * n_co,
        out_shape=tuple(comm.out_shapes) if comm else (),
        aliases={n_in + i: n_out + o for i, o in comm.aliases.items()} if comm else {},
        scratch=comm.scratch if comm else [])
    return full, extra


def _causal_keep(t):
    return lax.broadcasted_iota(jnp.int32, (t, t), 1) <= lax.broadcasted_iota(jnp.int32, (t, t), 0)


def _mla_fwd(q, k, v, comm=None):
    nh, rows, _ = q.shape
    t = _fit(rows, ATTN_TILE)
    scale = MLA_QK ** -0.5
    grid = (nh, rows // t)

    def body(ins, outs):
        (q_ref, k_ref, v_ref), (o_ref, lse_ref) = ins, outs
        i = pl.program_id(1)
        qb = q_ref[...]

        def step(j, carry, diagonal):
            m, l, acc = carry
            rows_j = pl.ds(pl.multiple_of(j * t, t), t)
            s = _dot(qb, k_ref[rows_j, :], 1, 1) * scale
            if diagonal:
                s = jnp.where(_causal_keep(t), s, MASK_VALUE)
            m_new = jnp.maximum(m, jnp.max(s, axis=-1, keepdims=True))
            alpha = jnp.exp(m - m_new)
            p = jnp.exp(s - m_new)
            l = alpha * l + jnp.sum(p, axis=-1, keepdims=True)
            acc = alpha * acc + _dot(p.astype(BF16), v_ref[rows_j, :], 1, 0)
            return m_new, l, acc

        init = (jnp.full((t, 1), MASK_VALUE, F32), jnp.zeros((t, 1), F32), jnp.zeros((t, MLA_V), F32))
        carry = lax.fori_loop(0, i, lambda j, c: step(j, c, False), init)
        m, l, acc = step(i, carry, True)
        o_ref[...] = acc / l
        lse_ref[...] = m + jnp.log(l)

    full, extra = _carry(body, comm, 3, 2, grid)
    res = pl.pallas_call(
        full, name="mla_fwd",
        out_shape=(jax.ShapeDtypeStruct((rows, nh * MLA_V), F32), jax.ShapeDtypeStruct((nh, rows, 1), F32)) + extra["out_shape"],
        grid=grid,
        in_specs=[pl.BlockSpec((None, t, MLA_QK), lambda h, i: (h, i, 0)),
                  pl.BlockSpec((None, rows, MLA_QK), lambda h, i: (h, 0, 0)),
                  pl.BlockSpec((None, rows, MLA_V), lambda h, i: (h, 0, 0))] + extra["in_specs"],
        out_specs=(pl.BlockSpec((t, MLA_V), lambda h, i: (i, h)), pl.BlockSpec((None, t, 1), lambda h, i: (h, i, 0))) + extra["out_specs"],
        input_output_aliases=extra["aliases"], scratch_shapes=extra["scratch"],
        compiler_params=_params("arbitrary", "arbitrary"),
    )(q, k, v, *extra["operands"])
    return res[0], res[1], res[2:]


def _mla_bwd_dq(q, k, v, do, lse, delta, comm=None):
    nh, rows, _ = q.shape
    t = _fit(rows, ATTN_TILE)
    scale = MLA_QK ** -0.5
    grid = (nh, rows // t)

    def body(ins, outs):
        (q_ref, k_ref, v_ref, do_ref, lse_ref, dl_ref), (dq_ref,) = ins, outs
        i = pl.program_id(1)
        qb, dob, lse_b, dl_b = q_ref[...], do_ref[...], lse_ref[...], dl_ref[...]

        def step(j, dq, diagonal):
            rows_j = pl.ds(pl.multiple_of(j * t, t), t)
            kb = k_ref[rows_j, :]
            s = _dot(qb, kb, 1, 1) * scale
            if diagonal:
                s = jnp.where(_causal_keep(t), s, MASK_VALUE)
            p = jnp.exp(s - lse_b)
            dp = _dot(dob, v_ref[rows_j, :], 1, 1)
            ds = p * (dp - dl_b) * scale
            return dq + _dot(ds.astype(BF16), kb, 1, 0)

        dq = lax.fori_loop(0, i, lambda j, c: step(j, c, False), jnp.zeros((t, MLA_QK), F32))
        dq_ref[...] = step(i, dq, True)

    full, extra = _carry(body, comm, 6, 1, grid)
    res = pl.pallas_call(
        full, name="mla_bwd_dq", out_shape=(jax.ShapeDtypeStruct((nh, rows, MLA_QK), F32),) + extra["out_shape"], grid=grid,
        in_specs=[pl.BlockSpec((None, t, MLA_QK), lambda h, i: (h, i, 0)),
                  pl.BlockSpec((None, rows, MLA_QK), lambda h, i: (h, 0, 0)),
                  pl.BlockSpec((None, rows, MLA_V), lambda h, i: (h, 0, 0)),
                  pl.BlockSpec((t, MLA_V), lambda h, i: (i, h)),
                  pl.BlockSpec((None, t, 1), lambda h, i: (h, i, 0)),
                  pl.BlockSpec((None, t, 1), lambda h, i: (h, i, 0))] + extra["in_specs"],
        out_specs=(pl.BlockSpec((None, t, MLA_QK), lambda h, i: (h, i, 0)),) + extra["out_specs"],
        input_output_aliases=extra["aliases"], scratch_shapes=extra["scratch"],
        compiler_params=_params("arbitrary", "arbitrary"),
    )(q, k, v, do, lse, delta, *extra["operands"])
    return res[0], res[1:]


def _mla_bwd_dkv(q, k, v, do, lse, delta, comm=None):
    nh, rows, _ = q.shape
    t = _fit(rows, ATTN_TILE)
    n_t = rows // t
    scale = MLA_QK ** -0.5
    grid = (nh, n_t)

    def body(ins, outs):
        (q_ref, k_ref, v_ref, do_ref, lse_ref, dl_ref), (dk_ref, dv_ref) = ins, outs
        j = pl.program_id(1)
        kb, vb = k_ref[...], v_ref[...]

        def step(i, carry, diagonal):
            dk, dv = carry
            rows_i = pl.ds(pl.multiple_of(i * t, t), t)
            qb, dob = q_ref[rows_i, :], do_ref[rows_i, :]
            s = _dot(qb, kb, 1, 1) * scale
            if diagonal:
                s = jnp.where(_causal_keep(t), s, MASK_VALUE)
            p = jnp.exp(s - lse_ref[rows_i, :])
            dv = dv + _dot(p.astype(BF16), dob, 0, 0)
            dp = _dot(dob, vb, 1, 1)
            ds = p * (dp - dl_ref[rows_i, :]) * scale
            dk = dk + _dot(ds.astype(BF16), qb, 0, 0)
            return dk, dv

        carry = step(j, (jnp.zeros((t, MLA_QK), F32), jnp.zeros((t, MLA_V), F32)), True)
        dk, dv = lax.fori_loop(j + 1, n_t, lambda i, c: step(i, c, False), carry)
        dk_ref[...] = dk
        dv_ref[...] = dv

    full, extra = _carry(body, comm, 6, 2, grid)
    res = pl.pallas_call(
        full, name="mla_bwd_dkv",
        out_shape=(jax.ShapeDtypeStruct((nh, rows, MLA_QK), F32), jax.ShapeDtypeStruct((nh, rows, MLA_V), F32)) + extra["out_shape"],
        grid=grid,
        in_specs=[pl.BlockSpec((None, rows, MLA_QK), lambda h, j: (h, 0, 0)),
                  pl.BlockSpec((None, t, MLA_QK), lambda h, j: (h, j, 0)),
                  pl.BlockSpec((None, t, MLA_V), lambda h, j: (h, j, 0)),
                  pl.BlockSpec((rows, MLA_V), lambda h, j: (0, h)),
                  pl.BlockSpec((None, rows, 1), lambda h, j: (h, 0, 0)),
                  pl.BlockSpec((None, rows, 1), lambda h, j: (h, 0, 0))] + extra["in_specs"],
        out_specs=(pl.BlockSpec((None, t, MLA_QK), lambda h, j: (h, j, 0)), pl.BlockSpec((None, t, MLA_V), lambda h, j: (h, j, 0))) + extra["out_specs"],
        input_output_aliases=extra["aliases"], scratch_shapes=extra["scratch"],
        compiler_params=_params("arbitrary", "arbitrary"),
    )(q, k, v, do, lse, delta, *extra["operands"])
    return res[0], res[1], res[2:]


def _swa_slope(h):
    return 2.0 ** (-8.0 * (h + 1) / SWA_Q_HEADS)


def _swa_band_specs(rows):
    w = SWA_WINDOW
    kvw = SWA_KV_HEADS * SWA_HEAD_DIM

    def prev(i):
        return jnp.maximum(i - 1, 0)

    return [pl.BlockSpec((w, kvw), lambda i: (prev(i), 0)), pl.BlockSpec((w, kvw), lambda i: (i, 0)),
            pl.BlockSpec((w, kvw), lambda i: (prev(i), 0)), pl.BlockSpec((w, kvw), lambda i: (i, 0)),
            pl.BlockSpec((w, 1), lambda i: (i, 0)),
            pl.BlockSpec((1, w), lambda i: (0, prev(i))), pl.BlockSpec((1, w), lambda i: (0, i)),
            pl.BlockSpec(memory_space=pltpu.SMEM)]


def _swa_bias(i, pc_ref, pp_ref, pn_ref):
    w = SWA_WINDOW
    k_pos = jnp.concatenate([pp_ref[...], pn_ref[...]], axis=1)
    dist = jnp.abs(pc_ref[...] - k_pos).astype(F32)
    r = lax.broadcasted_iota(jnp.int32, (w, 2 * w), 0)
    col = lax.broadcasted_iota(jnp.int32, (w, 2 * w), 1)
    delta = r + w - col
    valid = (delta >= 0) & (delta < w) & ((col >= w) | (i > 0))
    return dist, valid


def _swa_fwd(q, k, v, pos_col, pos_row, sinks):
    rows = q.shape[0]
    w, hd = SWA_WINDOW, SWA_HEAD_DIM
    scale = hd ** -0.5

    def body(q_ref, kp_ref, kc_ref, vp_ref, vc_ref, pc_ref, pp_ref, pn_ref, sink_ref, o_ref, lse_ref):
        dist, valid = _swa_bias(pl.program_id(0), pc_ref, pp_ref, pn_ref)
        for kvh in range(SWA_KV_HEADS):
            cols = slice(kvh * hd, (kvh + 1) * hd)
            kb = jnp.concatenate([kp_ref[:, cols], kc_ref[:, cols]], axis=0)
            vb = jnp.concatenate([vp_ref[:, cols], vc_ref[:, cols]], axis=0)
            for g in range(SWA_GROUP):
                h = kvh * SWA_GROUP + g
                sink = sink_ref[h]
                s = _dot(q_ref[:, h * hd:(h + 1) * hd], kb, 1, 1) * scale - _swa_slope(h) * dist
                s = jnp.where(valid, s, MASK_VALUE)
                m = jnp.maximum(jnp.max(s, axis=-1, keepdims=True), sink)
                e = jnp.exp(s - m)
                den = jnp.sum(e, axis=-1, keepdims=True) + jnp.exp(sink - m)
                o_ref[:, h * hd:(h + 1) * hd] = _dot((e / den).astype(BF16), vb, 1, 0)
                lse_ref[:, h:h + 1] = m + jnp.log(den)

    return pl.pallas_call(
        body, name="swa_fwd",
        out_shape=(jax.ShapeDtypeStruct((rows, MIX_B), F32), jax.ShapeDtypeStruct((rows, SWA_Q_HEADS), F32)),
        grid=(rows // w,),
        in_specs=[pl.BlockSpec((w, MIX_B), lambda i: (i, 0))] + _swa_band_specs(rows),
        out_specs=(pl.BlockSpec((w, MIX_B), lambda i: (i, 0)), pl.BlockSpec((w, SWA_Q_HEADS), lambda i: (i, 0))),
        compiler_params=_params("parallel"),
    )(q, k, k, v, v, pos_col, pos_row, pos_row, sinks)


def _swa_bwd(q, k, v, pos_col, pos_row, sinks, do, lse, delta):
    rows = q.shape[0]
    w, hd = SWA_WINDOW, SWA_HEAD_DIM
    kvw = SWA_KV_HEADS * hd
    scale = hd ** -0.5

    def body(q_ref, kp_ref, kc_ref, vp_ref, vc_ref, pc_ref, pp_ref, pn_ref, sink_ref, do_ref, lse_ref, dl_ref,
             dq_ref, dkc_ref, dkp_ref, dvc_ref, dvp_ref, dsink_ref):
        dist, valid = _swa_bias(pl.program_id(0), pc_ref, pp_ref, pn_ref)

        @pl.when(pl.program_id(0) == 0)
        def _():
            dsink_ref[...] = jnp.zeros_like(dsink_ref)

        for kvh in range(SWA_KV_HEADS):
            cols = slice(kvh * hd, (kvh + 1) * hd)
            kb = jnp.concatenate([kp_ref[:, cols], kc_ref[:, cols]], axis=0)
            vb = jnp.concatenate([vp_ref[:, cols], vc_ref[:, cols]], axis=0)
            dk = jnp.zeros((2 * w, hd), F32)
            dv = jnp.zeros((2 * w, hd), F32)
            for g in range(SWA_GROUP):
                h = kvh * SWA_GROUP + g
                hc = slice(h * hd, (h + 1) * hd)
                qb, dob = q_ref[:, hc], do_ref[:, hc]
                lse_h, dl_h = lse_ref[:, h:h + 1], dl_ref[:, h:h + 1]
                s = _dot(qb, kb, 1, 1) * scale - _swa_slope(h) * dist
                s = jnp.where(valid, s, MASK_VALUE)
                p = jnp.exp(s - lse_h)
                ds = (p * (_dot(dob, vb, 1, 1) - dl_h) * scale).astype(BF16)
                dq_ref[:, hc] = _dot(ds, kb, 1, 0).astype(BF16)
                dk = dk + _dot(ds, qb, 0, 0)
                dv = dv + _dot(p.astype(BF16), dob, 0, 0)
                dsink_ref[:, h:h + 1] += -jnp.sum(jnp.exp(sink_ref[h] - lse_h) * dl_h, axis=0, keepdims=True)
            dkp_ref[:, cols] = dk[:w]
            dkc_ref[:, cols] = dk[w:]
            dvp_ref[:, cols] = dv[:w]
            dvc_ref[:, cols] = dv[w:]

    def blk(n):
        return pl.BlockSpec((w, n), lambda i: (i, 0))

    return pl.pallas_call(
        body, name="swa_bwd",
        out_shape=(jax.ShapeDtypeStruct((rows, MIX_B), BF16),) + (jax.ShapeDtypeStruct((rows, kvw), F32),) * 4
        + (jax.ShapeDtypeStruct((1, SWA_Q_HEADS), F32),),
        grid=(rows // w,),
        in_specs=[blk(MIX_B)] + _swa_band_specs(rows) + [blk(MIX_B), blk(SWA_Q_HEADS), blk(SWA_Q_HEADS)],
        out_specs=(blk(MIX_B), blk(kvw), blk(kvw), blk(kvw), blk(kvw), pl.BlockSpec((1, SWA_Q_HEADS), lambda i: (0, 0))),
        compiler_params=_params("arbitrary"),
    )(q, k, k, v, v, pos_col, pos_row, pos_row, sinks, do, lse, delta)


def _band_merge(cur, prev, name):
    rows, n = cur.shape
    w = SWA_WINDOW
    last = rows // w - 1

    def body(c_ref, p_ref, o_ref):
        nxt = jnp.where(pl.program_id(0) < last, p_ref[...], 0.0)
        o_ref[...] = (c_ref[...] + nxt).astype(BF16)

    return pl.pallas_call(
        body, name=name, out_shape=jax.ShapeDtypeStruct((rows, n), BF16), grid=(rows // w,),
        in_specs=[pl.BlockSpec((w, n), lambda j: (j, 0)), pl.BlockSpec((w, n), lambda j: (jnp.minimum(j + 1, last), 0))],
        out_specs=pl.BlockSpec((w, n), lambda j: (j, 0)), compiler_params=_params("parallel"),
    )(cur, prev)


def _adamw(w, g, m, v, name):
    rows, cols = w.shape
    tr, tc = _fit(rows, ROW_TILE, 8), _fit(cols, MM_TILE)
    c1 = 1.0 - ADAM_B1 ** ADAM_STEP
    c2 = 1.0 - ADAM_B2 ** ADAM_STEP

    def body(w_ref, g_ref, m_ref, v_ref, d_ref, mo_ref, vo_ref):
        gr = g_ref[...]
        m_new = ADAM_B1 * m_ref[...] + (1.0 - ADAM_B1) * gr
        v_new = ADAM_B2 * v_ref[...] + (1.0 - ADAM_B2) * jnp.square(gr)
        mo_ref[...] = m_new
        vo_ref[...] = v_new
        d_ref[...] = -ADAM_LR * ((m_new / c1) / (jnp.sqrt(v_new / c2) + ADAM_EPS) + ADAM_WD * w_ref[...])

    spec = pl.BlockSpec((tr, tc), lambda i, j: (i, j))
    return pl.pallas_call(
        body, name=name, out_shape=(jax.ShapeDtypeStruct(w.shape, F32),) * 3, grid=(rows // tr, cols // tc),
        in_specs=[spec] * 4, out_specs=(spec,) * 3, compiler_params=_params("parallel", "parallel"),
    )(w, g, m, v)


OTHER_CHIPS = ((1, 0), (0, 1), (1, 1))


def _place():
    x, y, c = lax.axis_index("x"), lax.axis_index("y"), lax.axis_index("c")
    return x, y, c


def _flip(v, f):
    return 1 - v if f else v


class _Comm:
    def __init__(self, ins, out_shapes, aliases, sem_sizes, start, finish):
        self.ins, self.out_shapes, self.aliases, self.sem_sizes = list(ins), list(out_shapes), dict(aliases), list(sem_sizes)
        self.start, self.finish = start, finish

    @property
    def scratch(self):
        return [pltpu.SemaphoreType.DMA((n,)) for n in self.sem_sizes]


def _run_comm(comm, name):
    n_in, n_out = len(comm.ins), len(comm.out_shapes)

    def body(*refs):
        ins, outs, sems = refs[:n_in], refs[n_in:n_in + n_out], refs[n_in + n_out:]
        comm.start(ins, outs, sems)
        comm.finish(ins, outs, sems)

    return pl.pallas_call(
        body, name=name, out_shape=tuple(comm.out_shapes), in_specs=[ANY] * n_in, out_specs=(ANY,) * n_out,
        input_output_aliases=comm.aliases, scratch_shapes=comm.scratch,
    )(*comm.ins)


def _grid_ends(grid):
    first = last = None
    for axis, n in enumerate(grid):
        pid = pl.program_id(axis)
        first = (pid == 0) if first is None else first & (pid == 0)
        last = (pid == n - 1) if last is None else last & (pid == n - 1)
    return first, last


def _cast_into_slot(w, name):
    rows, cols = w.shape
    tr, tc = _fit(rows, 704, 16), _fit(cols, MM_TILE)

    def body(w_ref, o_ref):
        o_ref[...] = w_ref[...].astype(BF16)

    return pl.pallas_call(
        body, name=name, out_shape=jax.ShapeDtypeStruct((N_CHIPS, rows, cols), BF16), grid=(rows // tr, cols // tc),
        in_specs=[pl.BlockSpec((tr, tc), lambda i, j: (i, j))],
        out_specs=pl.BlockSpec((None, tr, tc), lambda i, j: (2 * lax.axis_index("x") + lax.axis_index("y"), i, j)),
        compiler_params=_params("parallel", "parallel"),
    )(w)


def _gather_comm(slots):
    n = len(slots)
    pairs = [(a, j) for a in range(n) for j in range(3)]

    def copies(src, dst, sems):
        send, recv, fsend, frecv = sems
        x, y, c = _place()
        me = 2 * x + y

        def rows(a, core):
            half = slots[a].shape[1] // 2
            return pl.ds(core * half, half)

        def chip(j):
            return _flip(x, OTHER_CHIPS[j][0]), _flip(y, OTHER_CHIPS[j][1])

        def out(a, j):
            px, py = chip(j)
            return pltpu.make_async_remote_copy(
                src_ref=src[a].at[me, rows(a, c)], dst_ref=dst[a].at[me, rows(a, c)], send_sem=send.at[a * 3 + j],
                recv_sem=recv.at[a * 3 + j], device_id=(px, py, c), device_id_type=MESH)

        def landed(a, j):
            px, py = chip(j)
            blk = 2 * px + py
            return pltpu.make_async_remote_copy(
                src_ref=dst[a].at[blk, rows(a, c)], dst_ref=dst[a].at[blk, rows(a, c)], send_sem=send.at[a * 3 + j],
                recv_sem=recv.at[a * 3 + j], device_id=(x, y, c), device_id_type=MESH)

        def passed(a, j, core):
            px, py = chip(j)
            blk = 2 * px + py
            return pltpu.make_async_remote_copy(
                src_ref=dst[a].at[blk, rows(a, core)], dst_ref=dst[a].at[blk, rows(a, core)], send_sem=fsend.at[a * 3 + j],
                recv_sem=frecv.at[a * 3 + j], device_id=(x, y, 1 - c), device_id_type=MESH)

        return c, out, landed, passed

    def start(src, dst, sems):
        _, out, _, _ = copies(src, dst, sems)
        for a, j in pairs:
            out(a, j).start()

    def finish(src, dst, sems):
        c, out, landed, passed = copies(src, dst, sems)
        for a, j in pairs:
            landed(a, j).wait_recv()
            passed(a, j, c).start()
        for a, j in pairs:
            passed(a, j, 1 - c).wait_recv()
        for a, j in pairs:
            out(a, j).wait_send()
            passed(a, j, c).wait_send()

    shapes = [jax.ShapeDtypeStruct(s.shape, s.dtype) for s in slots]
    return _Comm(slots, shapes, {a: a for a in range(n)}, [3 * n] * 4, start, finish)


def _pair_exchange_comm(grads):
    n = len(grads)

    def copy(src, dst, sems, a):
        x, y, c = _place()
        half = grads[a].shape[1] // 2
        return pltpu.make_async_remote_copy(
            src_ref=src[a].at[:, pl.ds((1 - c) * half, half)], dst_ref=dst[a], send_sem=sems[0].at[a],
            recv_sem=sems[1].at[a], device_id=(x, y, 1 - c), device_id_type=MESH)

    def start(src, dst, sems):
        for a in range(n):
            copy(src, dst, sems, a).start()

    def finish(src, dst, sems):
        for a in range(n):
            copy(src, dst, sems, a).wait()

    shapes = [jax.ShapeDtypeStruct((g.shape[0], g.shape[1] // 2, g.shape[2]), g.dtype) for g in grads]
    return _Comm(grads, shapes, {}, [n, n], start, finish)


def _pair_add(mine, theirs, name):
    nb, rows, cols = theirs.shape
    tr, tc = _fit(rows, 704, 16), _fit(cols, MM_TILE)
    n_r = rows // tr

    def body(a_ref, b_ref, lo_ref, own_ref):
        s = a_ref[...] + b_ref[...]
        lo_ref[...] = s.astype(BF16)

        @pl.when(pl.program_id(2) == 2 * lax.axis_index("x") + lax.axis_index("y"))
        def _():
            own_ref[...] = s

    return pl.pallas_call(
        body, name=name, grid=(n_r, cols // tc, nb),
        in_specs=[pl.BlockSpec((None, tr, tc), lambda i, j, b: (b, lax.axis_index("c") * n_r + i, j)),
                  pl.BlockSpec((None, tr, tc), lambda i, j, b: (b, i, j))],
        out_specs=(pl.BlockSpec((None, tr, tc), lambda i, j, b: (b, i, j)), pl.BlockSpec((tr, tc), lambda i, j, b: (i, j))),
        out_shape=(jax.ShapeDtypeStruct(theirs.shape, BF16), jax.ShapeDtypeStruct((rows, cols), F32)),
        compiler_params=_params("parallel", "parallel", "arbitrary"),
    )(mine, theirs)


def _chip_exchange_comm(sums):
    n = len(sums)
    pairs = [(a, j) for a in range(n) for j in range(3)]

    def copy(src, dst, sems, a, j):
        x, y, c = _place()
        px, py = _flip(x, OTHER_CHIPS[j][0]), _flip(y, OTHER_CHIPS[j][1])
        return pltpu.make_async_remote_copy(
            src_ref=src[a].at[2 * px + py], dst_ref=dst[a].at[j], send_sem=sems[0].at[a * 3 + j],
            recv_sem=sems[1].at[a * 3 + j], device_id=(px, py, c), device_id_type=MESH)

    def start(src, dst, sems):
        for a, j in pairs:
            copy(src, dst, sems, a, j).start()

    def finish(src, dst, sems):
        for a, j in pairs:
            copy(src, dst, sems, a, j).wait()

    shapes = [jax.ShapeDtypeStruct((3,) + s.shape[1:], s.dtype) for s in sums]
    return _Comm(sums, shapes, {}, [3 * n, 3 * n], start, finish)


def _chip_add(own, got, name):
    rows, cols = own.shape
    tr, tc = _fit(rows, 704, 16), _fit(cols, MM_TILE)
    n_r = rows // tr

    def body(o_ref, g_ref, out_ref):
        out_ref[...] = ((o_ref[...] + g_ref[0].astype(F32)) + g_ref[1].astype(F32)) + g_ref[2].astype(F32)

    return pl.pallas_call(
        body, name=name, out_shape=jax.ShapeDtypeStruct((2 * rows, cols), F32), grid=(n_r, cols // tc),
        in_specs=[pl.BlockSpec((tr, tc), lambda i, j: (i, j)), pl.BlockSpec((3, tr, tc), lambda i, j: (0, i, j))],
        out_specs=pl.BlockSpec((tr, tc), lambda i, j: (lax.axis_index("c") * n_r + i, j)),
        compiler_params=_params("parallel", "parallel"),
    )(own, got)


def _pair_share_comm(grads):
    n = len(grads)

    def copy(src, dst, sems, a, mine):
        x, y, c = _place()
        half = grads[a].shape[0] // 2
        rows = pl.ds((c if mine else 1 - c) * half, half)
        return pltpu.make_async_remote_copy(
            src_ref=src[a].at[rows], dst_ref=dst[a].at[rows], send_sem=sems[0].at[a], recv_sem=sems[1].at[a],
            device_id=(x, y, 1 - c), device_id_type=MESH)

    def start(src, dst, sems):
        for a in range(n):
            copy(src, dst, sems, a, True).start()

    def finish(src, dst, sems):
        for a in range(n):
            copy(src, dst, sems, a, False).wait_recv()
            copy(src, dst, sems, a, True).wait_send()

    shapes = [jax.ShapeDtypeStruct(g.shape, g.dtype) for g in grads]
    return _Comm(grads, shapes, {a: a for a in range(n)}, [n, n], start, finish)


def _all_sum_small(vec):
    r, n = vec.shape
    flips = [(a, b, d) for a in (0, 1) for b in (0, 1) for d in (0, 1)][1:]

    def body(v_ref, o_ref, buf, send, recv):
        x, y, c = _place()
        me = 4 * x + 2 * y + c
        cps = []
        for k, (fx, fy, fc) in enumerate(flips):
            cp = pltpu.make_async_remote_copy(
                src_ref=v_ref, dst_ref=buf.at[me], send_sem=send.at[k], recv_sem=recv.at[k],
                device_id=(_flip(x, fx), _flip(y, fy), _flip(c, fc)), device_id_type=MESH)
            cp.start()
            cps.append(cp)
        buf[me] = v_ref[...]
        for k, (fx, fy, fc) in enumerate(flips):
            peer = 4 * _flip(x, fx) + 2 * _flip(y, fy) + _flip(c, fc)
            pltpu.make_async_remote_copy(
                src_ref=v_ref, dst_ref=buf.at[peer], send_sem=send.at[k], recv_sem=recv.at[k],
                device_id=(x, y, c), device_id_type=MESH).wait_recv()
        for cp in cps:
            cp.wait_send()
        acc = buf[0]
        for d in range(1, 8):
            acc = acc + buf[d]
        o_ref[...] = acc

    return pl.pallas_call(
        body, name="all_sum_small", out_shape=jax.ShapeDtypeStruct((r, n), F32),
        in_specs=[pl.BlockSpec(memory_space=pltpu.VMEM)], out_specs=pl.BlockSpec(memory_space=pltpu.VMEM),
        scratch_shapes=[pltpu.VMEM((8, r, n), F32), pltpu.SemaphoreType.DMA((7,)), pltpu.SemaphoreType.DMA((7,))],
    )(vec)


SMALL = ("attn_pre_g", "q_norm_g", "kv_norm_g", "swa_sinks", "grp_a_g", "grp_b_g", "attn_post_g", "ffn_pre_g", "ffn_post_g")
BIG = ("w_in", "w_uq", "w_ukv", "w_o", "w_gate", "w_up", "w_down")
ORDER = ("attn_pre_g", "w_in", "q_norm_g", "w_uq", "kv_norm_g", "w_ukv", "swa_sinks", "grp_a_g", "grp_b_g", "w_o",
         "attn_post_g", "ffn_pre_g", "w_gate", "w_up", "w_down", "ffn_post_g")


def _pad_lanes(v):
    n = v.shape[1]
    return jnp.pad(v, ((0, 0), (0, -n % LANES)))


def kernel(x, positions, attn_pre_g, w_in, q_norm_g, w_uq, kv_norm_g, w_ukv, swa_sinks, grp_a_g, grp_b_g, w_o, attn_post_g, ffn_pre_g, w_gate, w_up, w_down, ffn_post_g, loss_target, m_attn_pre_g, m_w_in, m_q_norm_g, m_w_uq, m_kv_norm_g, m_w_ukv, m_swa_sinks, m_grp_a_g, m_grp_b_g, m_w_o, m_attn_post_g, m_ffn_pre_g, m_w_gate, m_w_up, m_w_down, m_ffn_post_g, v_attn_pre_g, v_w_in, v_q_norm_g, v_w_uq, v_kv_norm_g, v_w_ukv, v_swa_sinks, v_grp_a_g, v_grp_b_g, v_w_o, v_attn_post_g, v_ffn_pre_g, v_w_gate, v_w_up, v_w_down, v_ffn_post_g):
    given = dict(locals())
    w32 = {k: given[k][0] for k in BIG}
    gains = {k: given[k] for k in SMALL}
    xs, tgt = x[0], loss_target[0]
    seq, d_model = xs.shape
    q_rank, kv_rank = q_norm_g.shape[1], kv_norm_g.shape[1]
    kvw = SWA_KV_HEADS * SWA_HEAD_DIM

    slot = {k: _cast_into_slot(w32[k], "cast_" + k) for k in BIG}
    full = {}
    (full["w_in"],) = _run_comm(_gather_comm([slot["w_in"]]), "gather_w_in")

    pos = positions[0]
    inv = 1.0 / (ROPE_THETA ** (jnp.arange(0, MLA_ROPE, 2, dtype=F32) / MLA_ROPE))
    ang = pos.astype(F32)[:, None] * inv
    cos, sin = jnp.cos(ang), jnp.sin(ang)
    pos_col, pos_row = pos[:, None], pos[None, :]
    sinks = swa_sinks[0]

    a = _norm_fwd(xs, attn_pre_g, BF16, "attn_pre_norm")
    proj4, (full["w_uq"], full["w_ukv"], full["w_o"]) = _matmul(
        a, full["w_in"], name="in_proj", comm=_gather_comm([slot["w_uq"], slot["w_ukv"], slot["w_o"]]))
    proj = jnp.concatenate([proj4[b] for b in range(N_CHIPS)], axis=1)
    cuts = (0, q_rank, q_rank + kv_rank, q_rank + kv_rank + MLA_ROPE)
    cuts = cuts + (cuts[3] + MIX_B, cuts[3] + MIX_B + kvw, cuts[3] + MIX_B + 2 * kvw)
    c_q, c_kv, k_rope, q_s, k_s, v_s = (proj[:, lo:hi] for lo, hi in zip(cuts[:-1], cuts[1:]))
    cqn = _norm_fwd(c_q, q_norm_g, BF16, "q_norm")
    ckvn = _norm_fwd(c_kv, kv_norm_g, BF16, "kv_norm")
    q4 = _matmul(cqn, full["w_uq"], name="q_up")
    kv4 = _matmul(ckvn, full["w_ukv"], name="kv_up")
    qh, kh, vh = _mla_prep(q4, kv4, k_rope, cos, sin)
    o_a, lse_a, (full["w_gate"],) = _mla_fwd(qh, kh, vh, comm=_gather_comm([slot["w_gate"]]))
    q_sb, k_sb, v_sb = q_s.astype(BF16), k_s.astype(BF16), v_s.astype(BF16)
    o_b, lse_b = _swa_fwd(q_sb, k_sb, v_sb, pos_col, pos_row, sinks)
    mix = _mix_fwd(o_a, o_b, grp_a_g, grp_b_g)
    w_o_full = full["w_o"].reshape(N_CHIPS * full["w_o"].shape[1], d_model)
    ao = _matmul(mix, w_o_full, name="out_proj")
    h1, f = _post_pre_fwd(xs, ao, attn_post_g, ffn_pre_g)
    gate, (full["w_up"],) = _matmul(f, full["w_gate"], name="ffn_gate", comm=_gather_comm([slot["w_up"]]))
    up, (full["w_down"],) = _matmul(f, full["w_up"], name="ffn_up", comm=_gather_comm([slot["w_down"]]))
    act = _swiglu_fwd(gate, up)
    dn = _matmul(act, full["w_down"], reduce_b=True, name="ffn_down")
    loss_row, dy, ddn, d_ffn_post = _loss_bwd(h1, dn, ffn_post_g, tgt)

    dact = _matmul(ddn, full["w_down"], tb=True, name="ffn_down_dx")
    dw_down = _matmul(act, ddn, ta=True, name="ffn_down_dw", tn=512)
    dgate, dup = _swiglu_bwd(gate, up, dact)
    dw_gate = _matmul(f, dgate, ta=True, name="ffn_gate_dw", tm=512)
    dw_up = _matmul(f, dup, ta=True, name="ffn_up_dw", tm=512)
    ffn = ("w_down", "w_gate", "w_up")
    grads4 = dict(w_down=dw_down, w_gate=dw_gate, w_up=dw_up)
    df_g, theirs = _matmul(dgate, full["w_gate"], tb=True, reduce_b=True, name="ffn_gate_dx",
                           comm=_pair_exchange_comm([grads4[k] for k in ffn]))
    low, own = {}, {}
    for k, t in zip(ffn, theirs):
        low[k], own[k] = _pair_add(grads4[k], t, "pair_add_" + k)
    df_u = _matmul(dup, full["w_up"], tb=True, reduce_b=True, name="ffn_up_dx")
    dh1, d_ffn_pre = _norm_bwd(h1, ffn_pre_g, [df_g, df_u], [dy], F32, "ffn_pre_norm_bwd")
    dao, d_attn_post = _norm_bwd(ao, attn_post_g, [dh1], [], BF16, "attn_post_norm_bwd")
    dmix = _matmul(dao, w_o_full, tb=True, name="out_proj_dx")
    dw_o = _matmul(mix, dao, ta=True, name="out_proj_dw")
    do_a, do_b, d_grp_a, d_grp_b, dl_a, dl_b = _mix_bwd(o_a, o_b, grp_a_g, grp_b_g, dmix)
    dl_a = dl_a.T[:, :, None]
    got = {}
    dqh, (got["w_down"],) = _mla_bwd_dq(qh, kh, vh, do_a, lse_a, dl_a, comm=_chip_exchange_comm([low["w_down"]]))
    dkh, dvh, (got["w_gate"], got["w_up"]) = _mla_bwd_dkv(
        qh, kh, vh, do_a, lse_a, dl_a, comm=_chip_exchange_comm([low["w_gate"], low["w_up"]]))
    dq4, dkv4, dk_rope = _mla_unprep(dqh, dkh, dvh, cos, sin)
    dw_uq = _matmul(cqn, dq4, ta=True, name="q_up_dw")
    dcqn = _matmul(dq4, full["w_uq"], tb=True, reduce_b=True, name="q_up_dx")
    dw_ukv = _matmul(ckvn, dkv4, ta=True, name="kv_up_dw")
    dckvn = _matmul(dkv4, full["w_ukv"], tb=True, reduce_b=True, name="kv_up_dx")
    dc_q, d_q_norm = _norm_bwd(c_q, q_norm_g, [dcqn], [], BF16, "q_norm_bwd")
    dc_kv, d_kv_norm = _norm_bwd(c_kv, kv_norm_g, [dckvn], [], BF16, "kv_norm_bwd")
    dq_s, dk_cur, dk_prev, dv_cur, dv_prev, d_sinks = _swa_bwd(q_sb, k_sb, v_sb, pos_col, pos_row, sinks, do_b, lse_b, dl_b)
    dk_s = _band_merge(dk_cur, dk_prev, "swa_dk_merge")
    dv_s = _band_merge(dv_cur, dv_prev, "swa_dv_merge")
    dproj = jnp.concatenate([dc_q, dc_kv, dk_rope.astype(BF16), dq_s, dk_s, dv_s], axis=1)
    blk_w = dproj.shape[1] // N_CHIPS
    dproj4 = jnp.stack([dproj[:, b * blk_w:(b + 1) * blk_w] for b in range(N_CHIPS)])
    mid = ("w_o", "w_uq", "w_ukv")
    grads4.update(w_o=dw_o.reshape(N_CHIPS, -1, d_model), w_uq=dw_uq, w_ukv=dw_ukv)
    dw_in, theirs = _matmul(a, dproj4, ta=True, name="in_proj_dw", tm=512,
                            comm=_pair_exchange_comm([grads4[k] for k in mid]))
    for k, t in zip(mid, theirs):
        low[k], own[k] = _pair_add(grads4[k], t, "pair_add_" + k)
    da, got_mid = _matmul(dproj4, full["w_in"], tb=True, reduce_b=True, name="in_proj_dx",
                          comm=_chip_exchange_comm([low[k] for k in mid]))
    got.update(zip(mid, got_mid))
    dx, d_attn_pre = _norm_bwd(xs, attn_pre_g, [da], [dh1], F32, "attn_pre_norm_bwd")

    small_grads = dict(attn_pre_g=d_attn_pre, q_norm_g=d_q_norm, kv_norm_g=d_kv_norm, swa_sinks=d_sinks, grp_a_g=d_grp_a,
                       grp_b_g=d_grp_b, attn_post_g=d_attn_post, ffn_pre_g=d_ffn_pre, ffn_post_g=d_ffn_post)
    parts = [loss_row] + [_pad_lanes(small_grads[k]) for k in SMALL]
    packed = jnp.concatenate(parts, axis=1)
    n_packed = packed.shape[1]
    packed = jnp.pad(packed, ((0, 0), (0, -n_packed % (8 * LANES)))).reshape(8, -1)
    total = _all_sum_small(packed).reshape(1, -1)
    loss = total[0, 0]
    g_small, off = {}, LANES
    for k in SMALL:
        n = gains[k].shape[1]
        g_small[k] = total[:, off:off + n]
        off += n + (-n % LANES)

    def pack_small(prefix):
        return jnp.concatenate([_pad_lanes(given[prefix + k]) for k in SMALL], axis=1)

    d_sm, m_sm, v_sm = _adamw(pack_small(""), total[:, LANES:n_packed], pack_small("m_"), pack_small("v_"), "adamw_small")
    delta, new_m, new_v, off = {}, {}, {}, 0
    for k in SMALL:
        n = gains[k].shape[1]
        delta[k], new_m[k], new_v[k] = d_sm[:, off:off + n], m_sm[:, off:off + n], v_sm[:, off:off + n]
        off += n + (-n % LANES)

    grads4["w_in"] = dw_in
    (theirs_in,) = _run_comm(_pair_exchange_comm([dw_in]), "grad_pair_exchange")
    low["w_in"], own["w_in"] = _pair_add(dw_in, theirs_in, "pair_add_w_in")
    (got["w_in"],) = _run_comm(_chip_exchange_comm([low["w_in"]]), "grad_chip_exchange")
    halves = [_chip_add(own[k], got[k], "chip_add_" + k) for k in BIG]
    g_big = dict(zip(BIG, _run_comm(_pair_share_comm(halves), "grad_pair_share")))
    for k in BIG:
        delta[k], new_m[k], new_v[k] = _adamw(w32[k], g_big[k], given["m_" + k][0], given["v_" + k][0], "adamw_" + k)

    def out(d, k):
        return d[k][None] if k in BIG else d[k]

    grads = {**g_small, **g_big}
    return (loss, dx[None], *[out(grads, k) for k in ORDER], *[out(delta, k) for k in ORDER],
            *[out(new_m, k) for k in ORDER], *[out(new_v, k) for k in ORDER])
```

```python
import functools
import math

import jax
import jax.numpy as jnp
from jax import lax
from jax.experimental import pallas as pl
from jax.experimental.pallas import tpu as pltpu

F32, BF16 = jnp.float32, jnp.bfloat16
MESH = pl.DeviceIdType.MESH
ANY = pl.BlockSpec(memory_space=pl.ANY)

N_CHIPS = 4
EPS = 1e-6
MLA_HEADS, MLA_NOPE, MLA_ROPE, MLA_V = 16, 128, 64, 128
MLA_QK = MLA_NOPE + MLA_ROPE
HEADS_PER_CHIP = MLA_HEADS // N_CHIPS
ROPE_THETA = 10000.0
SWA_Q_HEADS, SWA_KV_HEADS, SWA_HEAD_DIM, SWA_WINDOW = 32, 8, 64, 128
SWA_GROUP = SWA_Q_HEADS // SWA_KV_HEADS
MIX_A, MIX_B = MLA_HEADS * MLA_V, SWA_Q_HEADS * SWA_HEAD_DIM
MASK_VALUE = float(jnp.finfo(jnp.float32).min)
ADAM_LR, ADAM_B1, ADAM_B2, ADAM_EPS, ADAM_WD, ADAM_STEP = 0.001, 0.9, 0.999, 1e-08, 0.01, 10

LANES = 128
VMEM_LIMIT = 56 << 20
ATTN_TILE = 512
ROW_TILE = 256
MM_TILE = 1024


def _fit(dim, pref, mult=LANES):
    if dim <= pref:
        return dim
    for t in range(pref - pref % mult, 0, -mult):
        if dim % t == 0:
            return t
    return dim


def _params(*semantics):
    return pltpu.CompilerParams(dimension_semantics=semantics, vmem_limit_bytes=VMEM_LIMIT)


def _dot(a, b, ca, cb):
    return lax.dot_general(a, b, (((ca,), (cb,)), ((), ())), preferred_element_type=F32)


def _matmul(a, b, *, name, ta=False, tb=False, reduce_b=False, out_dtype=F32, tm=MM_TILE, tn=MM_TILE, tk=MM_TILE, comm=None,
            epilogue=None, extras=()):
    a3, b3 = a.ndim == 3, b.ndim == 3
    nb = a.shape[0] if a3 else (b.shape[0] if b3 else 1)
    (K, M) = a.shape[-2:] if ta else a.shape[-2:][::-1]
    (N, K2) = b.shape[-2:] if tb else b.shape[-2:][::-1]
    assert K == K2, (a.shape, b.shape)
    tm, tn, tk = _fit(M, tm), _fit(N, tn), _fit(K, tk)
    batched_out = (a3 or b3) and not reduce_b
    n_bo = nb if batched_out else 1
    n_br = nb if reduce_b else 1
    nk = K // tk

    def sel(bo, br):
        return br if reduce_b else bo

    def a_map(bo, i, j, br, k):
        t = (k, i) if ta else (i, k)
        return (sel(bo, br),) + t if a3 else t

    def b_map(bo, i, j, br, k):
        t = (j, k) if tb else (k, j)
        return (sel(bo, br),) + t if b3 else t

    def o_map(bo, i, j, br, k):
        return (bo, i, j) if batched_out else (i, j)

    a_blk = (tk, tm) if ta else (tm, tk)
    b_blk = (tn, tk) if tb else (tk, tn)
    a_blk = (None,) + a_blk if a3 else a_blk
    b_blk = (None,) + b_blk if b3 else b_blk
    o_blk = (None, tm, tn) if batched_out else (tm, tn)
    o_shape = (nb, M, N) if batched_out else (M, N)

    grid = (n_bo, M // tm, N // tn, n_br, nk)
    n_ci, n_co = (len(comm.ins), len(comm.out_shapes)) if comm else (0, 0)
    n_x = len(extras)
    out_dtypes = tuple(out_dtype) if epilogue else (out_dtype,)
    n_o = len(out_dtypes)

    def body(*refs):
        a_ref, b_ref = refs[:2]
        x_refs, c_in = refs[2:2 + n_x], refs[2 + n_x:2 + n_x + n_ci]
        refs = refs[2 + n_x + n_ci:]
        o_refs, c_out, acc_ref, sems = refs[:n_o], refs[n_o:n_o + n_co], refs[n_o + n_co], refs[n_o + n_co + 1:]
        br, k = pl.program_id(3), pl.program_id(4)
        first, last = _grid_ends(grid)
        if comm:
            pl.when(first)(lambda: comm.start(c_in, c_out, sems))

        @pl.when((br == 0) & (k == 0))
        def _():
            acc_ref[...] = jnp.zeros_like(acc_ref)

        acc_ref[...] += _dot(a_ref[...], b_ref[...], 0 if ta else 1, 1 if tb else 0)

        @pl.when((br == n_br - 1) & (k == nk - 1))
        def _():
            vals = epilogue(acc_ref[...], *[r[...] for r in x_refs]) if epilogue else (acc_ref[...],)
            for o_ref, v in zip(o_refs, vals):
                o_ref[...] = v.astype(o_ref.dtype)

        if comm:
            pl.when(last)(lambda: comm.finish(c_in, c_out, sems))

    o_spec = pl.BlockSpec(o_blk, o_map)
    res = pl.pallas_call(
        body, name=name,
        out_shape=tuple(jax.ShapeDtypeStruct(o_shape, d) for d in out_dtypes) + tuple(comm.out_shapes if comm else ()),
        grid=grid,
        in_specs=[pl.BlockSpec(a_blk, a_map), pl.BlockSpec(b_blk, b_map)] + [o_spec] * n_x + [ANY] * n_ci,
        out_specs=(o_spec,) * n_o + (ANY,) * n_co,
        input_output_aliases={2 + n_x + i: n_o + o for i, o in comm.aliases.items()} if comm else {},
        scratch_shapes=[pltpu.VMEM((tm, tn), F32)] + (comm.scratch if comm else []),
        compiler_params=_params(*(["parallel"] * 3 + ["arbitrary"] * 2 if not comm else ["arbitrary"] * 5)),
    )(a, b, *extras, *(comm.ins if comm else ()))
    main = res[:n_o] if epilogue else res[0]
    return (main, res[n_o:]) if comm else main


def _inv_rms(u):
    return lax.rsqrt(jnp.mean(u * u, axis=-1, keepdims=True) + EPS)


def _norm_bwd_math(u, g, dz):
    r = _inv_rms(u)
    w = dz * g
    du = r * w - u * (r * r * r * jnp.mean(w * u, axis=-1, keepdims=True))
    dg = jnp.sum(dz * (u * r), axis=0, keepdims=True)
    return du, dg


def _row_spec(tr, n):
    return pl.BlockSpec((tr, n), lambda i: (i, 0))


def _gain_spec(n):
    return pl.BlockSpec((1, n), lambda i: (0, 0))


def _norm_fwd(u, g, out_dtype, name):
    rows, n = u.shape
    tr = _fit(rows, ROW_TILE, 16)

    def body(u_ref, g_ref, o_ref):
        x = u_ref[...]
        o_ref[...] = (x * _inv_rms(x) * g_ref[...]).astype(o_ref.dtype)

    return pl.pallas_call(
        body, name=name, out_shape=jax.ShapeDtypeStruct((rows, n), out_dtype), grid=(rows // tr,),
        in_specs=[_row_spec(tr, n), _gain_spec(n)], out_specs=_row_spec(tr, n), compiler_params=_params("parallel"),
    )(u, g)


def _norm_bwd(u, g, dzs, adds, out_dtype, name):
    rows, n = u.shape
    tr = _fit(rows, ROW_TILE, 16)
    n_dz, n_add = len(dzs), len(adds)

    def body(*refs):
        u_ref, g_ref = refs[:2]
        dz_refs = refs[2:2 + n_dz]
        add_refs = refs[2 + n_dz:2 + n_dz + n_add]
        du_ref, dg_ref = refs[2 + n_dz + n_add:]
        dz = dz_refs[0][...].astype(F32)
        for r in dz_refs[1:]:
            dz = dz + r[...].astype(F32)
        du, dg = _norm_bwd_math(u_ref[...], g_ref[...], dz)
        for r in add_refs:
            du = du + r[...]
        du_ref[...] = du.astype(du_ref.dtype)

        @pl.when(pl.program_id(0) == 0)
        def _():
            dg_ref[...] = jnp.zeros_like(dg_ref)

        dg_ref[...] += dg

    return pl.pallas_call(
        body, name=name,
        out_shape=(jax.ShapeDtypeStruct((rows, n), out_dtype), jax.ShapeDtypeStruct((1, n), F32)), grid=(rows // tr,),
        in_specs=[_row_spec(tr, n), _gain_spec(n)] + [_row_spec(tr, n)] * (n_dz + n_add),
        out_specs=(_row_spec(tr, n), _gain_spec(n)), compiler_params=_params("arbitrary"),
    )(u, g, *dzs, *adds)


def _mix_fwd(o_a, o_b, g_a, g_b):
    rows = o_a.shape[0]
    tr = _fit(rows, ROW_TILE, 16)

    def body(a_ref, b_ref, ga_ref, gb_ref, o_ref):
        a, b = a_ref[...], b_ref[...]
        o_ref[:, :MIX_A] = (a * _inv_rms(a) * ga_ref[...]).astype(BF16)
        o_ref[:, MIX_A:] = (b * _inv_rms(b) * gb_ref[...]).astype(BF16)

    return pl.pallas_call(
        body, name="mix_fwd", out_shape=jax.ShapeDtypeStruct((rows, MIX_A + MIX_B), BF16), grid=(rows // tr,),
        in_specs=[_row_spec(tr, MIX_A), _row_spec(tr, MIX_B), _gain_spec(MIX_A), _gain_spec(MIX_B)],
        out_specs=_row_spec(tr, MIX_A + MIX_B), compiler_params=_params("parallel"),
    )(o_a, o_b, g_a, g_b)


def _mix_bwd(o_a, o_b, g_a, g_b, dmix):
    rows = o_a.shape[0]
    tr = _fit(rows, ROW_TILE, 16)

    def body(a_ref, b_ref, ga_ref, gb_ref, dm_ref, doa_ref, dob_ref, dga_ref, dgb_ref, dla_ref, dlb_ref):
        a, b = a_ref[...], b_ref[...]
        doa, dga = _norm_bwd_math(a, ga_ref[...], dm_ref[:, :MIX_A])
        dob, dgb = _norm_bwd_math(b, gb_ref[...], dm_ref[:, MIX_A:])
        doa_ref[...] = doa.astype(BF16)
        dob_ref[...] = dob.astype(BF16)
        pa, pb = doa * a, dob * b
        for h in range(MLA_HEADS):
            dla_ref[:, h:h + 1] = jnp.sum(pa[:, h * MLA_V:(h + 1) * MLA_V], axis=-1, keepdims=True)
        for h in range(SWA_Q_HEADS):
            dlb_ref[:, h:h + 1] = jnp.sum(pb[:, h * SWA_HEAD_DIM:(h + 1) * SWA_HEAD_DIM], axis=-1, keepdims=True)

        @pl.when(pl.program_id(0) == 0)
        def _():
            dga_ref[...] = jnp.zeros_like(dga_ref)
            dgb_ref[...] = jnp.zeros_like(dgb_ref)

        dga_ref[...] += dga
        dgb_ref[...] += dgb

    return pl.pallas_call(
        body, name="mix_bwd",
        out_shape=(jax.ShapeDtypeStruct((rows, MIX_A), BF16), jax.ShapeDtypeStruct((rows, MIX_B), BF16),
                   jax.ShapeDtypeStruct((1, MIX_A), F32), jax.ShapeDtypeStruct((1, MIX_B), F32),
                   jax.ShapeDtypeStruct((rows, MLA_HEADS), F32), jax.ShapeDtypeStruct((rows, SWA_Q_HEADS), F32)),
        grid=(rows // tr,),
        in_specs=[_row_spec(tr, MIX_A), _row_spec(tr, MIX_B), _gain_spec(MIX_A), _gain_spec(MIX_B),
                  _row_spec(tr, MIX_A + MIX_B)],
        out_specs=(_row_spec(tr, MIX_A), _row_spec(tr, MIX_B), _gain_spec(MIX_A), _gain_spec(MIX_B),
                   _row_spec(tr, MLA_HEADS), _row_spec(tr, SWA_Q_HEADS)),
        compiler_params=_params("arbitrary"),
    )(o_a, o_b, g_a, g_b, dmix)


def _post_pre_fwd(x, ao, g_post, g_pre):
    rows, n = x.shape
    tr = _fit(rows, ROW_TILE, 16)

    def body(x_ref, ao_ref, g1_ref, g2_ref, h_ref, f_ref):
        u = ao_ref[...]
        h = x_ref[...] + u * _inv_rms(u) * g1_ref[...]
        h_ref[...] = h
        f_ref[...] = (h * _inv_rms(h) * g2_ref[...]).astype(BF16)

    return pl.pallas_call(
        body, name="post_pre_fwd",
        out_shape=(jax.ShapeDtypeStruct((rows, n), F32), jax.ShapeDtypeStruct((rows, n), BF16)), grid=(rows // tr,),
        in_specs=[_row_spec(tr, n), _row_spec(tr, n), _gain_spec(n), _gain_spec(n)],
        out_specs=(_row_spec(tr, n), _row_spec(tr, n)), compiler_params=_params("parallel"),
    )(x, ao, g_post, g_pre)


def _loss_bwd(h1, dn, g_post, target):
    rows, n = h1.shape
    tr = _fit(rows, ROW_TILE, 16)

    def body(h_ref, u_ref, g_ref, t_ref, loss_ref, dy_ref, du_ref, dg_ref):
        u, g = u_ref[...], g_ref[...]
        err = h_ref[...] + u * _inv_rms(u) * g - t_ref[...]
        dy = err / n
        dy_ref[...] = dy
        du, dg = _norm_bwd_math(u, g, dy)
        du_ref[...] = du.astype(BF16)

        @pl.when(pl.program_id(0) == 0)
        def _():
            loss_ref[...] = jnp.zeros_like(loss_ref)
            dg_ref[...] = jnp.zeros_like(dg_ref)

        loss_ref[...] += jnp.full((1, LANES), 0.5 * jnp.sum(jnp.mean(err * err, axis=-1)), F32)
        dg_ref[...] += dg

    return pl.pallas_call(
        body, name="loss_bwd",
        out_shape=(jax.ShapeDtypeStruct((1, LANES), F32), jax.ShapeDtypeStruct((rows, n), F32),
                   jax.ShapeDtypeStruct((rows, n), BF16), jax.ShapeDtypeStruct((1, n), F32)),
        grid=(rows // tr,),
        in_specs=[_row_spec(tr, n), _row_spec(tr, n), _gain_spec(n), _row_spec(tr, n)],
        out_specs=(_gain_spec(LANES), _row_spec(tr, n), _row_spec(tr, n), _gain_spec(n)),
        compiler_params=_params("arbitrary"),
    )(h1, dn, g_post, target)


def _blk3_spec(tr, n):
    return pl.BlockSpec((None, tr, n), lambda b, i: (b, i, 0))


def _swiglu_fwd_tile(up, gate):
    return up, gate * jax.nn.sigmoid(gate) * up


def _swiglu_bwd_tile(dact, gate, up):
    sig = jax.nn.sigmoid(gate)
    return dact * up * (sig * (1.0 + gate * (1.0 - sig))), dact * (gate * sig)


def _rope(x, cos, sin):
    half = MLA_ROPE // 2
    x1, x2 = x[:, :half], x[:, half:]
    return jnp.concatenate([x1 * cos - x2 * sin, x2 * cos + x1 * sin], axis=-1)


def _rope_t(d, cos, sin):
    half = MLA_ROPE // 2
    d1, d2 = d[:, :half], d[:, half:]
    return jnp.concatenate([d1 * cos + d2 * sin, d2 * cos - d1 * sin], axis=-1)


def _mla_prep(q4, kv4, k_rope, cos, sin):
    nb, rows, _ = q4.shape
    tr = _fit(rows, ROW_TILE, 16)
    hpc, half = HEADS_PER_CHIP, MLA_ROPE // 2

    def body(q_ref, kv_ref, kr_ref, c_ref, s_ref, qo_ref, ko_ref, vo_ref):
        cos, sin = c_ref[...], s_ref[...]
        k_pe = _rope(kr_ref[...], cos, sin)
        q_all, kv_all = q_ref[...], kv_ref[...]
        for h in range(hpc):
            q = q_all[:, h * MLA_QK:(h + 1) * MLA_QK]
            qo_ref[h] = jnp.concatenate([q[:, :MLA_NOPE], _rope(q[:, MLA_NOPE:], cos, sin)], axis=-1).astype(BF16)
            kv = kv_all[:, h * (MLA_NOPE + MLA_V):(h + 1) * (MLA_NOPE + MLA_V)]
            ko_ref[h] = jnp.concatenate([kv[:, :MLA_NOPE], k_pe], axis=-1).astype(BF16)
            vo_ref[h] = kv[:, MLA_NOPE:].astype(BF16)

    def head_spec(n):
        return pl.BlockSpec((hpc, tr, n), lambda b, i: (b, i, 0))

    def row_spec(n):
        return pl.BlockSpec((tr, n), lambda b, i: (i, 0))

    return pl.pallas_call(
        body, name="mla_prep",
        out_shape=(jax.ShapeDtypeStruct((MLA_HEADS, rows, MLA_QK), BF16), jax.ShapeDtypeStruct((MLA_HEADS, rows, MLA_QK), BF16),
                   jax.ShapeDtypeStruct((MLA_HEADS, rows, MLA_V), BF16)),
        grid=(nb, rows // tr),
        in_specs=[_blk3_spec(tr, hpc * MLA_QK), _blk3_spec(tr, hpc * (MLA_NOPE + MLA_V)), row_spec(MLA_ROPE),
                  row_spec(half), row_spec(half)],
        out_specs=(head_spec(MLA_QK), head_spec(MLA_QK), head_spec(MLA_V)),
        compiler_params=_params("parallel", "parallel"),
    )(q4, kv4, k_rope, cos, sin)


def _mla_unprep(dq, dk, dv, cos, sin):
    _, rows, _ = dq.shape
    tr = _fit(rows, ROW_TILE, 16)
    hpc, half = HEADS_PER_CHIP, MLA_ROPE // 2

    def body(dq_ref, dk_ref, dv_ref, c_ref, s_ref, q4_ref, kv4_ref, kr_ref):
        cos, sin = c_ref[...], s_ref[...]
        d_pe = jnp.zeros((tr, MLA_ROPE), F32)
        q_parts, kv_parts = [], []
        for h in range(hpc):
            g, gk = dq_ref[h], dk_ref[h]
            q_parts += [g[:, :MLA_NOPE], _rope_t(g[:, MLA_NOPE:], cos, sin)]
            kv_parts += [gk[:, :MLA_NOPE], dv_ref[h]]
            d_pe = d_pe + gk[:, MLA_NOPE:]
        q4_ref[...] = jnp.concatenate(q_parts, axis=-1).astype(BF16)
        kv4_ref[...] = jnp.concatenate(kv_parts, axis=-1).astype(BF16)

        @pl.when(pl.program_id(1) == 0)
        def _():
            kr_ref[...] = jnp.zeros_like(kr_ref)

        kr_ref[...] += _rope_t(d_pe, cos, sin)

    def head_spec(n):
        return pl.BlockSpec((hpc, tr, n), lambda i, b: (b, i, 0))

    def row_spec(n):
        return pl.BlockSpec((tr, n), lambda i, b: (i, 0))

    def blk_spec(n):
        return pl.BlockSpec((None, tr, n), lambda i, b: (b, i, 0))

    return pl.pallas_call(
        body, name="mla_unprep",
        out_shape=(jax.ShapeDtypeStruct((N_CHIPS, rows, hpc * MLA_QK), BF16),
                   jax.ShapeDtypeStruct((N_CHIPS, rows, hpc * (MLA_NOPE + MLA_V)), BF16),
                   jax.ShapeDtypeStruct((rows, MLA_ROPE), F32)),
        grid=(rows // tr, N_CHIPS),
        in_specs=[head_spec(MLA_QK), head_spec(MLA_QK), head_spec(MLA_V), row_spec(half), row_spec(half)],
        out_specs=(blk_spec(hpc * MLA_QK), blk_spec(hpc * (MLA_NOPE + MLA_V)), row_spec(MLA_ROPE)),
        compiler_params=_params("parallel", "arbitrary"),
    )(dq, dk, dv, cos, sin)


def _carry(body, comm, n_in, n_out, grid):
    n_ci, n_co = (len(comm.ins), len(comm.out_shapes)) if comm else (0, 0)

    def full(*refs):
        ins, c_in = refs[:n_in], refs[n_in:n_in + n_ci]
        outs = refs[n_in + n_ci:n_in + n_ci + n_out]
        c_out = refs[n_in + n_ci + n_out:n_in + n_ci + n_out + n_co]
        sems = refs[n_in + n_ci + n_out + n_co:]
        first, last = _grid_ends(grid)
        if comm:
            pl.when(first)(lambda: comm.start(c_in, c_out, sems))
        body(ins, outs)
        if comm:
            pl.when(last)(lambda: comm.finish(c_in, c_out, sems))

    extra = dict(
        operands=list(comm.ins) if comm else [], in_specs=[ANY] * n_ci, out_specs=(ANY,) * n_co,
        out_shape=tuple(comm.out_shapes) if comm else (),
        aliases={n_in + i: n_out + o for i, o in comm.aliases.items()} if comm else {},
        scratch=comm.scratch if comm else [])
    return full, extra


def _causal_keep(t):
    return lax.broadcasted_iota(jnp.int32, (t, t), 1) <= lax.broadcasted_iota(jnp.int32, (t, t), 0)


def _mla_fwd(q, k, v, comm=None):
    nh, rows, _ = q.shape
    t = _fit(rows, ATTN_TILE)
    scale = MLA_QK ** -0.5
    grid = (nh, rows // t)

    def body(ins, outs):
        (q_ref, k_ref, v_ref), (o_ref, lse_ref) = ins, outs
        i = pl.program_id(1)
        qb = q_ref[...]

        def step(j, carry, diagonal):
            m, l, acc = carry
            rows_j = pl.ds(pl.multiple_of(j * t, t), t)
            s = _dot(qb, k_ref[rows_j, :], 1, 1) * scale
            if diagonal:
                s = jnp.where(_causal_keep(t), s, MASK_VALUE)
            m_new = jnp.maximum(m, jnp.max(s, axis=-1, keepdims=True))
            alpha = jnp.exp(m - m_new)
            p = jnp.exp(s - m_new)
            l = alpha * l + jnp.sum(p, axis=-1, keepdims=True)
            acc = alpha * acc + _dot(p.astype(BF16), v_ref[rows_j, :], 1, 0)
            return m_new, l, acc

        init = (jnp.full((t, 1), MASK_VALUE, F32), jnp.zeros((t, 1), F32), jnp.zeros((t, MLA_V), F32))
        carry = lax.fori_loop(0, i, lambda j, c: step(j, c, False), init)
        m, l, acc = step(i, carry, True)
        o_ref[...] = acc / l
        lse_ref[...] = m + jnp.log(l)

    full, extra = _carry(body, comm, 3, 2, grid)
    res = pl.pallas_call(
        full, name="mla_fwd",
        out_shape=(jax.ShapeDtypeStruct((rows, nh * MLA_V), F32), jax.ShapeDtypeStruct((nh, rows, 1), F32)) + extra["out_shape"],
        grid=grid,
        in_specs=[pl.BlockSpec((None, t, MLA_QK), lambda h, i: (h, i, 0)),
                  pl.BlockSpec((None, rows, MLA_QK), lambda h, i: (h, 0, 0)),
                  pl.BlockSpec((None, rows, MLA_V), lambda h, i: (h, 0, 0))] + extra["in_specs"],
        out_specs=(pl.BlockSpec((t, MLA_V), lambda h, i: (i, h)), pl.BlockSpec((None, t, 1), lambda h, i: (h, i, 0))) + extra["out_specs"],
        input_output_aliases=extra["aliases"], scratch_shapes=extra["scratch"],
        compiler_params=_params("arbitrary", "arbitrary"),
    )(q, k, v, *extra["operands"])
    return res[0], res[1], res[2:]


def _mla_bwd(q, k, v, do, lse, delta, comm=None):
    nh, rows, _ = q.shape
    t = _fit(rows, ATTN_TILE)
    n_t = rows // t
    scale = MLA_QK ** -0.5
    grid = (nh, n_t)

    def body(ins, outs):
        (q_ref, k_ref, v_ref, do_ref, lse_ref, dl_ref), (dq_ref, dk_ref, dv_ref) = ins, outs
        j = pl.program_id(1)
        kb, vb = k_ref[...], v_ref[...]

        @pl.when(j == 0)
        def _():
            dq_ref[...] = jnp.zeros_like(dq_ref)

        def step(i, carry, diagonal):
            dk, dv = carry
            rows_i = pl.ds(pl.multiple_of(i * t, t), t)
            qb, dob = q_ref[rows_i, :], do_ref[rows_i, :]
            s = _dot(qb, kb, 1, 1) * scale
            if diagonal:
                s = jnp.where(_causal_keep(t), s, MASK_VALUE)
            p = jnp.exp(s - lse_ref[rows_i, :])
            dv = dv + _dot(p.astype(BF16), dob, 0, 0)
            dp = _dot(dob, vb, 1, 1)
            ds = (p * (dp - dl_ref[rows_i, :]) * scale).astype(BF16)
            dk = dk + _dot(ds, qb, 0, 0)
            dq_ref[rows_i, :] += _dot(ds, kb, 1, 0)
            return dk, dv

        carry = step(j, (jnp.zeros((t, MLA_QK), F32), jnp.zeros((t, MLA_V), F32)), True)
        dk, dv = lax.fori_loop(j + 1, n_t, lambda i, c: step(i, c, False), carry)
        dk_ref[...] = dk
        dv_ref[...] = dv

    def head(n):
        return pl.BlockSpec((None, rows, n), lambda h, j: (h, 0, 0))

    def tile(n):
        return pl.BlockSpec((None, t, n), lambda h, j: (h, j, 0))

    full, extra = _carry(body, comm, 6, 3, grid)
    res = pl.pallas_call(
        full, name="mla_bwd",
        out_shape=(jax.ShapeDtypeStruct((nh, rows, MLA_QK), F32), jax.ShapeDtypeStruct((nh, rows, MLA_QK), F32),
                   jax.ShapeDtypeStruct((nh, rows, MLA_V), F32)) + extra["out_shape"],
        grid=grid,
        in_specs=[head(MLA_QK), tile(MLA_QK), tile(MLA_V), pl.BlockSpec((rows, MLA_V), lambda h, j: (0, h)), head(1), head(1)]
        + extra["in_specs"],
        out_specs=(head(MLA_QK), tile(MLA_QK), tile(MLA_V)) + extra["out_specs"],
        input_output_aliases=extra["aliases"], scratch_shapes=extra["scratch"],
        compiler_params=_params("arbitrary", "arbitrary"),
    )(q, k, v, do, lse, delta, *extra["operands"])
    return res[0], res[1], res[2], res[3:]


def _swa_slope(h):
    return 2.0 ** (-8.0 * (h + 1) / SWA_Q_HEADS)


def _swa_band_specs(rows):
    w = SWA_WINDOW
    kvw = SWA_KV_HEADS * SWA_HEAD_DIM

    def prev(i):
        return jnp.maximum(i - 1, 0)

    return [pl.BlockSpec((w, kvw), lambda i: (prev(i), 0)), pl.BlockSpec((w, kvw), lambda i: (i, 0)),
            pl.BlockSpec((w, kvw), lambda i: (prev(i), 0)), pl.BlockSpec((w, kvw), lambda i: (i, 0)),
            pl.BlockSpec((w, 1), lambda i: (i, 0)),
            pl.BlockSpec((1, w), lambda i: (0, prev(i))), pl.BlockSpec((1, w), lambda i: (0, i)),
            pl.BlockSpec(memory_space=pltpu.SMEM)]


def _swa_bias(i, pc_ref, pp_ref, pn_ref):
    w = SWA_WINDOW
    k_pos = jnp.concatenate([pp_ref[...], pn_ref[...]], axis=1)
    dist = jnp.abs(pc_ref[...] - k_pos).astype(F32)
    r = lax.broadcasted_iota(jnp.int32, (w, 2 * w), 0)
    col = lax.broadcasted_iota(jnp.int32, (w, 2 * w), 1)
    delta = r + w - col
    valid = (delta >= 0) & (delta < w) & ((col >= w) | (i > 0))
    return dist, valid


def _swa_fwd(q, k, v, pos_col, pos_row, sinks):
    rows = q.shape[0]
    w, hd = SWA_WINDOW, SWA_HEAD_DIM
    scale = hd ** -0.5

    def body(q_ref, kp_ref, kc_ref, vp_ref, vc_ref, pc_ref, pp_ref, pn_ref, sink_ref, o_ref, lse_ref):
        dist, valid = _swa_bias(pl.program_id(0), pc_ref, pp_ref, pn_ref)
        for kvh in range(SWA_KV_HEADS):
            cols = slice(kvh * hd, (kvh + 1) * hd)
            kb = jnp.concatenate([kp_ref[:, cols], kc_ref[:, cols]], axis=0)
            vb = jnp.concatenate([vp_ref[:, cols], vc_ref[:, cols]], axis=0)
            for g in range(SWA_GROUP):
                h = kvh * SWA_GROUP + g
                sink = sink_ref[h]
                s = _dot(q_ref[:, h * hd:(h + 1) * hd], kb, 1, 1) * scale - _swa_slope(h) * dist
                s = jnp.where(valid, s, MASK_VALUE)
                m = jnp.maximum(jnp.max(s, axis=-1, keepdims=True), sink)
                e = jnp.exp(s - m)
                den = jnp.sum(e, axis=-1, keepdims=True) + jnp.exp(sink - m)
                o_ref[:, h * hd:(h + 1) * hd] = _dot((e / den).astype(BF16), vb, 1, 0)
                lse_ref[:, h:h + 1] = m + jnp.log(den)

    return pl.pallas_call(
        body, name="swa_fwd",
        out_shape=(jax.ShapeDtypeStruct((rows, MIX_B), F32), jax.ShapeDtypeStruct((rows, SWA_Q_HEADS), F32)),
        grid=(rows // w,),
        in_specs=[pl.BlockSpec((w, MIX_B), lambda i: (i, 0))] + _swa_band_specs(rows),
        out_specs=(pl.BlockSpec((w, MIX_B), lambda i: (i, 0)), pl.BlockSpec((w, SWA_Q_HEADS), lambda i: (i, 0))),
        compiler_params=_params("parallel"),
    )(q, k, k, v, v, pos_col, pos_row, pos_row, sinks)


def _swa_bwd(q, k, v, pos_col, pos_row, sinks, do, lse, delta):
    rows = q.shape[0]
    w, hd = SWA_WINDOW, SWA_HEAD_DIM
    kvw = SWA_KV_HEADS * hd
    scale = hd ** -0.5

    def body(q_ref, kp_ref, kc_ref, vp_ref, vc_ref, pc_ref, pp_ref, pn_ref, sink_ref, do_ref, lse_ref, dl_ref,
             dq_ref, dkc_ref, dkp_ref, dvc_ref, dvp_ref, dsink_ref):
        dist, valid = _swa_bias(pl.program_id(0), pc_ref, pp_ref, pn_ref)

        @pl.when(pl.program_id(0) == 0)
        def _():
            dsink_ref[...] = jnp.zeros_like(dsink_ref)

        for kvh in range(SWA_KV_HEADS):
            cols = slice(kvh * hd, (kvh + 1) * hd)
            kb = jnp.concatenate([kp_ref[:, cols], kc_ref[:, cols]], axis=0)
            vb = jnp.concatenate([vp_ref[:, cols], vc_ref[:, cols]], axis=0)
            dk = jnp.zeros((2 * w, hd), F32)
            dv = jnp.zeros((2 * w, hd), F32)
            for g in range(SWA_GROUP):
                h = kvh * SWA_GROUP + g
                hc = slice(h * hd, (h + 1) * hd)
                qb, dob = q_ref[:, hc], do_ref[:, hc]
                lse_h, dl_h = lse_ref[:, h:h + 1], dl_ref[:, h:h + 1]
                s = _dot(qb, kb, 1, 1) * scale - _swa_slope(h) * dist
                s = jnp.where(valid, s, MASK_VALUE)
                p = jnp.exp(s - lse_h)
                ds = (p * (_dot(dob, vb, 1, 1) - dl_h) * scale).astype(BF16)
                dq_ref[:, hc] = _dot(ds, kb, 1, 0).astype(BF16)
                dk = dk + _dot(ds, qb, 0, 0)
                dv = dv + _dot(p.astype(BF16), dob, 0, 0)
                dsink_ref[:, h:h + 1] += -jnp.sum(jnp.exp(sink_ref[h] - lse_h) * dl_h, axis=0, keepdims=True)
            dkp_ref[:, cols] = dk[:w]
            dkc_ref[:, cols] = dk[w:]
            dvp_ref[:, cols] = dv[:w]
            dvc_ref[:, cols] = dv[w:]

    def blk(n):
        return pl.BlockSpec((w, n), lambda i: (i, 0))

    return pl.pallas_call(
        body, name="swa_bwd",
        out_shape=(jax.ShapeDtypeStruct((rows, MIX_B), BF16),) + (jax.ShapeDtypeStruct((rows, kvw), F32),) * 4
        + (jax.ShapeDtypeStruct((1, SWA_Q_HEADS), F32),),
        grid=(rows // w,),
        in_specs=[blk(MIX_B)] + _swa_band_specs(rows) + [blk(MIX_B), blk(SWA_Q_HEADS), blk(SWA_Q_HEADS)],
        out_specs=(blk(MIX_B), blk(kvw), blk(kvw), blk(kvw), blk(kvw), pl.BlockSpec((1, SWA_Q_HEADS), lambda i: (0, 0))),
        compiler_params=_params("arbitrary"),
    )(q, k, k, v, v, pos_col, pos_row, pos_row, sinks, do, lse, delta)


def _band_merge(cur, prev, name):
    rows, n = cur.shape
    w = SWA_WINDOW
    last = rows // w - 1

    def body(c_ref, p_ref, o_ref):
        nxt = jnp.where(pl.program_id(0) < last, p_ref[...], 0.0)
        o_ref[...] = (c_ref[...] + nxt).astype(BF16)

    return pl.pallas_call(
        body, name=name, out_shape=jax.ShapeDtypeStruct((rows, n), BF16), grid=(rows // w,),
        in_specs=[pl.BlockSpec((w, n), lambda j: (j, 0)), pl.BlockSpec((w, n), lambda j: (jnp.minimum(j + 1, last), 0))],
        out_specs=pl.BlockSpec((w, n), lambda j: (j, 0)), compiler_params=_params("parallel"),
    )(cur, prev)


def _adamw(w, g, m, v, name, emit_grad=False):
    rows, cols = w.shape
    tr, tc = _fit(rows, 352, 8), _fit(cols, MM_TILE)
    c1 = 1.0 - ADAM_B1 ** ADAM_STEP
    c2 = 1.0 - ADAM_B2 ** ADAM_STEP
    n_out = 4 if emit_grad else 3

    def body(w_ref, g_ref, m_ref, v_ref, d_ref, mo_ref, vo_ref, *go_ref):
        gr = g_ref[...]
        m_new = ADAM_B1 * m_ref[...] + (1.0 - ADAM_B1) * gr
        v_new = ADAM_B2 * v_ref[...] + (1.0 - ADAM_B2) * jnp.square(gr)
        mo_ref[...] = m_new
        vo_ref[...] = v_new
        d_ref[...] = -ADAM_LR * ((m_new / c1) / (jnp.sqrt(v_new / c2) + ADAM_EPS) + ADAM_WD * w_ref[...])
        if emit_grad:
            go_ref[0][...] = gr

    spec = pl.BlockSpec((tr, tc), lambda i, j: (i, j))
    return pl.pallas_call(
        body, name=name, out_shape=(jax.ShapeDtypeStruct(w.shape, F32),) * n_out, grid=(rows // tr, cols // tc),
        in_specs=[spec] * 4, out_specs=(spec,) * n_out, compiler_params=_params("parallel", "parallel"),
    )(w, g, m, v)


OTHER_CHIPS = ((1, 0), (0, 1), (1, 1))


def _place():
    x, y, c = lax.axis_index("x"), lax.axis_index("y"), lax.axis_index("c")
    return x, y, c


def _flip(v, f):
    return 1 - v if f else v


class _Comm:
    def __init__(self, ins, out_shapes, aliases, sem_sizes, start, finish):
        self.ins, self.out_shapes, self.aliases, self.sem_sizes = list(ins), list(out_shapes), dict(aliases), list(sem_sizes)
        self.start, self.finish = start, finish

    @property
    def scratch(self):
        return [pltpu.SemaphoreType.DMA((n,)) for n in self.sem_sizes]


def _run_comm(comm, name):
    n_in, n_out = len(comm.ins), len(comm.out_shapes)

    def body(*refs):
        ins, outs, sems = refs[:n_in], refs[n_in:n_in + n_out], refs[n_in + n_out:]
        comm.start(ins, outs, sems)
        comm.finish(ins, outs, sems)

    return pl.pallas_call(
        body, name=name, out_shape=tuple(comm.out_shapes), in_specs=[ANY] * n_in, out_specs=(ANY,) * n_out,
        input_output_aliases=comm.aliases, scratch_shapes=comm.scratch,
    )(*comm.ins)


def _grid_ends(grid):
    first = last = None
    for axis, n in enumerate(grid):
        pid = pl.program_id(axis)
        first = (pid == 0) if first is None else first & (pid == 0)
        last = (pid == n - 1) if last is None else last & (pid == n - 1)
    return first, last


def _cast_into_slot(w, name):
    rows, cols = w.shape
    tr, tc = _fit(rows, 704, 16), _fit(cols, MM_TILE)

    def body(w_ref, o_ref):
        o_ref[...] = w_ref[...].astype(BF16)

    return pl.pallas_call(
        body, name=name, out_shape=jax.ShapeDtypeStruct((N_CHIPS, rows, cols), BF16), grid=(rows // tr, cols // tc),
        in_specs=[pl.BlockSpec((tr, tc), lambda i, j: (i, j))],
        out_specs=pl.BlockSpec((None, tr, tc), lambda i, j: (2 * lax.axis_index("x") + lax.axis_index("y"), i, j)),
        compiler_params=_params("parallel", "parallel"),
    )(w)


def _gather_comm(slots):
    n = len(slots)
    pairs = [(a, j) for a in range(n) for j in range(3)]

    def copies(src, dst, sems):
        send, recv, fsend, frecv = sems
        x, y, c = _place()
        me = 2 * x + y

        def rows(a, core):
            half = slots[a].shape[1] // 2
            return pl.ds(core * half, half)

        def chip(j):
            return _flip(x, OTHER_CHIPS[j][0]), _flip(y, OTHER_CHIPS[j][1])

        def out(a, j):
            px, py = chip(j)
            return pltpu.make_async_remote_copy(
                src_ref=src[a].at[me, rows(a, c)], dst_ref=dst[a].at[me, rows(a, c)], send_sem=send.at[a * 3 + j],
                recv_sem=recv.at[a * 3 + j], device_id=(px, py, c), device_id_type=MESH)

        def landed(a, j):
            px, py = chip(j)
            blk = 2 * px + py
            return pltpu.make_async_remote_copy(
                src_ref=dst[a].at[blk, rows(a, c)], dst_ref=dst[a].at[blk, rows(a, c)], send_sem=send.at[a * 3 + j],
                recv_sem=recv.at[a * 3 + j], device_id=(x, y, c), device_id_type=MESH)

        def passed(a, j, core):
            px, py = chip(j)
            blk = 2 * px + py
            return pltpu.make_async_remote_copy(
                src_ref=dst[a].at[blk, rows(a, core)], dst_ref=dst[a].at[blk, rows(a, core)], send_sem=fsend.at[a * 3 + j],
                recv_sem=frecv.at[a * 3 + j], device_id=(x, y, 1 - c), device_id_type=MESH)

        return c, out, landed, passed

    def start(src, dst, sems):
        _, out, _, _ = copies(src, dst, sems)
        for a, j in pairs:
            out(a, j).start()

    def finish(src, dst, sems):
        c, out, landed, passed = copies(src, dst, sems)
        for a, j in pairs:
            landed(a, j).wait_recv()
            passed(a, j, c).start()
        for a, j in pairs:
            passed(a, j, 1 - c).wait_recv()
        for a, j in pairs:
            out(a, j).wait_send()
            passed(a, j, c).wait_send()

    shapes = [jax.ShapeDtypeStruct(s.shape, s.dtype) for s in slots]
    return _Comm(slots, shapes, {a: a for a in range(n)}, [3 * n] * 4, start, finish)


def _pair_exchange_comm(grads):
    n = len(grads)

    def copy(src, dst, sems, a):
        x, y, c = _place()
        half = grads[a].shape[1] // 2
        return pltpu.make_async_remote_copy(
            src_ref=src[a].at[:, pl.ds((1 - c) * half, half)], dst_ref=dst[a], send_sem=sems[0].at[a],
            recv_sem=sems[1].at[a], device_id=(x, y, 1 - c), device_id_type=MESH)

    def start(src, dst, sems):
        for a in range(n):
            copy(src, dst, sems, a).start()

    def finish(src, dst, sems):
        for a in range(n):
            copy(src, dst, sems, a).wait()

    shapes = [jax.ShapeDtypeStruct((g.shape[0], g.shape[1] // 2, g.shape[2]), g.dtype) for g in grads]
    return _Comm(grads, shapes, {}, [n, n], start, finish)


def _pair_add(mine, theirs, name):
    nb, rows, cols = theirs.shape
    tr, tc = _fit(rows, 704, 16), _fit(cols, MM_TILE)
    n_r = rows // tr

    def body(a_ref, b_ref, lo_ref, own_ref):
        s = a_ref[...] + b_ref[...]
        lo_ref[...] = s.astype(BF16)

        @pl.when(pl.program_id(2) == 2 * lax.axis_index("x") + lax.axis_index("y"))
        def _():
            own_ref[...] = s

    return pl.pallas_call(
        body, name=name, grid=(n_r, cols // tc, nb),
        in_specs=[pl.BlockSpec((None, tr, tc), lambda i, j, b: (b, lax.axis_index("c") * n_r + i, j)),
                  pl.BlockSpec((None, tr, tc), lambda i, j, b: (b, i, j))],
        out_specs=(pl.BlockSpec((None, tr, tc), lambda i, j, b: (b, i, j)), pl.BlockSpec((tr, tc), lambda i, j, b: (i, j))),
        out_shape=(jax.ShapeDtypeStruct(theirs.shape, BF16), jax.ShapeDtypeStruct((rows, cols), F32)),
        compiler_params=_params("parallel", "parallel", "arbitrary"),
    )(mine, theirs)


def _chip_exchange_comm(sums):
    n = len(sums)
    pairs = [(a, j) for a in range(n) for j in range(3)]

    def copy(src, dst, sems, a, j):
        x, y, c = _place()
        px, py = _flip(x, OTHER_CHIPS[j][0]), _flip(y, OTHER_CHIPS[j][1])
        return pltpu.make_async_remote_copy(
            src_ref=src[a].at[2 * px + py], dst_ref=dst[a].at[j], send_sem=sems[0].at[a * 3 + j],
            recv_sem=sems[1].at[a * 3 + j], device_id=(px, py, c), device_id_type=MESH)

    def start(src, dst, sems):
        for a, j in pairs:
            copy(src, dst, sems, a, j).start()

    def finish(src, dst, sems):
        for a, j in pairs:
            copy(src, dst, sems, a, j).wait()

    shapes = [jax.ShapeDtypeStruct((3,) + s.shape[1:], s.dtype) for s in sums]
    return _Comm(sums, shapes, {}, [3 * n, 3 * n], start, finish)


def _chip_add(own, got, name):
    rows, cols = own.shape
    tr, tc = _fit(rows, 704, 16), _fit(cols, MM_TILE)
    n_r = rows // tr

    def body(o_ref, g_ref, out_ref):
        out_ref[...] = ((o_ref[...] + g_ref[0].astype(F32)) + g_ref[1].astype(F32)) + g_ref[2].astype(F32)

    return pl.pallas_call(
        body, name=name, out_shape=jax.ShapeDtypeStruct((2 * rows, cols), F32), grid=(n_r, cols // tc),
        in_specs=[pl.BlockSpec((tr, tc), lambda i, j: (i, j)), pl.BlockSpec((3, tr, tc), lambda i, j: (0, i, j))],
        out_specs=pl.BlockSpec((tr, tc), lambda i, j: (lax.axis_index("c") * n_r + i, j)),
        compiler_params=_params("parallel", "parallel"),
    )(own, got)


def _pair_share_comm(grads):
    n = len(grads)

    def copy(src, dst, sems, a, mine):
        x, y, c = _place()
        half = grads[a].shape[0] // 2
        rows = pl.ds((c if mine else 1 - c) * half, half)
        return pltpu.make_async_remote_copy(
            src_ref=src[a].at[rows], dst_ref=dst[a].at[rows], send_sem=sems[0].at[a], recv_sem=sems[1].at[a],
            device_id=(x, y, 1 - c), device_id_type=MESH)

    def start(src, dst, sems):
        for a in range(n):
            copy(src, dst, sems, a, True).start()

    def finish(src, dst, sems):
        for a in range(n):
            copy(src, dst, sems, a, False).wait_recv()
            copy(src, dst, sems, a, True).wait_send()

    shapes = [jax.ShapeDtypeStruct(g.shape, g.dtype) for g in grads]
    return _Comm(grads, shapes, {a: a for a in range(n)}, [n, n], start, finish)


def _all_sum_small(vec):
    r, n = vec.shape
    flips = [(a, b, d) for a in (0, 1) for b in (0, 1) for d in (0, 1)][1:]

    def body(v_ref, o_ref, buf, send, recv):
        x, y, c = _place()
        me = 4 * x + 2 * y + c
        cps = []
        for k, (fx, fy, fc) in enumerate(flips):
            cp = pltpu.make_async_remote_copy(
                src_ref=v_ref, dst_ref=buf.at[me], send_sem=send.at[k], recv_sem=recv.at[k],
                device_id=(_flip(x, fx), _flip(y, fy), _flip(c, fc)), device_id_type=MESH)
            cp.start()
            cps.append(cp)
        buf[me] = v_ref[...]
        for k, (fx, fy, fc) in enumerate(flips):
            peer = 4 * _flip(x, fx) + 2 * _flip(y, fy) + _flip(c, fc)
            pltpu.make_async_remote_copy(
                src_ref=v_ref, dst_ref=buf.at[peer], send_sem=send.at[k], recv_sem=recv.at[k],
                device_id=(x, y, c), device_id_type=MESH).wait_recv()
        for cp in cps:
            cp.wait_send()
        acc = buf[0]
        for d in range(1, 8):
            acc = acc + buf[d]
        o_ref[...] = acc

    return pl.pallas_call(
        body, name="all_sum_small", out_shape=jax.ShapeDtypeStruct((r, n), F32),
        in_specs=[pl.BlockSpec(memory_space=pltpu.VMEM)], out_specs=pl.BlockSpec(memory_space=pltpu.VMEM),
        scratch_shapes=[pltpu.VMEM((8, r, n), F32), pltpu.SemaphoreType.DMA((7,)), pltpu.SemaphoreType.DMA((7,))],
    )(vec)


SMALL = ("attn_pre_g", "q_norm_g", "kv_norm_g", "swa_sinks", "grp_a_g", "grp_b_g", "attn_post_g", "ffn_pre_g", "ffn_post_g")
BIG = ("w_in", "w_uq", "w_ukv", "w_o", "w_gate", "w_up", "w_down")
ORDER = ("attn_pre_g", "w_in", "q_norm_g", "w_uq", "kv_norm_g", "w_ukv", "swa_sinks", "grp_a_g", "grp_b_g", "w_o",
         "attn_post_g", "ffn_pre_g", "w_gate", "w_up", "w_down", "ffn_post_g")


def _pad_lanes(v):
    n = v.shape[1]
    return jnp.pad(v, ((0, 0), (0, -n % LANES)))


def kernel(x, positions, attn_pre_g, w_in, q_norm_g, w_uq, kv_norm_g, w_ukv, swa_sinks, grp_a_g, grp_b_g, w_o, attn_post_g, ffn_pre_g, w_gate, w_up, w_down, ffn_post_g, loss_target, m_attn_pre_g, m_w_in, m_q_norm_g, m_w_uq, m_kv_norm_g, m_w_ukv, m_swa_sinks, m_grp_a_g, m_grp_b_g, m_w_o, m_attn_post_g, m_ffn_pre_g, m_w_gate, m_w_up, m_w_down, m_ffn_post_g, v_attn_pre_g, v_w_in, v_q_norm_g, v_w_uq, v_kv_norm_g, v_w_ukv, v_swa_sinks, v_grp_a_g, v_grp_b_g, v_w_o, v_attn_post_g, v_ffn_pre_g, v_w_gate, v_w_up, v_w_down, v_ffn_post_g):
    given = dict(locals())
    w32 = {k: given[k][0] for k in BIG}
    gains = {k: given[k] for k in SMALL}
    xs, tgt = x[0], loss_target[0]
    seq, d_model = xs.shape
    q_rank, kv_rank = q_norm_g.shape[1], kv_norm_g.shape[1]
    kvw = SWA_KV_HEADS * SWA_HEAD_DIM

    slot = {k: _cast_into_slot(w32[k], "cast_" + k) for k in BIG}
    full = {}
    (full["w_in"],) = _run_comm(_gather_comm([slot["w_in"]]), "gather_w_in")

    pos = positions[0]
    inv = 1.0 / (ROPE_THETA ** (jnp.arange(0, MLA_ROPE, 2, dtype=F32) / MLA_ROPE))
    ang = pos.astype(F32)[:, None] * inv
    cos, sin = jnp.cos(ang), jnp.sin(ang)
    pos_col, pos_row = pos[:, None], pos[None, :]
    sinks = swa_sinks[0]

    a = _norm_fwd(xs, attn_pre_g, BF16, "attn_pre_norm")
    proj4, (full["w_uq"], full["w_ukv"], full["w_o"]) = _matmul(
        a, full["w_in"], name="in_proj", comm=_gather_comm([slot["w_uq"], slot["w_ukv"], slot["w_o"]]))
    proj = jnp.concatenate([proj4[b] for b in range(N_CHIPS)], axis=1)
    cuts = (0, q_rank, q_rank + kv_rank, q_rank + kv_rank + MLA_ROPE)
    cuts = cuts + (cuts[3] + MIX_B, cuts[3] + MIX_B + kvw, cuts[3] + MIX_B + 2 * kvw)
    c_q, c_kv, k_rope, q_s, k_s, v_s = (proj[:, lo:hi] for lo, hi in zip(cuts[:-1], cuts[1:]))
    cqn = _norm_fwd(c_q, q_norm_g, BF16, "q_norm")
    ckvn = _norm_fwd(c_kv, kv_norm_g, BF16, "kv_norm")
    q4 = _matmul(cqn, full["w_uq"], name="q_up")
    kv4 = _matmul(ckvn, full["w_ukv"], name="kv_up")
    qh, kh, vh = _mla_prep(q4, kv4, k_rope, cos, sin)
    o_a, lse_a, (full["w_gate"],) = _mla_fwd(qh, kh, vh, comm=_gather_comm([slot["w_gate"]]))
    q_sb, k_sb, v_sb = q_s.astype(BF16), k_s.astype(BF16), v_s.astype(BF16)
    o_b, lse_b = _swa_fwd(q_sb, k_sb, v_sb, pos_col, pos_row, sinks)
    mix = _mix_fwd(o_a, o_b, grp_a_g, grp_b_g)
    w_o_full = full["w_o"].reshape(N_CHIPS * full["w_o"].shape[1], d_model)
    ao = _matmul(mix, w_o_full, name="out_proj")
    h1, f = _post_pre_fwd(xs, ao, attn_post_g, ffn_pre_g)
    gate, (full["w_up"],) = _matmul(f, full["w_gate"], name="ffn_gate", comm=_gather_comm([slot["w_up"]]))
    (up, act), (full["w_down"],) = _matmul(
        f, full["w_up"], name="ffn_up", comm=_gather_comm([slot["w_down"]]), epilogue=_swiglu_fwd_tile, extras=(gate,),
        out_dtype=(F32, BF16), tm=512)
    dn = _matmul(act, full["w_down"], reduce_b=True, name="ffn_down")
    loss_row, dy, ddn, d_ffn_post = _loss_bwd(h1, dn, ffn_post_g, tgt)

    dgate, dup = _matmul(ddn, full["w_down"], tb=True, name="ffn_down_dx", epilogue=_swiglu_bwd_tile, extras=(gate, up),
                         out_dtype=(BF16, BF16), tm=512, tk=512)
    dw_down = _matmul(act, ddn, ta=True, name="ffn_down_dw", tn=512)
    dw_gate = _matmul(f, dgate, ta=True, name="ffn_gate_dw", tm=512)
    dw_up = _matmul(f, dup, ta=True, name="ffn_up_dw", tm=512)
    ffn = ("w_down", "w_gate", "w_up")
    grads4 = dict(w_down=dw_down, w_gate=dw_gate, w_up=dw_up)
    df_g, theirs = _matmul(dgate, full["w_gate"], tb=True, reduce_b=True, name="ffn_gate_dx",
                           comm=_pair_exchange_comm([grads4[k] for k in ffn]))
    low, own, got = {}, {}, {}
    for k, t in zip(ffn, theirs):
        low[k], own[k] = _pair_add(grads4[k], t, "pair_add_" + k)
    df_u, (got["w_down"],) = _matmul(dup, full["w_up"], tb=True, reduce_b=True, name="ffn_up_dx",
                                     comm=_chip_exchange_comm([low["w_down"]]))
    dh1, d_ffn_pre = _norm_bwd(h1, ffn_pre_g, [df_g, df_u], [dy], F32, "ffn_pre_norm_bwd")
    dao, d_attn_post = _norm_bwd(ao, attn_post_g, [dh1], [], BF16, "attn_post_norm_bwd")
    dmix = _matmul(dao, w_o_full, tb=True, name="out_proj_dx")
    dw_o = _matmul(mix, dao, ta=True, name="out_proj_dw")
    do_a, do_b, d_grp_a, d_grp_b, dl_a, dl_b = _mix_bwd(o_a, o_b, grp_a_g, grp_b_g, dmix)
    dl_a = dl_a.T[:, :, None]
    dqh, dkh, dvh, (got["w_gate"], got["w_up"]) = _mla_bwd(
        qh, kh, vh, do_a, lse_a, dl_a, comm=_chip_exchange_comm([low["w_gate"], low["w_up"]]))
    dq4, dkv4, dk_rope = _mla_unprep(dqh, dkh, dvh, cos, sin)
    dw_uq = _matmul(cqn, dq4, ta=True, name="q_up_dw")
    dcqn = _matmul(dq4, full["w_uq"], tb=True, reduce_b=True, name="q_up_dx")
    dw_ukv = _matmul(ckvn, dkv4, ta=True, name="kv_up_dw")
    dckvn = _matmul(dkv4, full["w_ukv"], tb=True, reduce_b=True, name="kv_up_dx")
    dc_q, d_q_norm = _norm_bwd(c_q, q_norm_g, [dcqn], [], BF16, "q_norm_bwd")
    dc_kv, d_kv_norm = _norm_bwd(c_kv, kv_norm_g, [dckvn], [], BF16, "kv_norm_bwd")
    dq_s, dk_cur, dk_prev, dv_cur, dv_prev, d_sinks = _swa_bwd(q_sb, k_sb, v_sb, pos_col, pos_row, sinks, do_b, lse_b, dl_b)
    dk_s = _band_merge(dk_cur, dk_prev, "swa_dk_merge")
    dv_s = _band_merge(dv_cur, dv_prev, "swa_dv_merge")
    dproj = jnp.concatenate([dc_q, dc_kv, dk_rope.astype(BF16), dq_s, dk_s, dv_s], axis=1)
    blk_w = dproj.shape[1] // N_CHIPS
    dproj4 = jnp.stack([dproj[:, b * blk_w:(b + 1) * blk_w] for b in range(N_CHIPS)])
    mid = ("w_o", "w_uq", "w_ukv")
    grads4.update(w_o=dw_o.reshape(N_CHIPS, -1, d_model), w_uq=dw_uq, w_ukv=dw_ukv)
    dw_in, theirs = _matmul(a, dproj4, ta=True, name="in_proj_dw", tm=512,
                            comm=_pair_exchange_comm([grads4[k] for k in mid]))
    for k, t in zip(mid, theirs):
        low[k], own[k] = _pair_add(grads4[k], t, "pair_add_" + k)
    da, got_mid = _matmul(dproj4, full["w_in"], tb=True, reduce_b=True, name="in_proj_dx",
                          comm=_chip_exchange_comm([low[k] for k in mid]))
    got.update(zip(mid, got_mid))
    dx, d_attn_pre = _norm_bwd(xs, attn_pre_g, [da], [dh1], F32, "attn_pre_norm_bwd")

    small_grads = dict(attn_pre_g=d_attn_pre, q_norm_g=d_q_norm, kv_norm_g=d_kv_norm, swa_sinks=d_sinks, grp_a_g=d_grp_a,
                       grp_b_g=d_grp_b, attn_post_g=d_attn_post, ffn_pre_g=d_ffn_pre, ffn_post_g=d_ffn_post)
    parts = [loss_row] + [_pad_lanes(small_grads[k]) for k in SMALL]
    packed = jnp.concatenate(parts, axis=1)
    n_packed = packed.shape[1]
    packed = jnp.pad(packed, ((0, 0), (0, -n_packed % (8 * LANES)))).reshape(8, -1)
    total8 = _all_sum_small(packed)
    total = total8.reshape(1, -1)
    loss = total[0, 0]
    g_small, off = {}, LANES
    for k in SMALL:
        n = gains[k].shape[1]
        g_small[k] = total[:, off:off + n]
        off += n + (-n % LANES)

    def pack_small(prefix):
        flat = jnp.concatenate([jnp.zeros((1, LANES), F32)] + [_pad_lanes(given[prefix + k]) for k in SMALL], axis=1)
        return jnp.pad(flat, ((0, 0), (0, -n_packed % (8 * LANES)))).reshape(8, -1)

    d_sm, m_sm, v_sm = (r.reshape(1, -1) for r in
                        _adamw(pack_small(""), total8, pack_small("m_"), pack_small("v_"), "adamw_small"))
    delta, new_m, new_v, off = {}, {}, {}, LANES
    for k in SMALL:
        n = gains[k].shape[1]
        delta[k], new_m[k], new_v[k] = d_sm[:, off:off + n], m_sm[:, off:off + n], v_sm[:, off:off + n]
        off += n + (-n % LANES)

    grads4["w_in"] = dw_in
    (theirs_in,) = _run_comm(_pair_exchange_comm([dw_in]), "grad_pair_exchange")
    low["w_in"], own["w_in"] = _pair_add(dw_in, theirs_in, "pair_add_w_in")
    (got["w_in"],) = _run_comm(_chip_exchange_comm([low["w_in"]]), "grad_chip_exchange")
    halves = [_chip_add(own[k], got[k], "chip_add_" + k) for k in BIG]
    g_big = dict(zip(BIG, _run_comm(_pair_share_comm(halves), "grad_pair_share")))
    for k in BIG:
        delta[k], new_m[k], new_v[k], g_big[k] = _adamw(
            w32[k], g_big[k], given["m_" + k][0], given["v_" + k][0], "adamw_" + k, emit_grad=True)

    def out(d, k):
        return d[k][None] if k in BIG else d[k]

    grads = {**g_small, **g_big}
    return (loss, dx[None], *[out(grads, k) for k in ORDER], *[out(delta, k) for k in ORDER],
            *[out(new_m, k) for k in ORDER], *[out(new_v, k) for k in ORDER])
```

```python
import functools
import math

import jax
import jax.numpy as jnp
from jax import lax
from jax.experimental import pallas as pl
from jax.experimental.pallas import tpu as pltpu

F32, BF16 = jnp.float32, jnp.bfloat16
MESH = pl.DeviceIdType.MESH
ANY = pl.BlockSpec(memory_space=pl.ANY)

N_CHIPS = 4
EPS = 1e-6
MLA_HEADS, MLA_NOPE, MLA_ROPE, MLA_V = 16, 128, 64, 128
MLA_QK = MLA_NOPE + MLA_ROPE
HEADS_PER_CHIP = MLA_HEADS // N_CHIPS
ROPE_THETA = 10000.0
SWA_Q_HEADS, SWA_KV_HEADS, SWA_HEAD_DIM, SWA_WINDOW = 32, 8, 64, 128
SWA_GROUP = SWA_Q_HEADS // SWA_KV_HEADS
MIX_A, MIX_B = MLA_HEADS * MLA_V, SWA_Q_HEADS * SWA_HEAD_DIM
MASK_VALUE = float(jnp.finfo(jnp.float32).min)
ADAM_LR, ADAM_B1, ADAM_B2, ADAM_EPS, ADAM_WD, ADAM_STEP = 0.001, 0.9, 0.999, 1e-08, 0.01, 10

LANES = 128
VMEM_LIMIT = 56 << 20
ATTN_TILE = 512
ROW_TILE = 256
MM_TILE = 1024


def _fit(dim, pref, mult=LANES):
    if dim <= pref:
        return dim
    for t in range(pref - pref % mult, 0, -mult):
        if dim % t == 0:
            return t
    return dim


def _params(*semantics):
    return pltpu.CompilerParams(dimension_semantics=semantics, vmem_limit_bytes=VMEM_LIMIT)


def _dot(a, b, ca, cb):
    return lax.dot_general(a, b, (((ca,), (cb,)), ((), ())), preferred_element_type=F32)


def _matmul(a, b, *, name, ta=False, tb=False, reduce_b=False, out_dtype=F32, tm=MM_TILE, tn=MM_TILE, tk=MM_TILE, comm=None,
            epilogue=None, extras=()):
    a3, b3 = a.ndim == 3, b.ndim == 3
    nb = a.shape[0] if a3 else (b.shape[0] if b3 else 1)
    (K, M) = a.shape[-2:] if ta else a.shape[-2:][::-1]
    (N, K2) = b.shape[-2:] if tb else b.shape[-2:][::-1]
    assert K == K2, (a.shape, b.shape)
    tm, tn, tk = _fit(M, tm), _fit(N, tn), _fit(K, tk)
    batched_out = (a3 or b3) and not reduce_b
    n_bo = nb if batched_out else 1
    n_br = nb if reduce_b else 1
    nk = K // tk

    def sel(bo, br):
        return br if reduce_b else bo

    def a_map(bo, i, j, br, k):
        t = (k, i) if ta else (i, k)
        return (sel(bo, br),) + t if a3 else t

    def b_map(bo, i, j, br, k):
        t = (j, k) if tb else (k, j)
        return (sel(bo, br),) + t if b3 else t

    def o_map(bo, i, j, br, k):
        return (bo, i, j) if batched_out else (i, j)

    a_blk = (tk, tm) if ta else (tm, tk)
    b_blk = (tn, tk) if tb else (tk, tn)
    a_blk = (None,) + a_blk if a3 else a_blk
    b_blk = (None,) + b_blk if b3 else b_blk
    o_blk = (None, tm, tn) if batched_out else (tm, tn)
    o_shape = (nb, M, N) if batched_out else (M, N)

    grid = (n_bo, M // tm, N // tn, n_br, nk)
    n_ci, n_co = (len(comm.ins), len(comm.out_shapes)) if comm else (0, 0)
    n_x = len(extras)
    out_dtypes = tuple(out_dtype) if epilogue else (out_dtype,)
    n_o = len(out_dtypes)

    def body(*refs):
        a_ref, b_ref = refs[:2]
        x_refs, c_in = refs[2:2 + n_x], refs[2 + n_x:2 + n_x + n_ci]
        refs = refs[2 + n_x + n_ci:]
        o_refs, c_out, acc_ref, sems = refs[:n_o], refs[n_o:n_o + n_co], refs[n_o + n_co], refs[n_o + n_co + 1:]
        br, k = pl.program_id(3), pl.program_id(4)
        first, last = _grid_ends(grid)
        if comm:
            pl.when(first)(lambda: comm.start(c_in, c_out, sems))

        @pl.when((br == 0) & (k == 0))
        def _():
            acc_ref[...] = jnp.zeros_like(acc_ref)

        acc_ref[...] += _dot(a_ref[...], b_ref[...], 0 if ta else 1, 1 if tb else 0)

        @pl.when((br == n_br - 1) & (k == nk - 1))
        def _():
            vals = epilogue(acc_ref[...], *[r[...] for r in x_refs]) if epilogue else (acc_ref[...],)
            for o_ref, v in zip(o_refs, vals):
                o_ref[...] = v.astype(o_ref.dtype)

        if comm:
            pl.when(last)(lambda: comm.finish(c_in, c_out, sems))

    o_spec = pl.BlockSpec(o_blk, o_map)
    res = pl.pallas_call(
        body, name=name,
        out_shape=tuple(jax.ShapeDtypeStruct(o_shape, d) for d in out_dtypes) + tuple(comm.out_shapes if comm else ()),
        grid=grid,
        in_specs=[pl.BlockSpec(a_blk, a_map), pl.BlockSpec(b_blk, b_map)] + [o_spec] * n_x + [ANY] * n_ci,
        out_specs=(o_spec,) * n_o + (ANY,) * n_co,
        input_output_aliases={2 + n_x + i: n_o + o for i, o in comm.aliases.items()} if comm else {},
        scratch_shapes=[pltpu.VMEM((tm, tn), F32)] + (comm.scratch if comm else []),
        compiler_params=_params(*(["parallel"] * 3 + ["arbitrary"] * 2 if not comm else ["arbitrary"] * 5)),
    )(a, b, *extras, *(comm.ins if comm else ()))
    main = res[:n_o] if epilogue else res[0]
    return (main, res[n_o:]) if comm else main


def _inv_rms(u):
    return lax.rsqrt(jnp.mean(u * u, axis=-1, keepdims=True) + EPS)


def _norm_bwd_math(u, g, dz):
    r = _inv_rms(u)
    w = dz * g
    du = r * w - u * (r * r * r * jnp.mean(w * u, axis=-1, keepdims=True))
    dg = jnp.sum(dz * (u * r), axis=0, keepdims=True)
    return du, dg


def _row_spec(tr, n):
    return pl.BlockSpec((tr, n), lambda i: (i, 0))


def _gain_spec(n):
    return pl.BlockSpec((1, n), lambda i: (0, 0))


def _norm_fwd(u, g, out_dtype, name):
    rows, n = u.shape
    tr = _fit(rows, ROW_TILE, 16)

    def body(u_ref, g_ref, o_ref):
        x = u_ref[...]
        o_ref[...] = (x * _inv_rms(x) * g_ref[...]).astype(o_ref.dtype)

    return pl.pallas_call(
        body, name=name, out_shape=jax.ShapeDtypeStruct((rows, n), out_dtype), grid=(rows // tr,),
        in_specs=[_row_spec(tr, n), _gain_spec(n)], out_specs=_row_spec(tr, n), compiler_params=_params("parallel"),
    )(u, g)


def _norm_bwd(u, g, dzs, adds, out_dtype, name):
    rows, n = u.shape
    tr = _fit(rows, ROW_TILE, 16)
    n_dz, n_add = len(dzs), len(adds)

    def body(*refs):
        u_ref, g_ref = refs[:2]
        dz_refs = refs[2:2 + n_dz]
        add_refs = refs[2 + n_dz:2 + n_dz + n_add]
        du_ref, dg_ref = refs[2 + n_dz + n_add:]
        dz = dz_refs[0][...].astype(F32)
        for r in dz_refs[1:]:
            dz = dz + r[...].astype(F32)
        du, dg = _norm_bwd_math(u_ref[...], g_ref[...], dz)
        for r in add_refs:
            du = du + r[...]
        du_ref[...] = du.astype(du_ref.dtype)

        @pl.when(pl.program_id(0) == 0)
        def _():
            dg_ref[...] = jnp.zeros_like(dg_ref)

        dg_ref[...] += dg

    return pl.pallas_call(
        body, name=name,
        out_shape=(jax.ShapeDtypeStruct((rows, n), out_dtype), jax.ShapeDtypeStruct((1, n), F32)), grid=(rows // tr,),
        in_specs=[_row_spec(tr, n), _gain_spec(n)] + [_row_spec(tr, n)] * (n_dz + n_add),
        out_specs=(_row_spec(tr, n), _gain_spec(n)), compiler_params=_params("arbitrary"),
    )(u, g, *dzs, *adds)


def _mix_fwd(o_a, o_b, g_a, g_b):
    rows = o_a.shape[0]
    tr = _fit(rows, ROW_TILE, 16)

    def body(a_ref, b_ref, ga_ref, gb_ref, o_ref):
        a, b = a_ref[...], b_ref[...]
        o_ref[:, :MIX_A] = (a * _inv_rms(a) * ga_ref[...]).astype(BF16)
        o_ref[:, MIX_A:] = (b * _inv_rms(b) * gb_ref[...]).astype(BF16)

    return pl.pallas_call(
        body, name="mix_fwd", out_shape=jax.ShapeDtypeStruct((rows, MIX_A + MIX_B), BF16), grid=(rows // tr,),
        in_specs=[_row_spec(tr, MIX_A), _row_spec(tr, MIX_B), _gain_spec(MIX_A), _gain_spec(MIX_B)],
        out_specs=_row_spec(tr, MIX_A + MIX_B), compiler_params=_params("parallel"),
    )(o_a, o_b, g_a, g_b)


def _mix_bwd(o_a, o_b, g_a, g_b, dmix):
    rows = o_a.shape[0]
    tr = _fit(rows, ROW_TILE, 16)

    def body(a_ref, b_ref, ga_ref, gb_ref, dm_ref, doa_ref, dob_ref, dga_ref, dgb_ref, dla_ref, dlb_ref):
        a, b = a_ref[...], b_ref[...]
        doa, dga = _norm_bwd_math(a, ga_ref[...], dm_ref[:, :MIX_A])
        dob, dgb = _norm_bwd_math(b, gb_ref[...], dm_ref[:, MIX_A:])
        doa_ref[...] = doa.astype(BF16)
        dob_ref[...] = dob.astype(BF16)
        pa, pb = doa * a, dob * b
        for h in range(MLA_HEADS):
            dla_ref[:, h:h + 1] = jnp.sum(pa[:, h * MLA_V:(h + 1) * MLA_V], axis=-1, keepdims=True)
        for h in range(SWA_Q_HEADS):
            dlb_ref[:, h:h + 1] = jnp.sum(pb[:, h * SWA_HEAD_DIM:(h + 1) * SWA_HEAD_DIM], axis=-1, keepdims=True)

        @pl.when(pl.program_id(0) == 0)
        def _():
            dga_ref[...] = jnp.zeros_like(dga_ref)
            dgb_ref[...] = jnp.zeros_like(dgb_ref)

        dga_ref[...] += dga
        dgb_ref[...] += dgb

    return pl.pallas_call(
        body, name="mix_bwd",
        out_shape=(jax.ShapeDtypeStruct((rows, MIX_A), BF16), jax.ShapeDtypeStruct((rows, MIX_B), BF16),
                   jax.ShapeDtypeStruct((1, MIX_A), F32), jax.ShapeDtypeStruct((1, MIX_B), F32),
                   jax.ShapeDtypeStruct((rows, MLA_HEADS), F32), jax.ShapeDtypeStruct((rows, SWA_Q_HEADS), F32)),
        grid=(rows // tr,),
        in_specs=[_row_spec(tr, MIX_A), _row_spec(tr, MIX_B), _gain_spec(MIX_A), _gain_spec(MIX_B),
                  _row_spec(tr, MIX_A + MIX_B)],
        out_specs=(_row_spec(tr, MIX_A), _row_spec(tr, MIX_B), _gain_spec(MIX_A), _gain_spec(MIX_B),
                   _row_spec(tr, MLA_HEADS), _row_spec(tr, SWA_Q_HEADS)),
        compiler_params=_params("arbitrary"),
    )(o_a, o_b, g_a, g_b, dmix)


def _post_pre_fwd(x, ao, g_post, g_pre):
    rows, n = x.shape
    tr = _fit(rows, ROW_TILE, 16)

    def body(x_ref, ao_ref, g1_ref, g2_ref, h_ref, f_ref):
        u = ao_ref[...]
        h = x_ref[...] + u * _inv_rms(u) * g1_ref[...]
        h_ref[...] = h
        f_ref[...] = (h * _inv_rms(h) * g2_ref[...]).astype(BF16)

    return pl.pallas_call(
        body, name="post_pre_fwd",
        out_shape=(jax.ShapeDtypeStruct((rows, n), F32), jax.ShapeDtypeStruct((rows, n), BF16)), grid=(rows // tr,),
        in_specs=[_row_spec(tr, n), _row_spec(tr, n), _gain_spec(n), _gain_spec(n)],
        out_specs=(_row_spec(tr, n), _row_spec(tr, n)), compiler_params=_params("parallel"),
    )(x, ao, g_post, g_pre)


def _loss_bwd(h1, dn, g_post, target):
    rows, n = h1.shape
    tr = _fit(rows, ROW_TILE, 16)

    def body(h_ref, u_ref, g_ref, t_ref, loss_ref, dy_ref, du_ref, dg_ref):
        u, g = u_ref[...], g_ref[...]
        err = h_ref[...] + u * _inv_rms(u) * g - t_ref[...]
        dy = err / n
        dy_ref[...] = dy
        du, dg = _norm_bwd_math(u, g, dy)
        du_ref[...] = du.astype(BF16)

        @pl.when(pl.program_id(0) == 0)
        def _():
            loss_ref[...] = jnp.zeros_like(loss_ref)
            dg_ref[...] = jnp.zeros_like(dg_ref)

        loss_ref[...] += jnp.full((1, LANES), 0.5 * jnp.sum(jnp.mean(err * err, axis=-1)), F32)
        dg_ref[...] += dg

    return pl.pallas_call(
        body, name="loss_bwd",
        out_shape=(jax.ShapeDtypeStruct((1, LANES), F32), jax.ShapeDtypeStruct((rows, n), F32),
                   jax.ShapeDtypeStruct((rows, n), BF16), jax.ShapeDtypeStruct((1, n), F32)),
        grid=(rows // tr,),
        in_specs=[_row_spec(tr, n), _row_spec(tr, n), _gain_spec(n), _row_spec(tr, n)],
        out_specs=(_gain_spec(LANES), _row_spec(tr, n), _row_spec(tr, n), _gain_spec(n)),
        compiler_params=_params("arbitrary"),
    )(h1, dn, g_post, target)


def _blk3_spec(tr, n):
    return pl.BlockSpec((None, tr, n), lambda b, i: (b, i, 0))


def _swiglu_fwd_tile(up, gate):
    return up, gate * jax.nn.sigmoid(gate) * up


def _swiglu_bwd_tile(dact, gate, up):
    sig = jax.nn.sigmoid(gate)
    return dact * up * (sig * (1.0 + gate * (1.0 - sig))), dact * (gate * sig)


def _rope(x, cos, sin):
    half = MLA_ROPE // 2
    x1, x2 = x[:, :half], x[:, half:]
    return jnp.concatenate([x1 * cos - x2 * sin, x2 * cos + x1 * sin], axis=-1)


def _rope_t(d, cos, sin):
    half = MLA_ROPE // 2
    d1, d2 = d[:, :half], d[:, half:]
    return jnp.concatenate([d1 * cos + d2 * sin, d2 * cos - d1 * sin], axis=-1)


def _mla_prep(q4, kv4, k_rope, cos, sin):
    nb, rows, _ = q4.shape
    tr = _fit(rows, ROW_TILE, 16)
    hpc, half = HEADS_PER_CHIP, MLA_ROPE // 2

    def body(q_ref, kv_ref, kr_ref, c_ref, s_ref, qo_ref, ko_ref, vo_ref):
        cos, sin = c_ref[...], s_ref[...]
        k_pe = _rope(kr_ref[...], cos, sin)
        q_all, kv_all = q_ref[...], kv_ref[...]
        for h in range(hpc):
            q = q_all[:, h * MLA_QK:(h + 1) * MLA_QK]
            qo_ref[h] = jnp.concatenate([q[:, :MLA_NOPE], _rope(q[:, MLA_NOPE:], cos, sin)], axis=-1).astype(BF16)
            kv = kv_all[:, h * (MLA_NOPE + MLA_V):(h + 1) * (MLA_NOPE + MLA_V)]
            ko_ref[h] = jnp.concatenate([kv[:, :MLA_NOPE], k_pe], axis=-1).astype(BF16)
            vo_ref[h] = kv[:, MLA_NOPE:].astype(BF16)

    def head_spec(n):
        return pl.BlockSpec((hpc, tr, n), lambda b, i: (b, i, 0))

    def row_spec(n):
        return pl.BlockSpec((tr, n), lambda b, i: (i, 0))

    return pl.pallas_call(
        body, name="mla_prep",
        out_shape=(jax.ShapeDtypeStruct((MLA_HEADS, rows, MLA_QK), BF16), jax.ShapeDtypeStruct((MLA_HEADS, rows, MLA_QK), BF16),
                   jax.ShapeDtypeStruct((MLA_HEADS, rows, MLA_V), BF16)),
        grid=(nb, rows // tr),
        in_specs=[_blk3_spec(tr, hpc * MLA_QK), _blk3_spec(tr, hpc * (MLA_NOPE + MLA_V)), row_spec(MLA_ROPE),
                  row_spec(half), row_spec(half)],
        out_specs=(head_spec(MLA_QK), head_spec(MLA_QK), head_spec(MLA_V)),
        compiler_params=_params("parallel", "parallel"),
    )(q4, kv4, k_rope, cos, sin)


def _mla_unprep(dq, dk, dv, cos, sin):
    _, rows, _ = dq.shape
    tr = _fit(rows, ROW_TILE, 16)
    hpc, half = HEADS_PER_CHIP, MLA_ROPE // 2

    def body(dq_ref, dk_ref, dv_ref, c_ref, s_ref, q4_ref, kv4_ref, kr_ref):
        cos, sin = c_ref[...], s_ref[...]
        d_pe = jnp.zeros((tr, MLA_ROPE), F32)
        q_parts, kv_parts = [], []
        for h in range(hpc):
            g, gk = dq_ref[h], dk_ref[h]
            q_parts += [g[:, :MLA_NOPE], _rope_t(g[:, MLA_NOPE:], cos, sin)]
            kv_parts += [gk[:, :MLA_NOPE], dv_ref[h]]
            d_pe = d_pe + gk[:, MLA_NOPE:]
        q4_ref[...] = jnp.concatenate(q_parts, axis=-1).astype(BF16)
        kv4_ref[...] = jnp.concatenate(kv_parts, axis=-1).astype(BF16)

        @pl.when(pl.program_id(1) == 0)
        def _():
            kr_ref[...] = jnp.zeros_like(kr_ref)

        kr_ref[...] += _rope_t(d_pe, cos, sin)

    def head_spec(n):
        return pl.BlockSpec((hpc, tr, n), lambda i, b: (b, i, 0))

    def row_spec(n):
        return pl.BlockSpec((tr, n), lambda i, b: (i, 0))

    def blk_spec(n):
        return pl.BlockSpec((None, tr, n), lambda i, b: (b, i, 0))

    return pl.pallas_call(
        body, name="mla_unprep",
        out_shape=(jax.ShapeDtypeStruct((N_CHIPS, rows, hpc * MLA_QK), BF16),
                   jax.ShapeDtypeStruct((N_CHIPS, rows, hpc * (MLA_NOPE + MLA_V)), BF16),
                   jax.ShapeDtypeStruct((rows, MLA_ROPE), F32)),
        grid=(rows // tr, N_CHIPS),
        in_specs=[head_spec(MLA_QK), head_spec(MLA_QK), head_spec(MLA_V), row_spec(half), row_spec(half)],
        out_specs=(blk_spec(hpc * MLA_QK), blk_spec(hpc * (MLA_NOPE + MLA_V)), row_spec(MLA_ROPE)),
        compiler_params=_params("parallel", "arbitrary"),
    )(dq, dk, dv, cos, sin)


def _carry(body, comm, n_in, n_out, grid):
    n_ci, n_co = (len(comm.ins), len(comm.out_shapes)) if comm else (0, 0)

    def full(*refs):
        ins, c_in = refs[:n_in], refs[n_in:n_in + n_ci]
        outs = refs[n_in + n_ci:n_in + n_ci + n_out]
        c_out = refs[n_in + n_ci + n_out:n_in + n_ci + n_out + n_co]
        sems = refs[n_in + n_ci + n_out + n_co:]
        first, last = _grid_ends(grid)
        if comm:
            pl.when(first)(lambda: comm.start(c_in, c_out, sems))
        body(ins, outs)
        if comm:
            pl.when(last)(lambda: comm.finish(c_in, c_out, sems))

    extra = dict(
        operands=list(comm.ins) if comm else [], in_specs=[ANY] * n_ci, out_specs=(ANY,) * n_co,
        out_shape=tuple(comm.out_shapes) if comm else (),
        aliases={n_in + i: n_out + o for i, o in comm.aliases.items()} if comm else {},
        scratch=comm.scratch if comm else [])
    return full, extra


def _causal_keep(t):
    return lax.broadcasted_iota(jnp.int32, (t, t), 1) <= lax.broadcasted_iota(jnp.int32, (t, t), 0)


def _mla_fwd(q, k, v, comm=None):
    nh, rows, _ = q.shape
    t = _fit(rows, ATTN_TILE)
    scale = MLA_QK ** -0.5
    grid = (nh, rows // t)

    def body(ins, outs):
        (q_ref, k_ref, v_ref), (o_ref, lse_ref) = ins, outs
        i = pl.program_id(1)
        qb = q_ref[...]

        def step(j, carry, diagonal):
            m, l, acc = carry
            rows_j = pl.ds(pl.multiple_of(j * t, t), t)
            s = _dot(qb, k_ref[rows_j, :], 1, 1) * scale
            if diagonal:
                s = jnp.where(_causal_keep(t), s, MASK_VALUE)
            m_new = jnp.maximum(m, jnp.max(s, axis=-1, keepdims=True))
            alpha = jnp.exp(m - m_new)
            p = jnp.exp(s - m_new)
            l = alpha * l + jnp.sum(p, axis=-1, keepdims=True)
            acc = alpha * acc + _dot(p.astype(BF16), v_ref[rows_j, :], 1, 0)
            return m_new, l, acc

        init = (jnp.full((t, 1), MASK_VALUE, F32), jnp.zeros((t, 1), F32), jnp.zeros((t, MLA_V), F32))
        carry = lax.fori_loop(0, i, lambda j, c: step(j, c, False), init)
        m, l, acc = step(i, carry, True)
        o_ref[...] = acc / l
        lse_ref[...] = m + jnp.log(l)

    full, extra = _carry(body, comm, 3, 2, grid)
    res = pl.pallas_call(
        full, name="mla_fwd",
        out_shape=(jax.ShapeDtypeStruct((rows, nh * MLA_V), F32), jax.ShapeDtypeStruct((nh, rows, 1), F32)) + extra["out_shape"],
        grid=grid,
        in_specs=[pl.BlockSpec((None, t, MLA_QK), lambda h, i: (h, i, 0)),
                  pl.BlockSpec((None, rows, MLA_QK), lambda h, i: (h, 0, 0)),
                  pl.BlockSpec((None, rows, MLA_V), lambda h, i: (h, 0, 0))] + extra["in_specs"],
        out_specs=(pl.BlockSpec((t, MLA_V), lambda h, i: (i, h)), pl.BlockSpec((None, t, 1), lambda h, i: (h, i, 0))) + extra["out_specs"],
        input_output_aliases=extra["aliases"], scratch_shapes=extra["scratch"],
        compiler_params=_params("arbitrary", "arbitrary"),
    )(q, k, v, *extra["operands"])
    return res[0], res[1], res[2:]


def _mla_bwd(q, k, v, do, lse, delta, comm=None):
    nh, rows, _ = q.shape
    t = _fit(rows, ATTN_TILE)
    n_t = rows // t
    scale = MLA_QK ** -0.5
    grid = (nh, n_t)

    def body(ins, outs):
        (q_ref, k_ref, v_ref, do_ref, lse_ref, dl_ref), (dq_ref, dk_ref, dv_ref) = ins, outs
        j = pl.program_id(1)
        kb, vb = k_ref[...], v_ref[...]

        @pl.when(j == 0)
        def _():
            dq_ref[...] = jnp.zeros_like(dq_ref)

        def step(i, carry, diagonal):
            dk, dv = carry
            rows_i = pl.ds(pl.multiple_of(i * t, t), t)
            qb, dob = q_ref[rows_i, :], do_ref[rows_i, :]
            s = _dot(qb, kb, 1, 1) * scale
            if diagonal:
                s = jnp.where(_causal_keep(t), s, MASK_VALUE)
            p = jnp.exp(s - lse_ref[rows_i, :])
            dv = dv + _dot(p.astype(BF16), dob, 0, 0)
            dp = _dot(dob, vb, 1, 1)
            ds = (p * (dp - dl_ref[rows_i, :]) * scale).astype(BF16)
            dk = dk + _dot(ds, qb, 0, 0)
            dq_ref[rows_i, :] += _dot(ds, kb, 1, 0)
            return dk, dv

        carry = step(j, (jnp.zeros((t, MLA_QK), F32), jnp.zeros((t, MLA_V), F32)), True)
        dk, dv = lax.fori_loop(j + 1, n_t, lambda i, c: step(i, c, False), carry)
        dk_ref[...] = dk
        dv_ref[...] = dv

    def head(n):
        return pl.BlockSpec((None, rows, n), lambda h, j: (h, 0, 0))

    def tile(n):
        return pl.BlockSpec((None, t, n), lambda h, j: (h, j, 0))

    full, extra = _carry(body, comm, 6, 3, grid)
    res = pl.pallas_call(
        full, name="mla_bwd",
        out_shape=(jax.ShapeDtypeStruct((nh, rows, MLA_QK), F32), jax.ShapeDtypeStruct((nh, rows, MLA_QK), F32),
                   jax.ShapeDtypeStruct((nh, rows, MLA_V), F32)) + extra["out_shape"],
        grid=grid,
        in_specs=[head(MLA_QK), tile(MLA_QK), tile(MLA_V), pl.BlockSpec((rows, MLA_V), lambda h, j: (0, h)), head(1), head(1)]
        + extra["in_specs"],
        out_specs=(head(MLA_QK), tile(MLA_QK), tile(MLA_V)) + extra["out_specs"],
        input_output_aliases=extra["aliases"], scratch_shapes=extra["scratch"],
        compiler_params=_params("arbitrary", "arbitrary"),
    )(q, k, v, do, lse, delta, *extra["operands"])
    return res[0], res[1], res[2], res[3:]


def _swa_slope(h):
    return 2.0 ** (-8.0 * (h + 1) / SWA_Q_HEADS)


def _swa_band_specs(rows):
    w = SWA_WINDOW
    kvw = SWA_KV_HEADS * SWA_HEAD_DIM

    def prev(i):
        return jnp.maximum(i - 1, 0)

    return [pl.BlockSpec((w, kvw), lambda i: (prev(i), 0)), pl.BlockSpec((w, kvw), lambda i: (i, 0)),
            pl.BlockSpec((w, kvw), lambda i: (prev(i), 0)), pl.BlockSpec((w, kvw), lambda i: (i, 0)),
            pl.BlockSpec((w, 1), lambda i: (i, 0)),
            pl.BlockSpec((1, w), lambda i: (0, prev(i))), pl.BlockSpec((1, w), lambda i: (0, i)),
            pl.BlockSpec(memory_space=pltpu.SMEM)]


def _swa_bias(i, pc_ref, pp_ref, pn_ref):
    w = SWA_WINDOW
    k_pos = jnp.concatenate([pp_ref[...], pn_ref[...]], axis=1)
    dist = jnp.abs(pc_ref[...] - k_pos).astype(F32)
    r = lax.broadcasted_iota(jnp.int32, (w, 2 * w), 0)
    col = lax.broadcasted_iota(jnp.int32, (w, 2 * w), 1)
    delta = r + w - col
    valid = (delta >= 0) & (delta < w) & ((col >= w) | (i > 0))
    return dist, valid


def _swa_fwd(q, k, v, pos_col, pos_row, sinks):
    rows = q.shape[0]
    w, hd = SWA_WINDOW, SWA_HEAD_DIM
    scale = hd ** -0.5

    def body(q_ref, kp_ref, kc_ref, vp_ref, vc_ref, pc_ref, pp_ref, pn_ref, sink_ref, o_ref, lse_ref):
        dist, valid = _swa_bias(pl.program_id(0), pc_ref, pp_ref, pn_ref)
        for kvh in range(SWA_KV_HEADS):
            cols = slice(kvh * hd, (kvh + 1) * hd)
            kb = jnp.concatenate([kp_ref[:, cols], kc_ref[:, cols]], axis=0)
            vb = jnp.concatenate([vp_ref[:, cols], vc_ref[:, cols]], axis=0)
            for g in range(SWA_GROUP):
                h = kvh * SWA_GROUP + g
                sink = sink_ref[h]
                s = _dot(q_ref[:, h * hd:(h + 1) * hd], kb, 1, 1) * scale - _swa_slope(h) * dist
                s = jnp.where(valid, s, MASK_VALUE)
                m = jnp.maximum(jnp.max(s, axis=-1, keepdims=True), sink)
                e = jnp.exp(s - m)
                den = jnp.sum(e, axis=-1, keepdims=True) + jnp.exp(sink - m)
                o_ref[:, h * hd:(h + 1) * hd] = _dot((e / den).astype(BF16), vb, 1, 0)
                lse_ref[:, h:h + 1] = m + jnp.log(den)

    return pl.pallas_call(
        body, name="swa_fwd",
        out_shape=(jax.ShapeDtypeStruct((rows, MIX_B), F32), jax.ShapeDtypeStruct((rows, SWA_Q_HEADS), F32)),
        grid=(rows // w,),
        in_specs=[pl.BlockSpec((w, MIX_B), lambda i: (i, 0))] + _swa_band_specs(rows),
        out_specs=(pl.BlockSpec((w, MIX_B), lambda i: (i, 0)), pl.BlockSpec((w, SWA_Q_HEADS), lambda i: (i, 0))),
        compiler_params=_params("parallel"),
    )(q, k, k, v, v, pos_col, pos_row, pos_row, sinks)


def _swa_bwd(q, k, v, pos_col, pos_row, sinks, do, lse, delta):
    rows = q.shape[0]
    w, hd = SWA_WINDOW, SWA_HEAD_DIM
    kvw = SWA_KV_HEADS * hd
    scale = hd ** -0.5

    def body(q_ref, kp_ref, kc_ref, vp_ref, vc_ref, pc_ref, pp_ref, pn_ref, sink_ref, do_ref, lse_ref, dl_ref,
             dq_ref, dkc_ref, dkp_ref, dvc_ref, dvp_ref, dsink_ref):
        dist, valid = _swa_bias(pl.program_id(0), pc_ref, pp_ref, pn_ref)

        @pl.when(pl.program_id(0) == 0)
        def _():
            dsink_ref[...] = jnp.zeros_like(dsink_ref)

        for kvh in range(SWA_KV_HEADS):
            cols = slice(kvh * hd, (kvh + 1) * hd)
            kb = jnp.concatenate([kp_ref[:, cols], kc_ref[:, cols]], axis=0)
            vb = jnp.concatenate([vp_ref[:, cols], vc_ref[:, cols]], axis=0)
            dk = jnp.zeros((2 * w, hd), F32)
            dv = jnp.zeros((2 * w, hd), F32)
            for g in range(SWA_GROUP):
                h = kvh * SWA_GROUP + g
                hc = slice(h * hd, (h + 1) * hd)
                qb, dob = q_ref[:, hc], do_ref[:, hc]
                lse_h, dl_h = lse_ref[:, h:h + 1], dl_ref[:, h:h + 1]
                s = _dot(qb, kb, 1, 1) * scale - _swa_slope(h) * dist
                s = jnp.where(valid, s, MASK_VALUE)
                p = jnp.exp(s - lse_h)
                ds = (p * (_dot(dob, vb, 1, 1) - dl_h) * scale).astype(BF16)
                dq_ref[:, hc] = _dot(ds, kb, 1, 0).astype(BF16)
                dk = dk + _dot(ds, qb, 0, 0)
                dv = dv + _dot(p.astype(BF16), dob, 0, 0)
                dsink_ref[:, h:h + 1] += -jnp.sum(jnp.exp(sink_ref[h] - lse_h) * dl_h, axis=0, keepdims=True)
            dkp_ref[:, cols] = dk[:w]
            dkc_ref[:, cols] = dk[w:]
            dvp_ref[:, cols] = dv[:w]
            dvc_ref[:, cols] = dv[w:]

    def blk(n):
        return pl.BlockSpec((w, n), lambda i: (i, 0))

    return pl.pallas_call(
        body, name="swa_bwd",
        out_shape=(jax.ShapeDtypeStruct((rows, MIX_B), BF16),) + (jax.ShapeDtypeStruct((rows, kvw), F32),) * 4
        + (jax.ShapeDtypeStruct((1, SWA_Q_HEADS), F32),),
        grid=(rows // w,),
        in_specs=[blk(MIX_B)] + _swa_band_specs(rows) + [blk(MIX_B), blk(SWA_Q_HEADS), blk(SWA_Q_HEADS)],
        out_specs=(blk(MIX_B), blk(kvw), blk(kvw), blk(kvw), blk(kvw), pl.BlockSpec((1, SWA_Q_HEADS), lambda i: (0, 0))),
        compiler_params=_params("arbitrary"),
    )(q, k, k, v, v, pos_col, pos_row, pos_row, sinks, do, lse, delta)


def _band_merge(cur, prev, name):
    rows, n = cur.shape
    w = SWA_WINDOW
    last = rows // w - 1

    def body(c_ref, p_ref, o_ref):
        nxt = jnp.where(pl.program_id(0) < last, p_ref[...], 0.0)
        o_ref[...] = (c_ref[...] + nxt).astype(BF16)

    return pl.pallas_call(
        body, name=name, out_shape=jax.ShapeDtypeStruct((rows, n), BF16), grid=(rows // w,),
        in_specs=[pl.BlockSpec((w, n), lambda j: (j, 0)), pl.BlockSpec((w, n), lambda j: (jnp.minimum(j + 1, last), 0))],
        out_specs=pl.BlockSpec((w, n), lambda j: (j, 0)), compiler_params=_params("parallel"),
    )(cur, prev)


def _adamw(w, g, m, v, name, emit_grad=False):
    rows, cols = w.shape[-2:]
    tr, tc = _fit(rows, 352, 8), _fit(cols, MM_TILE)
    c1 = 1.0 - ADAM_B1 ** ADAM_STEP
    c2 = 1.0 - ADAM_B2 ** ADAM_STEP
    n_out = 4 if emit_grad else 3
    lead = w.ndim == 3

    def body(w_ref, g_ref, m_ref, v_ref, d_ref, mo_ref, vo_ref, *go_ref):
        gr = g_ref[...]
        m_new = ADAM_B1 * m_ref[...] + (1.0 - ADAM_B1) * gr
        v_new = ADAM_B2 * v_ref[...] + (1.0 - ADAM_B2) * jnp.square(gr)
        mo_ref[...] = m_new
        vo_ref[...] = v_new
        d_ref[...] = -ADAM_LR * ((m_new / c1) / (jnp.sqrt(v_new / c2) + ADAM_EPS) + ADAM_WD * w_ref[...])
        if emit_grad:
            go_ref[0][...] = gr

    g_spec = pl.BlockSpec((tr, tc), lambda i, j: (i, j))
    spec = pl.BlockSpec((None, tr, tc), lambda i, j: (0, i, j)) if lead else g_spec
    return pl.pallas_call(
        body, name=name, out_shape=(jax.ShapeDtypeStruct(w.shape, F32),) * n_out, grid=(rows // tr, cols // tc),
        in_specs=[spec, g_spec, spec, spec], out_specs=(spec,) * n_out, compiler_params=_params("parallel", "parallel"),
    )(w, g, m, v)


OTHER_CHIPS = ((1, 0), (0, 1), (1, 1))


def _place():
    x, y, c = lax.axis_index("x"), lax.axis_index("y"), lax.axis_index("c")
    return x, y, c


def _flip(v, f):
    return 1 - v if f else v


class _Comm:
    def __init__(self, ins, out_shapes, aliases, sem_sizes, start, finish):
        self.ins, self.out_shapes, self.aliases, self.sem_sizes = list(ins), list(out_shapes), dict(aliases), list(sem_sizes)
        self.start, self.finish = start, finish

    @property
    def scratch(self):
        return [pltpu.SemaphoreType.DMA((n,)) for n in self.sem_sizes]


def _run_comm(comm, name):
    n_in, n_out = len(comm.ins), len(comm.out_shapes)

    def body(*refs):
        ins, outs, sems = refs[:n_in], refs[n_in:n_in + n_out], refs[n_in + n_out:]
        comm.start(ins, outs, sems)
        comm.finish(ins, outs, sems)

    return pl.pallas_call(
        body, name=name, out_shape=tuple(comm.out_shapes), in_specs=[ANY] * n_in, out_specs=(ANY,) * n_out,
        input_output_aliases=comm.aliases, scratch_shapes=comm.scratch,
    )(*comm.ins)


def _grid_ends(grid):
    first = last = None
    for axis, n in enumerate(grid):
        pid = pl.program_id(axis)
        first = (pid == 0) if first is None else first & (pid == 0)
        last = (pid == n - 1) if last is None else last & (pid == n - 1)
    return first, last


def _cast_into_slot(w, name):
    _, rows, cols = w.shape
    tr, tc = _fit(rows, 704, 16), _fit(cols, MM_TILE)

    def body(w_ref, o_ref):
        o_ref[...] = w_ref[...].astype(BF16)

    return pl.pallas_call(
        body, name=name, out_shape=jax.ShapeDtypeStruct((N_CHIPS, rows, cols), BF16), grid=(rows // tr, cols // tc),
        in_specs=[pl.BlockSpec((None, tr, tc), lambda i, j: (0, i, j))],
        out_specs=pl.BlockSpec((None, tr, tc), lambda i, j: (2 * lax.axis_index("x") + lax.axis_index("y"), i, j)),
        compiler_params=_params("parallel", "parallel"),
    )(w)


def _gather_comm(slots):
    n = len(slots)
    pairs = [(a, j) for a in range(n) for j in range(3)]

    def copies(src, dst, sems):
        send, recv, fsend, frecv = sems
        x, y, c = _place()
        me = 2 * x + y

        def rows(a, core):
            half = slots[a].shape[1] // 2
            return pl.ds(core * half, half)

        def chip(j):
            return _flip(x, OTHER_CHIPS[j][0]), _flip(y, OTHER_CHIPS[j][1])

        def out(a, j):
            px, py = chip(j)
            return pltpu.make_async_remote_copy(
                src_ref=src[a].at[me, rows(a, c)], dst_ref=dst[a].at[me, rows(a, c)], send_sem=send.at[a * 3 + j],
                recv_sem=recv.at[a * 3 + j], device_id=(px, py, c), device_id_type=MESH)

        def landed(a, j):
            px, py = chip(j)
            blk = 2 * px + py
            return pltpu.make_async_remote_copy(
                src_ref=dst[a].at[blk, rows(a, c)], dst_ref=dst[a].at[blk, rows(a, c)], send_sem=send.at[a * 3 + j],
                recv_sem=recv.at[a * 3 + j], device_id=(x, y, c), device_id_type=MESH)

        def passed(a, j, core):
            px, py = chip(j)
            blk = 2 * px + py
            return pltpu.make_async_remote_copy(
                src_ref=dst[a].at[blk, rows(a, core)], dst_ref=dst[a].at[blk, rows(a, core)], send_sem=fsend.at[a * 3 + j],
                recv_sem=frecv.at[a * 3 + j], device_id=(x, y, 1 - c), device_id_type=MESH)

        return c, out, landed, passed

    def start(src, dst, sems):
        _, out, _, _ = copies(src, dst, sems)
        for a, j in pairs:
            out(a, j).start()

    def finish(src, dst, sems):
        c, out, landed, passed = copies(src, dst, sems)
        for a, j in pairs:
            landed(a, j).wait_recv()
            passed(a, j, c).start()
        for a, j in pairs:
            passed(a, j, 1 - c).wait_recv()
        for a, j in pairs:
            out(a, j).wait_send()
            passed(a, j, c).wait_send()

    shapes = [jax.ShapeDtypeStruct(s.shape, s.dtype) for s in slots]
    return _Comm(slots, shapes, {a: a for a in range(n)}, [3 * n] * 4, start, finish)


def _pair_exchange_comm(grads):
    n = len(grads)

    def copy(src, dst, sems, a):
        x, y, c = _place()
        half = grads[a].shape[1] // 2
        return pltpu.make_async_remote_copy(
            src_ref=src[a].at[:, pl.ds((1 - c) * half, half)], dst_ref=dst[a], send_sem=sems[0].at[a],
            recv_sem=sems[1].at[a], device_id=(x, y, 1 - c), device_id_type=MESH)

    def start(src, dst, sems):
        for a in range(n):
            copy(src, dst, sems, a).start()

    def finish(src, dst, sems):
        for a in range(n):
            copy(src, dst, sems, a).wait()

    shapes = [jax.ShapeDtypeStruct((g.shape[0], g.shape[1] // 2, g.shape[2]), g.dtype) for g in grads]
    return _Comm(grads, shapes, {}, [n, n], start, finish)


def _pair_add(mine, theirs, name):
    nb, rows, cols = theirs.shape
    tr, tc = _fit(rows, 704, 16), _fit(cols, MM_TILE)
    n_r = rows // tr

    def body(a_ref, b_ref, lo_ref, own_ref):
        s = a_ref[...] + b_ref[...]
        lo_ref[...] = s.astype(BF16)

        @pl.when(pl.program_id(2) == 2 * lax.axis_index("x") + lax.axis_index("y"))
        def _():
            own_ref[...] = s

    return pl.pallas_call(
        body, name=name, grid=(n_r, cols // tc, nb),
        in_specs=[pl.BlockSpec((None, tr, tc), lambda i, j, b: (b, lax.axis_index("c") * n_r + i, j)),
                  pl.BlockSpec((None, tr, tc), lambda i, j, b: (b, i, j))],
        out_specs=(pl.BlockSpec((None, tr, tc), lambda i, j, b: (b, i, j)), pl.BlockSpec((tr, tc), lambda i, j, b: (i, j))),
        out_shape=(jax.ShapeDtypeStruct(theirs.shape, BF16), jax.ShapeDtypeStruct((rows, cols), F32)),
        compiler_params=_params("parallel", "parallel", "arbitrary"),
    )(mine, theirs)


def _chip_exchange_comm(sums):
    n = len(sums)
    pairs = [(a, j) for a in range(n) for j in range(3)]

    def copy(src, dst, sems, a, j):
        x, y, c = _place()
        px, py = _flip(x, OTHER_CHIPS[j][0]), _flip(y, OTHER_CHIPS[j][1])
        return pltpu.make_async_remote_copy(
            src_ref=src[a].at[2 * px + py], dst_ref=dst[a].at[j], send_sem=sems[0].at[a * 3 + j],
            recv_sem=sems[1].at[a * 3 + j], device_id=(px, py, c), device_id_type=MESH)

    def start(src, dst, sems):
        for a, j in pairs:
            copy(src, dst, sems, a, j).start()

    def finish(src, dst, sems):
        for a, j in pairs:
            copy(src, dst, sems, a, j).wait()

    shapes = [jax.ShapeDtypeStruct((3,) + s.shape[1:], s.dtype) for s in sums]
    return _Comm(sums, shapes, {}, [3 * n, 3 * n], start, finish)


def _chip_add(own, got, name):
    rows, cols = own.shape
    tr, tc = _fit(rows, 704, 16), _fit(cols, MM_TILE)
    n_r = rows // tr

    def body(o_ref, g_ref, out_ref):
        out_ref[...] = ((o_ref[...] + g_ref[0].astype(F32)) + g_ref[1].astype(F32)) + g_ref[2].astype(F32)

    return pl.pallas_call(
        body, name=name, out_shape=jax.ShapeDtypeStruct((2 * rows, cols), F32), grid=(n_r, cols // tc),
        in_specs=[pl.BlockSpec((tr, tc), lambda i, j: (i, j)), pl.BlockSpec((3, tr, tc), lambda i, j: (0, i, j))],
        out_specs=pl.BlockSpec((tr, tc), lambda i, j: (lax.axis_index("c") * n_r + i, j)),
        compiler_params=_params("parallel", "parallel"),
    )(own, got)


def _pair_share_comm(grads):
    n = len(grads)

    def copy(src, dst, sems, a, mine):
        x, y, c = _place()
        half = grads[a].shape[0] // 2
        rows = pl.ds((c if mine else 1 - c) * half, half)
        return pltpu.make_async_remote_copy(
            src_ref=src[a].at[rows], dst_ref=dst[a].at[rows], send_sem=sems[0].at[a], recv_sem=sems[1].at[a],
            device_id=(x, y, 1 - c), device_id_type=MESH)

    def start(src, dst, sems):
        for a in range(n):
            copy(src, dst, sems, a, True).start()

    def finish(src, dst, sems):
        for a in range(n):
            copy(src, dst, sems, a, False).wait_recv()
            copy(src, dst, sems, a, True).wait_send()

    shapes = [jax.ShapeDtypeStruct(g.shape, g.dtype) for g in grads]
    return _Comm(grads, shapes, {a: a for a in range(n)}, [n, n], start, finish)


def _all_sum_small(vec):
    r, n = vec.shape
    flips = [(a, b, d) for a in (0, 1) for b in (0, 1) for d in (0, 1)][1:]

    def body(v_ref, o_ref, buf, send, recv):
        x, y, c = _place()
        me = 4 * x + 2 * y + c
        cps = []
        for k, (fx, fy, fc) in enumerate(flips):
            cp = pltpu.make_async_remote_copy(
                src_ref=v_ref, dst_ref=buf.at[me], send_sem=send.at[k], recv_sem=recv.at[k],
                device_id=(_flip(x, fx), _flip(y, fy), _flip(c, fc)), device_id_type=MESH)
            cp.start()
            cps.append(cp)
        buf[me] = v_ref[...]
        for k, (fx, fy, fc) in enumerate(flips):
            peer = 4 * _flip(x, fx) + 2 * _flip(y, fy) + _flip(c, fc)
            pltpu.make_async_remote_copy(
                src_ref=v_ref, dst_ref=buf.at[peer], send_sem=send.at[k], recv_sem=recv.at[k],
                device_id=(x, y, c), device_id_type=MESH).wait_recv()
        for cp in cps:
            cp.wait_send()
        acc = buf[0]
        for d in range(1, 8):
            acc = acc + buf[d]
        o_ref[...] = acc

    return pl.pallas_call(
        body, name="all_sum_small", out_shape=jax.ShapeDtypeStruct((r, n), F32),
        in_specs=[pl.BlockSpec(memory_space=pltpu.VMEM)], out_specs=pl.BlockSpec(memory_space=pltpu.VMEM),
        scratch_shapes=[pltpu.VMEM((8, r, n), F32), pltpu.SemaphoreType.DMA((7,)), pltpu.SemaphoreType.DMA((7,))],
    )(vec)


SMALL = ("attn_pre_g", "q_norm_g", "kv_norm_g", "swa_sinks", "grp_a_g", "grp_b_g", "attn_post_g", "ffn_pre_g", "ffn_post_g")
BIG = ("w_in", "w_uq", "w_ukv", "w_o", "w_gate", "w_up", "w_down")
ORDER = ("attn_pre_g", "w_in", "q_norm_g", "w_uq", "kv_norm_g", "w_ukv", "swa_sinks", "grp_a_g", "grp_b_g", "w_o",
         "attn_post_g", "ffn_pre_g", "w_gate", "w_up", "w_down", "ffn_post_g")


def _pad_lanes(v):
    n = v.shape[1]
    return jnp.pad(v, ((0, 0), (0, -n % LANES)))


def kernel(x, positions, attn_pre_g, w_in, q_norm_g, w_uq, kv_norm_g, w_ukv, swa_sinks, grp_a_g, grp_b_g, w_o, attn_post_g, ffn_pre_g, w_gate, w_up, w_down, ffn_post_g, loss_target, m_attn_pre_g, m_w_in, m_q_norm_g, m_w_uq, m_kv_norm_g, m_w_ukv, m_swa_sinks, m_grp_a_g, m_grp_b_g, m_w_o, m_attn_post_g, m_ffn_pre_g, m_w_gate, m_w_up, m_w_down, m_ffn_post_g, v_attn_pre_g, v_w_in, v_q_norm_g, v_w_uq, v_kv_norm_g, v_w_ukv, v_swa_sinks, v_grp_a_g, v_grp_b_g, v_w_o, v_attn_post_g, v_ffn_pre_g, v_w_gate, v_w_up, v_w_down, v_ffn_post_g):
    given = dict(locals())
    w32 = {k: given[k] for k in BIG}
    gains = {k: given[k] for k in SMALL}
    xs, tgt = x[0], loss_target[0]
    seq, d_model = xs.shape
    q_rank, kv_rank = q_norm_g.shape[1], kv_norm_g.shape[1]
    kvw = SWA_KV_HEADS * SWA_HEAD_DIM

    slot = {k: _cast_into_slot(w32[k], "cast_" + k) for k in BIG}
    full = {}
    (full["w_in"],) = _run_comm(_gather_comm([slot["w_in"]]), "gather_w_in")

    pos = positions[0]
    inv = 1.0 / (ROPE_THETA ** (jnp.arange(0, MLA_ROPE, 2, dtype=F32) / MLA_ROPE))
    ang = pos.astype(F32)[:, None] * inv
    cos, sin = jnp.cos(ang), jnp.sin(ang)
    pos_col, pos_row = pos[:, None], pos[None, :]
    sinks = swa_sinks[0]

    a = _norm_fwd(xs, attn_pre_g, BF16, "attn_pre_norm")
    proj4, (full["w_uq"], full["w_ukv"], full["w_o"]) = _matmul(
        a, full["w_in"], name="in_proj", comm=_gather_comm([slot["w_uq"], slot["w_ukv"], slot["w_o"]]))
    proj = jnp.concatenate([proj4[b] for b in range(N_CHIPS)], axis=1)
    cuts = (0, q_rank, q_rank + kv_rank, q_rank + kv_rank + MLA_ROPE)
    cuts = cuts + (cuts[3] + MIX_B, cuts[3] + MIX_B + kvw, cuts[3] + MIX_B + 2 * kvw)
    c_q, c_kv, k_rope, q_s, k_s, v_s = (proj[:, lo:hi] for lo, hi in zip(cuts[:-1], cuts[1:]))
    cqn = _norm_fwd(c_q, q_norm_g, BF16, "q_norm")
    ckvn = _norm_fwd(c_kv, kv_norm_g, BF16, "kv_norm")
    q4 = _matmul(cqn, full["w_uq"], name="q_up")
    kv4 = _matmul(ckvn, full["w_ukv"], name="kv_up")
    qh, kh, vh = _mla_prep(q4, kv4, k_rope, cos, sin)
    o_a, lse_a, (full["w_gate"],) = _mla_fwd(qh, kh, vh, comm=_gather_comm([slot["w_gate"]]))
    q_sb, k_sb, v_sb = q_s.astype(BF16), k_s.astype(BF16), v_s.astype(BF16)
    o_b, lse_b = _swa_fwd(q_sb, k_sb, v_sb, pos_col, pos_row, sinks)
    mix = _mix_fwd(o_a, o_b, grp_a_g, grp_b_g)
    w_o_full = full["w_o"].reshape(N_CHIPS * full["w_o"].shape[1], d_model)
    ao = _matmul(mix, w_o_full, name="out_proj")
    h1, f = _post_pre_fwd(xs, ao, attn_post_g, ffn_pre_g)
    gate, (full["w_up"],) = _matmul(f, full["w_gate"], name="ffn_gate", comm=_gather_comm([slot["w_up"]]))
    (up, act), (full["w_down"],) = _matmul(
        f, full["w_up"], name="ffn_up", comm=_gather_comm([slot["w_down"]]), epilogue=_swiglu_fwd_tile, extras=(gate,),
        out_dtype=(F32, BF16), tm=512)
    dn = _matmul(act, full["w_down"], reduce_b=True, name="ffn_down")
    loss_row, dy, ddn, d_ffn_post = _loss_bwd(h1, dn, ffn_post_g, tgt)

    dgate, dup = _matmul(ddn, full["w_down"], tb=True, name="ffn_down_dx", epilogue=_swiglu_bwd_tile, extras=(gate, up),
                         out_dtype=(BF16, BF16), tm=512, tk=512)
    dw_down = _matmul(act, ddn, ta=True, name="ffn_down_dw", tn=512)
    dw_gate = _matmul(f, dgate, ta=True, name="ffn_gate_dw", tm=512)
    dw_up = _matmul(f, dup, ta=True, name="ffn_up_dw", tm=512)
    ffn = ("w_down", "w_gate", "w_up")
    grads4 = dict(w_down=dw_down, w_gate=dw_gate, w_up=dw_up)
    df_g, theirs = _matmul(dgate, full["w_gate"], tb=True, reduce_b=True, name="ffn_gate_dx",
                           comm=_pair_exchange_comm([grads4[k] for k in ffn]))
    low, own, got = {}, {}, {}
    for k, t in zip(ffn, theirs):
        low[k], own[k] = _pair_add(grads4[k], t, "pair_add_" + k)
    df_u, (got["w_down"],) = _matmul(dup, full["w_up"], tb=True, reduce_b=True, name="ffn_up_dx",
                                     comm=_chip_exchange_comm([low["w_down"]]))
    dh1, d_ffn_pre = _norm_bwd(h1, ffn_pre_g, [df_g, df_u], [dy], F32, "ffn_pre_norm_bwd")
    dao, d_attn_post = _norm_bwd(ao, attn_post_g, [dh1], [], BF16, "attn_post_norm_bwd")
    dmix = _matmul(dao, w_o_full, tb=True, name="out_proj_dx")
    dw_o = _matmul(mix, dao, ta=True, name="out_proj_dw")
    do_a, do_b, d_grp_a, d_grp_b, dl_a, dl_b = _mix_bwd(o_a, o_b, grp_a_g, grp_b_g, dmix)
    dl_a = dl_a.T[:, :, None]
    dqh, dkh, dvh, (got["w_gate"], got["w_up"]) = _mla_bwd(
        qh, kh, vh, do_a, lse_a, dl_a, comm=_chip_exchange_comm([low["w_gate"], low["w_up"]]))
    dq4, dkv4, dk_rope = _mla_unprep(dqh, dkh, dvh, cos, sin)
    dw_uq = _matmul(cqn, dq4, ta=True, name="q_up_dw")
    dcqn = _matmul(dq4, full["w_uq"], tb=True, reduce_b=True, name="q_up_dx")
    dw_ukv = _matmul(ckvn, dkv4, ta=True, name="kv_up_dw")
    dckvn = _matmul(dkv4, full["w_ukv"], tb=True, reduce_b=True, name="kv_up_dx")
    dc_q, d_q_norm = _norm_bwd(c_q, q_norm_g, [dcqn], [], BF16, "q_norm_bwd")
    dc_kv, d_kv_norm = _norm_bwd(c_kv, kv_norm_g, [dckvn], [], BF16, "kv_norm_bwd")
    dq_s, dk_cur, dk_prev, dv_cur, dv_prev, d_sinks = _swa_bwd(q_sb, k_sb, v_sb, pos_col, pos_row, sinks, do_b, lse_b, dl_b)
    dk_s = _band_merge(dk_cur, dk_prev, "swa_dk_merge")
    dv_s = _band_merge(dv_cur, dv_prev, "swa_dv_merge")
    dproj = jnp.concatenate([dc_q, dc_kv, dk_rope.astype(BF16), dq_s, dk_s, dv_s], axis=1)
    blk_w = dproj.shape[1] // N_CHIPS
    dproj4 = jnp.stack([dproj[:, b * blk_w:(b + 1) * blk_w] for b in range(N_CHIPS)])
    mid = ("w_o", "w_uq", "w_ukv")
    grads4.update(w_o=dw_o.reshape(N_CHIPS, -1, d_model), w_uq=dw_uq, w_ukv=dw_ukv)
    dw_in, theirs = _matmul(a, dproj4, ta=True, name="in_proj_dw", tm=512,
                            comm=_pair_exchange_comm([grads4[k] for k in mid]))
    for k, t in zip(mid, theirs):
        low[k], own[k] = _pair_add(grads4[k], t, "pair_add_" + k)
    da, got_mid = _matmul(dproj4, full["w_in"], tb=True, reduce_b=True, name="in_proj_dx",
                          comm=_chip_exchange_comm([low[k] for k in mid]))
    got.update(zip(mid, got_mid))
    dx, d_attn_pre = _norm_bwd(xs, attn_pre_g, [da], [dh1], F32, "attn_pre_norm_bwd")

    small_grads = dict(attn_pre_g=d_attn_pre, q_norm_g=d_q_norm, kv_norm_g=d_kv_norm, swa_sinks=d_sinks, grp_a_g=d_grp_a,
                       grp_b_g=d_grp_b, attn_post_g=d_attn_post, ffn_pre_g=d_ffn_pre, ffn_post_g=d_ffn_post)
    parts = [loss_row] + [_pad_lanes(small_grads[k]) for k in SMALL]
    packed = jnp.concatenate(parts, axis=1)
    n_packed = packed.shape[1]
    packed = jnp.pad(packed, ((0, 0), (0, -n_packed % (8 * LANES)))).reshape(8, -1)
    total8 = _all_sum_small(packed)
    total = total8.reshape(1, -1)
    loss = total[0, 0]
    g_small, off = {}, LANES
    for k in SMALL:
        n = gains[k].shape[1]
        g_small[k] = total[:, off:off + n]
        off += n + (-n % LANES)

    def pack_small(prefix):
        flat = jnp.concatenate([jnp.zeros((1, LANES), F32)] + [_pad_lanes(given[prefix + k]) for k in SMALL], axis=1)
        return jnp.pad(flat, ((0, 0), (0, -n_packed % (8 * LANES)))).reshape(8, -1)

    d_sm, m_sm, v_sm = (r.reshape(1, -1) for r in
                        _adamw(pack_small(""), total8, pack_small("m_"), pack_small("v_"), "adamw_small"))
    delta, new_m, new_v, off = {}, {}, {}, LANES
    for k in SMALL:
        n = gains[k].shape[1]
        delta[k], new_m[k], new_v[k] = d_sm[:, off:off + n], m_sm[:, off:off + n], v_sm[:, off:off + n]
        off += n + (-n % LANES)

    grads4["w_in"] = dw_in
    (theirs_in,) = _run_comm(_pair_exchange_comm([dw_in]), "grad_pair_exchange")
    low["w_in"], own["w_in"] = _pair_add(dw_in, theirs_in, "pair_add_w_in")
    (got["w_in"],) = _run_comm(_chip_exchange_comm([low["w_in"]]), "grad_chip_exchange")
    halves = [_chip_add(own[k], got[k], "chip_add_" + k) for k in BIG]
    g_big = dict(zip(BIG, _run_comm(_pair_share_comm(halves), "grad_pair_share")))
    for k in BIG:
        delta[k], new_m[k], new_v[k], g_big[k] = _adamw(
            w32[k], g_big[k], given["m_" + k], given["v_" + k], "adamw_" + k, emit_grad=True)

    grads = {**g_small, **g_big}
    return (loss, dx[None], *[grads[k] for k in ORDER], *[delta[k] for k in ORDER], *[new_m[k] for k in ORDER],
            *[new_v[k] for k in ORDER])
```

```python
import functools
import math

import jax
import jax.numpy as jnp
from jax import lax
from jax.experimental import pallas as pl
from jax.experimental.pallas import tpu as pltpu

F32, BF16 = jnp.float32, jnp.bfloat16
MESH = pl.DeviceIdType.MESH
ANY = pl.BlockSpec(memory_space=pl.ANY)

N_CHIPS = 4
EPS = 1e-6
MLA_HEADS, MLA_NOPE, MLA_ROPE, MLA_V = 16, 128, 64, 128
MLA_QK = MLA_NOPE + MLA_ROPE
HEADS_PER_CHIP = MLA_HEADS // N_CHIPS
ROPE_THETA = 10000.0
SWA_Q_HEADS, SWA_KV_HEADS, SWA_HEAD_DIM, SWA_WINDOW = 32, 8, 64, 128
SWA_GROUP = SWA_Q_HEADS // SWA_KV_HEADS
MIX_A, MIX_B = MLA_HEADS * MLA_V, SWA_Q_HEADS * SWA_HEAD_DIM
MASK_VALUE = float(jnp.finfo(jnp.float32).min)
ADAM_LR, ADAM_B1, ADAM_B2, ADAM_EPS, ADAM_WD, ADAM_STEP = 0.001, 0.9, 0.999, 1e-08, 0.01, 10

LANES = 128
VMEM_LIMIT = 56 << 20
ATTN_TILE = 512
ROW_TILE = 256
MM_TILE = 1024


def _fit(dim, pref, mult=LANES):
    if dim <= pref:
        return dim
    for t in range(pref - pref % mult, 0, -mult):
        if dim % t == 0:
            return t
    return dim


def _params(*semantics):
    return pltpu.CompilerParams(dimension_semantics=semantics, vmem_limit_bytes=VMEM_LIMIT)


def _dot(a, b, ca, cb):
    return lax.dot_general(a, b, (((ca,), (cb,)), ((), ())), preferred_element_type=F32)


def _matmul(a, b, *, name, ta=False, tb=False, reduce_b=False, out_dtype=F32, tm=MM_TILE, tn=MM_TILE, tk=MM_TILE, comm=None,
            epilogue=None, extras=()):
    a3, b3 = a.ndim == 3, b.ndim == 3
    nb = a.shape[0] if a3 else (b.shape[0] if b3 else 1)
    (K, M) = a.shape[-2:] if ta else a.shape[-2:][::-1]
    (N, K2) = b.shape[-2:] if tb else b.shape[-2:][::-1]
    assert K == K2, (a.shape, b.shape)
    tm, tn, tk = _fit(M, tm), _fit(N, tn), _fit(K, tk)
    batched_out = (a3 or b3) and not reduce_b
    n_bo = nb if batched_out else 1
    n_br = nb if reduce_b else 1
    nk = K // tk

    def sel(bo, br):
        return br if reduce_b else bo

    def a_map(bo, i, j, br, k):
        t = (k, i) if ta else (i, k)
        return (sel(bo, br),) + t if a3 else t

    def b_map(bo, i, j, br, k):
        t = (j, k) if tb else (k, j)
        return (sel(bo, br),) + t if b3 else t

    def o_map(bo, i, j, br, k):
        return (bo, i, j) if batched_out else (i, j)

    a_blk = (tk, tm) if ta else (tm, tk)
    b_blk = (tn, tk) if tb else (tk, tn)
    a_blk = (None,) + a_blk if a3 else a_blk
    b_blk = (None,) + b_blk if b3 else b_blk
    o_blk = (None, tm, tn) if batched_out else (tm, tn)
    o_shape = (nb, M, N) if batched_out else (M, N)

    grid = (n_bo, M // tm, N // tn, n_br, nk)
    n_ci, n_co = (len(comm.ins), len(comm.out_shapes)) if comm else (0, 0)
    n_x = len(extras)
    out_dtypes = tuple(out_dtype) if epilogue else (out_dtype,)
    n_o = len(out_dtypes)

    def body(*refs):
        a_ref, b_ref = refs[:2]
        x_refs, c_in = refs[2:2 + n_x], refs[2 + n_x:2 + n_x + n_ci]
        refs = refs[2 + n_x + n_ci:]
        o_refs, c_out, acc_ref, sems = refs[:n_o], refs[n_o:n_o + n_co], refs[n_o + n_co], refs[n_o + n_co + 1:]
        br, k = pl.program_id(3), pl.program_id(4)
        first, last = _grid_ends(grid)
        if comm:
            pl.when(first)(lambda: comm.start(c_in, c_out, sems))

        @pl.when((br == 0) & (k == 0))
        def _():
            acc_ref[...] = jnp.zeros_like(acc_ref)

        acc_ref[...] += _dot(a_ref[...], b_ref[...], 0 if ta else 1, 1 if tb else 0)

        @pl.when((br == n_br - 1) & (k == nk - 1))
        def _():
            vals = epilogue(acc_ref[...], *[r[...] for r in x_refs]) if epilogue else (acc_ref[...],)
            for o_ref, v in zip(o_refs, vals):
                o_ref[...] = v.astype(o_ref.dtype)

        if comm:
            pl.when(last)(lambda: comm.finish(c_in, c_out, sems))

    o_spec = pl.BlockSpec(o_blk, o_map)
    res = pl.pallas_call(
        body, name=name,
        out_shape=tuple(jax.ShapeDtypeStruct(o_shape, d) for d in out_dtypes) + tuple(comm.out_shapes if comm else ()),
        grid=grid,
        in_specs=[pl.BlockSpec(a_blk, a_map), pl.BlockSpec(b_blk, b_map)] + [o_spec] * n_x + [ANY] * n_ci,
        out_specs=(o_spec,) * n_o + (ANY,) * n_co,
        input_output_aliases={2 + n_x + i: n_o + o for i, o in comm.aliases.items()} if comm else {},
        scratch_shapes=[pltpu.VMEM((tm, tn), F32)] + (comm.scratch if comm else []),
        compiler_params=_params(*(["parallel"] * 3 + ["arbitrary"] * 2 if not comm else ["arbitrary"] * 5)),
    )(a, b, *extras, *(comm.ins if comm else ()))
    main = res[:n_o] if epilogue else res[0]
    return (main, res[n_o:]) if comm else main


def _inv_rms(u):
    return lax.rsqrt(jnp.mean(u * u, axis=-1, keepdims=True) + EPS)


def _norm_bwd_math(u, g, dz):
    r = _inv_rms(u)
    w = dz * g
    du = r * w - u * (r * r * r * jnp.mean(w * u, axis=-1, keepdims=True))
    dg = jnp.sum(dz * (u * r), axis=0, keepdims=True)
    return du, dg


def _row_spec(tr, n):
    return pl.BlockSpec((tr, n), lambda i: (i, 0))


def _gain_spec(n):
    return pl.BlockSpec((1, n), lambda i: (0, 0))


def _norm_fwd(u, g, out_dtype, name):
    rows, n = u.shape
    tr = _fit(rows, ROW_TILE, 16)

    def body(u_ref, g_ref, o_ref):
        x = u_ref[...]
        o_ref[...] = (x * _inv_rms(x) * g_ref[...]).astype(o_ref.dtype)

    return pl.pallas_call(
        body, name=name, out_shape=jax.ShapeDtypeStruct((rows, n), out_dtype), grid=(rows // tr,),
        in_specs=[_row_spec(tr, n), _gain_spec(n)], out_specs=_row_spec(tr, n), compiler_params=_params("parallel"),
    )(u, g)


def _norm_bwd(u, g, dzs, adds, out_dtype, name, comm=None):
    rows, n = u.shape
    tr = _fit(rows, ROW_TILE, 16)
    n_dz, n_add = len(dzs), len(adds)
    grid = (rows // tr,)

    def body(ins, outs):
        u_ref, g_ref = ins[:2]
        dz_refs, add_refs = ins[2:2 + n_dz], ins[2 + n_dz:]
        du_ref, dg_ref = outs
        dz = dz_refs[0][...].astype(F32)
        for r in dz_refs[1:]:
            dz = dz + r[...].astype(F32)
        du, dg = _norm_bwd_math(u_ref[...], g_ref[...], dz)
        for r in add_refs:
            du = du + r[...]
        du_ref[...] = du.astype(du_ref.dtype)

        @pl.when(pl.program_id(0) == 0)
        def _():
            dg_ref[...] = jnp.zeros_like(dg_ref)

        dg_ref[...] += dg

    full, extra = _carry(body, comm, 2 + n_dz + n_add, 2, grid)
    res = pl.pallas_call(
        full, name=name,
        out_shape=(jax.ShapeDtypeStruct((rows, n), out_dtype), jax.ShapeDtypeStruct((1, n), F32)) + extra["out_shape"],
        grid=grid,
        in_specs=[_row_spec(tr, n), _gain_spec(n)] + [_row_spec(tr, n)] * (n_dz + n_add) + extra["in_specs"],
        out_specs=(_row_spec(tr, n), _gain_spec(n)) + extra["out_specs"],
        input_output_aliases=extra["aliases"], scratch_shapes=extra["scratch"], compiler_params=_params("arbitrary"),
    )(u, g, *dzs, *adds, *extra["operands"])
    return (res[0], res[1], res[2:]) if comm else (res[0], res[1])


def _mix_fwd(o_a, o_b, g_a, g_b):
    rows = o_a.shape[0]
    tr = _fit(rows, ROW_TILE, 16)

    def body(a_ref, b_ref, ga_ref, gb_ref, o_ref):
        a, b = a_ref[...], b_ref[...]
        o_ref[:, :MIX_A] = (a * _inv_rms(a) * ga_ref[...]).astype(BF16)
        o_ref[:, MIX_A:] = (b * _inv_rms(b) * gb_ref[...]).astype(BF16)

    return pl.pallas_call(
        body, name="mix_fwd", out_shape=jax.ShapeDtypeStruct((rows, MIX_A + MIX_B), BF16), grid=(rows // tr,),
        in_specs=[_row_spec(tr, MIX_A), _row_spec(tr, MIX_B), _gain_spec(MIX_A), _gain_spec(MIX_B)],
        out_specs=_row_spec(tr, MIX_A + MIX_B), compiler_params=_params("parallel"),
    )(o_a, o_b, g_a, g_b)


def _mix_bwd(o_a, o_b, g_a, g_b, dmix):
    rows = o_a.shape[0]
    tr = _fit(rows, ROW_TILE, 16)

    def body(a_ref, b_ref, ga_ref, gb_ref, dm_ref, doa_ref, dob_ref, dga_ref, dgb_ref, dla_ref, dlb_ref):
        a, b = a_ref[...], b_ref[...]
        doa, dga = _norm_bwd_math(a, ga_ref[...], dm_ref[:, :MIX_A])
        dob, dgb = _norm_bwd_math(b, gb_ref[...], dm_ref[:, MIX_A:])
        doa_ref[...] = doa.astype(BF16)
        dob_ref[...] = dob.astype(BF16)
        pa, pb = doa * a, dob * b
        for h in range(MLA_HEADS):
            dla_ref[:, h:h + 1] = jnp.sum(pa[:, h * MLA_V:(h + 1) * MLA_V], axis=-1, keepdims=True)
        for h in range(SWA_Q_HEADS):
            dlb_ref[:, h:h + 1] = jnp.sum(pb[:, h * SWA_HEAD_DIM:(h + 1) * SWA_HEAD_DIM], axis=-1, keepdims=True)

        @pl.when(pl.program_id(0) == 0)
        def _():
            dga_ref[...] = jnp.zeros_like(dga_ref)
            dgb_ref[...] = jnp.zeros_like(dgb_ref)

        dga_ref[...] += dga
        dgb_ref[...] += dgb

    return pl.pallas_call(
        body, name="mix_bwd",
        out_shape=(jax.ShapeDtypeStruct((rows, MIX_A), BF16), jax.ShapeDtypeStruct((rows, MIX_B), BF16),
                   jax.ShapeDtypeStruct((1, MIX_A), F32), jax.ShapeDtypeStruct((1, MIX_B), F32),
                   jax.ShapeDtypeStruct((rows, MLA_HEADS), F32), jax.ShapeDtypeStruct((rows, SWA_Q_HEADS), F32)),
        grid=(rows // tr,),
        in_specs=[_row_spec(tr, MIX_A), _row_spec(tr, MIX_B), _gain_spec(MIX_A), _gain_spec(MIX_B),
                  _row_spec(tr, MIX_A + MIX_B)],
        out_specs=(_row_spec(tr, MIX_A), _row_spec(tr, MIX_B), _gain_spec(MIX_A), _gain_spec(MIX_B),
                   _row_spec(tr, MLA_HEADS), _row_spec(tr, SWA_Q_HEADS)),
        compiler_params=_params("arbitrary"),
    )(o_a, o_b, g_a, g_b, dmix)


def _post_pre_fwd(x, ao, g_post, g_pre):
    rows, n = x.shape
    tr = _fit(rows, ROW_TILE, 16)

    def body(x_ref, ao_ref, g1_ref, g2_ref, h_ref, f_ref):
        u = ao_ref[...]
        h = x_ref[...] + u * _inv_rms(u) * g1_ref[...]
        h_ref[...] = h
        f_ref[...] = (h * _inv_rms(h) * g2_ref[...]).astype(BF16)

    return pl.pallas_call(
        body, name="post_pre_fwd",
        out_shape=(jax.ShapeDtypeStruct((rows, n), F32), jax.ShapeDtypeStruct((rows, n), BF16)), grid=(rows // tr,),
        in_specs=[_row_spec(tr, n), _row_spec(tr, n), _gain_spec(n), _gain_spec(n)],
        out_specs=(_row_spec(tr, n), _row_spec(tr, n)), compiler_params=_params("parallel"),
    )(x, ao, g_post, g_pre)


def _loss_bwd(h1, dn, g_post, target):
    rows, n = h1.shape
    tr = _fit(rows, ROW_TILE, 16)

    def body(h_ref, u_ref, g_ref, t_ref, loss_ref, dy_ref, du_ref, dg_ref):
        u, g = u_ref[...], g_ref[...]
        err = h_ref[...] + u * _inv_rms(u) * g - t_ref[...]
        dy = err / n
        dy_ref[...] = dy
        du, dg = _norm_bwd_math(u, g, dy)
        du_ref[...] = du.astype(BF16)

        @pl.when(pl.program_id(0) == 0)
        def _():
            loss_ref[...] = jnp.zeros_like(loss_ref)
            dg_ref[...] = jnp.zeros_like(dg_ref)

        loss_ref[...] += jnp.full((1, LANES), 0.5 * jnp.sum(jnp.mean(err * err, axis=-1)), F32)
        dg_ref[...] += dg

    return pl.pallas_call(
        body, name="loss_bwd",
        out_shape=(jax.ShapeDtypeStruct((1, LANES), F32), jax.ShapeDtypeStruct((rows, n), F32),
                   jax.ShapeDtypeStruct((rows, n), BF16), jax.ShapeDtypeStruct((1, n), F32)),
        grid=(rows // tr,),
        in_specs=[_row_spec(tr, n), _row_spec(tr, n), _gain_spec(n), _row_spec(tr, n)],
        out_specs=(_gain_spec(LANES), _row_spec(tr, n), _row_spec(tr, n), _gain_spec(n)),
        compiler_params=_params("arbitrary"),
    )(h1, dn, g_post, target)


def _blk3_spec(tr, n):
    return pl.BlockSpec((None, tr, n), lambda b, i: (b, i, 0))


def _swiglu_fwd_tile(up, gate):
    return up, gate * jax.nn.sigmoid(gate) * up


def _swiglu_bwd_tile(dact, gate, up):
    sig = jax.nn.sigmoid(gate)
    return dact * up * (sig * (1.0 + gate * (1.0 - sig))), dact * (gate * sig)


def _rope(x, cos, sin):
    half = MLA_ROPE // 2
    x1, x2 = x[:, :half], x[:, half:]
    return jnp.concatenate([x1 * cos - x2 * sin, x2 * cos + x1 * sin], axis=-1)


def _rope_t(d, cos, sin):
    half = MLA_ROPE // 2
    d1, d2 = d[:, :half], d[:, half:]
    return jnp.concatenate([d1 * cos + d2 * sin, d2 * cos - d1 * sin], axis=-1)


def _mla_prep(q4, kv4, k_rope, cos, sin):
    nb, rows, _ = q4.shape
    tr = _fit(rows, ROW_TILE, 16)
    hpc, half = HEADS_PER_CHIP, MLA_ROPE // 2

    def body(q_ref, kv_ref, kr_ref, c_ref, s_ref, qo_ref, ko_ref, vo_ref):
        cos, sin = c_ref[...], s_ref[...]
        k_pe = _rope(kr_ref[...], cos, sin)
        q_all, kv_all = q_ref[...], kv_ref[...]
        for h in range(hpc):
            q = q_all[:, h * MLA_QK:(h + 1) * MLA_QK]
            qo_ref[h] = jnp.concatenate([q[:, :MLA_NOPE], _rope(q[:, MLA_NOPE:], cos, sin)], axis=-1).astype(BF16)
            kv = kv_all[:, h * (MLA_NOPE + MLA_V):(h + 1) * (MLA_NOPE + MLA_V)]
            ko_ref[h] = jnp.concatenate([kv[:, :MLA_NOPE], k_pe], axis=-1).astype(BF16)
            vo_ref[h] = kv[:, MLA_NOPE:].astype(BF16)

    def head_spec(n):
        return pl.BlockSpec((hpc, tr, n), lambda b, i: (b, i, 0))

    def row_spec(n):
        return pl.BlockSpec((tr, n), lambda b, i: (i, 0))

    return pl.pallas_call(
        body, name="mla_prep",
        out_shape=(jax.ShapeDtypeStruct((MLA_HEADS, rows, MLA_QK), BF16), jax.ShapeDtypeStruct((MLA_HEADS, rows, MLA_QK), BF16),
                   jax.ShapeDtypeStruct((MLA_HEADS, rows, MLA_V), BF16)),
        grid=(nb, rows // tr),
        in_specs=[_blk3_spec(tr, hpc * MLA_QK), _blk3_spec(tr, hpc * (MLA_NOPE + MLA_V)), row_spec(MLA_ROPE),
                  row_spec(half), row_spec(half)],
        out_specs=(head_spec(MLA_QK), head_spec(MLA_QK), head_spec(MLA_V)),
        compiler_params=_params("parallel", "parallel"),
    )(q4, kv4, k_rope, cos, sin)


def _mla_unprep(dq, dk, dv, cos, sin):
    _, rows, _ = dq.shape
    tr = _fit(rows, ROW_TILE, 16)
    hpc, half = HEADS_PER_CHIP, MLA_ROPE // 2

    def body(dq_ref, dk_ref, dv_ref, c_ref, s_ref, q4_ref, kv4_ref, kr_ref):
        cos, sin = c_ref[...], s_ref[...]
        d_pe = jnp.zeros((tr, MLA_ROPE), F32)
        q_parts, kv_parts = [], []
        for h in range(hpc):
            g, gk = dq_ref[h], dk_ref[h]
            q_parts += [g[:, :MLA_NOPE], _rope_t(g[:, MLA_NOPE:], cos, sin)]
            kv_parts += [gk[:, :MLA_NOPE], dv_ref[h]]
            d_pe = d_pe + gk[:, MLA_NOPE:]
        q4_ref[...] = jnp.concatenate(q_parts, axis=-1).astype(BF16)
        kv4_ref[...] = jnp.concatenate(kv_parts, axis=-1).astype(BF16)

        @pl.when(pl.program_id(1) == 0)
        def _():
            kr_ref[...] = jnp.zeros_like(kr_ref)

        kr_ref[...] += _rope_t(d_pe, cos, sin)

    def head_spec(n):
        return pl.BlockSpec((hpc, tr, n), lambda i, b: (b, i, 0))

    def row_spec(n):
        return pl.BlockSpec((tr, n), lambda i, b: (i, 0))

    def blk_spec(n):
        return pl.BlockSpec((None, tr, n), lambda i, b: (b, i, 0))

    return pl.pallas_call(
        body, name="mla_unprep",
        out_shape=(jax.ShapeDtypeStruct((N_CHIPS, rows, hpc * MLA_QK), BF16),
                   jax.ShapeDtypeStruct((N_CHIPS, rows, hpc * (MLA_NOPE + MLA_V)), BF16),
                   jax.ShapeDtypeStruct((rows, MLA_ROPE), F32)),
        grid=(rows // tr, N_CHIPS),
        in_specs=[head_spec(MLA_QK), head_spec(MLA_QK), head_spec(MLA_V), row_spec(half), row_spec(half)],
        out_specs=(blk_spec(hpc * MLA_QK), blk_spec(hpc * (MLA_NOPE + MLA_V)), row_spec(MLA_ROPE)),
        compiler_params=_params("parallel", "arbitrary"),
    )(dq, dk, dv, cos, sin)


def _carry(body, comm, n_in, n_out, grid):
    n_ci, n_co = (len(comm.ins), len(comm.out_shapes)) if comm else (0, 0)

    def full(*refs):
        ins, c_in = refs[:n_in], refs[n_in:n_in + n_ci]
        outs = refs[n_in + n_ci:n_in + n_ci + n_out]
        c_out = refs[n_in + n_ci + n_out:n_in + n_ci + n_out + n_co]
        sems = refs[n_in + n_ci + n_out + n_co:]
        first, last = _grid_ends(grid)
        if comm:
            pl.when(first)(lambda: comm.start(c_in, c_out, sems))
        body(ins, outs)
        if comm:
            pl.when(last)(lambda: comm.finish(c_in, c_out, sems))

    extra = dict(
        operands=list(comm.ins) if comm else [], in_specs=[ANY] * n_ci, out_specs=(ANY,) * n_co,
        out_shape=tuple(comm.out_shapes) if comm else (),
        aliases={n_in + i: n_out + o for i, o in comm.aliases.items()} if comm else {},
        scratch=comm.scratch if comm else [])
    return full, extra


def _causal_keep(t):
    return lax.broadcasted_iota(jnp.int32, (t, t), 1) <= lax.broadcasted_iota(jnp.int32, (t, t), 0)


def _mla_fwd(q, k, v, comm=None):
    nh, rows, _ = q.shape
    t = _fit(rows, ATTN_TILE)
    scale = MLA_QK ** -0.5
    grid = (nh, rows // t)

    def body(ins, outs):
        (q_ref, k_ref, v_ref), (o_ref, lse_ref) = ins, outs
        i = pl.program_id(1)
        qb = q_ref[...]

        def step(j, carry, diagonal):
            m, l, acc = carry
            rows_j = pl.ds(pl.multiple_of(j * t, t), t)
            s = _dot(qb, k_ref[rows_j, :], 1, 1) * scale
            if diagonal:
                s = jnp.where(_causal_keep(t), s, MASK_VALUE)
            m_new = jnp.maximum(m, jnp.max(s, axis=-1, keepdims=True))
            alpha = jnp.exp(m - m_new)
            p = jnp.exp(s - m_new)
            l = alpha * l + jnp.sum(p, axis=-1, keepdims=True)
            acc = alpha * acc + _dot(p.astype(BF16), v_ref[rows_j, :], 1, 0)
            return m_new, l, acc

        init = (jnp.full((t, 1), MASK_VALUE, F32), jnp.zeros((t, 1), F32), jnp.zeros((t, MLA_V), F32))
        carry = lax.fori_loop(0, i, lambda j, c: step(j, c, False), init)
        m, l, acc = step(i, carry, True)
        o_ref[...] = acc / l
        lse_ref[...] = m + jnp.log(l)

    full, extra = _carry(body, comm, 3, 2, grid)
    res = pl.pallas_call(
        full, name="mla_fwd",
        out_shape=(jax.ShapeDtypeStruct((rows, nh * MLA_V), F32), jax.ShapeDtypeStruct((nh, rows, 1), F32)) + extra["out_shape"],
        grid=grid,
        in_specs=[pl.BlockSpec((None, t, MLA_QK), lambda h, i: (h, i, 0)),
                  pl.BlockSpec((None, rows, MLA_QK), lambda h, i: (h, 0, 0)),
                  pl.BlockSpec((None, rows, MLA_V), lambda h, i: (h, 0, 0))] + extra["in_specs"],
        out_specs=(pl.BlockSpec((t, MLA_V), lambda h, i: (i, h)), pl.BlockSpec((None, t, 1), lambda h, i: (h, i, 0))) + extra["out_specs"],
        input_output_aliases=extra["aliases"], scratch_shapes=extra["scratch"],
        compiler_params=_params("arbitrary", "arbitrary"),
    )(q, k, v, *extra["operands"])
    return res[0], res[1], res[2:]


def _mla_bwd(q, k, v, do, lse, delta, comm=None):
    nh, rows, _ = q.shape
    t = _fit(rows, ATTN_TILE)
    n_t = rows // t
    scale = MLA_QK ** -0.5
    grid = (nh, n_t)

    def body(ins, outs):
        (q_ref, k_ref, v_ref, do_ref, lse_ref, dl_ref), (dq_ref, dk_ref, dv_ref) = ins, outs
        j = pl.program_id(1)
        kb, vb = k_ref[...], v_ref[...]

        @pl.when(j == 0)
        def _():
            dq_ref[...] = jnp.zeros_like(dq_ref)

        def step(i, carry, diagonal):
            dk, dv = carry
            rows_i = pl.ds(pl.multiple_of(i * t, t), t)
            qb, dob = q_ref[rows_i, :], do_ref[rows_i, :]
            s = _dot(qb, kb, 1, 1) * scale
            if diagonal:
                s = jnp.where(_causal_keep(t), s, MASK_VALUE)
            p = jnp.exp(s - lse_ref[rows_i, :])
            dv = dv + _dot(p.astype(BF16), dob, 0, 0)
            dp = _dot(dob, vb, 1, 1)
            ds = (p * (dp - dl_ref[rows_i, :]) * scale).astype(BF16)
            dk = dk + _dot(ds, qb, 0, 0)
            dq_ref[rows_i, :] += _dot(ds, kb, 1, 0)
            return dk, dv

        carry = step(j, (jnp.zeros((t, MLA_QK), F32), jnp.zeros((t, MLA_V), F32)), True)
        dk, dv = lax.fori_loop(j + 1, n_t, lambda i, c: step(i, c, False), carry)
        dk_ref[...] = dk
        dv_ref[...] = dv

    def head(n):
        return pl.BlockSpec((None, rows, n), lambda h, j: (h, 0, 0))

    def tile(n):
        return pl.BlockSpec((None, t, n), lambda h, j: (h, j, 0))

    full, extra = _carry(body, comm, 6, 3, grid)
    res = pl.pallas_call(
        full, name="mla_bwd",
        out_shape=(jax.ShapeDtypeStruct((nh, rows, MLA_QK), F32), jax.ShapeDtypeStruct((nh, rows, MLA_QK), F32),
                   jax.ShapeDtypeStruct((nh, rows, MLA_V), F32)) + extra["out_shape"],
        grid=grid,
        in_specs=[head(MLA_QK), tile(MLA_QK), tile(MLA_V), pl.BlockSpec((rows, MLA_V), lambda h, j: (0, h)), head(1), head(1)]
        + extra["in_specs"],
        out_specs=(head(MLA_QK), tile(MLA_QK), tile(MLA_V)) + extra["out_specs"],
        input_output_aliases=extra["aliases"], scratch_shapes=extra["scratch"],
        compiler_params=_params("arbitrary", "arbitrary"),
    )(q, k, v, do, lse, delta, *extra["operands"])
    return res[0], res[1], res[2], res[3:]


def _swa_slope(h):
    return 2.0 ** (-8.0 * (h + 1) / SWA_Q_HEADS)


def _swa_band_specs(rows):
    w = SWA_WINDOW
    kvw = SWA_KV_HEADS * SWA_HEAD_DIM

    def prev(i):
        return jnp.maximum(i - 1, 0)

    return [pl.BlockSpec((w, kvw), lambda i: (prev(i), 0)), pl.BlockSpec((w, kvw), lambda i: (i, 0)),
            pl.BlockSpec((w, kvw), lambda i: (prev(i), 0)), pl.BlockSpec((w, kvw), lambda i: (i, 0)),
            pl.BlockSpec((w, 1), lambda i: (i, 0)),
            pl.BlockSpec((1, w), lambda i: (0, prev(i))), pl.BlockSpec((1, w), lambda i: (0, i)),
            pl.BlockSpec(memory_space=pltpu.SMEM)]


def _swa_bias(i, pc_ref, pp_ref, pn_ref):
    w = SWA_WINDOW
    k_pos = jnp.concatenate([pp_ref[...], pn_ref[...]], axis=1)
    dist = jnp.abs(pc_ref[...] - k_pos).astype(F32)
    r = lax.broadcasted_iota(jnp.int32, (w, 2 * w), 0)
    col = lax.broadcasted_iota(jnp.int32, (w, 2 * w), 1)
    delta = r + w - col
    valid = (delta >= 0) & (delta < w) & ((col >= w) | (i > 0))
    return dist, valid


def _swa_fwd(q, k, v, pos_col, pos_row, sinks):
    rows = q.shape[0]
    w, hd = SWA_WINDOW, SWA_HEAD_DIM
    scale = hd ** -0.5

    def body(q_ref, kp_ref, kc_ref, vp_ref, vc_ref, pc_ref, pp_ref, pn_ref, sink_ref, o_ref, lse_ref):
        dist, valid = _swa_bias(pl.program_id(0), pc_ref, pp_ref, pn_ref)
        for kvh in range(SWA_KV_HEADS):
            cols = slice(kvh * hd, (kvh + 1) * hd)
            kb = jnp.concatenate([kp_ref[:, cols], kc_ref[:, cols]], axis=0)
            vb = jnp.concatenate([vp_ref[:, cols], vc_ref[:, cols]], axis=0)
            for g in range(SWA_GROUP):
                h = kvh * SWA_GROUP + g
                sink = sink_ref[h]
                s = _dot(q_ref[:, h * hd:(h + 1) * hd], kb, 1, 1) * scale - _swa_slope(h) * dist
                s = jnp.where(valid, s, MASK_VALUE)
                m = jnp.maximum(jnp.max(s, axis=-1, keepdims=True), sink)
                e = jnp.exp(s - m)
                den = jnp.sum(e, axis=-1, keepdims=True) + jnp.exp(sink - m)
                o_ref[:, h * hd:(h + 1) * hd] = _dot((e / den).astype(BF16), vb, 1, 0)
                lse_ref[:, h:h + 1] = m + jnp.log(den)

    return pl.pallas_call(
        body, name="swa_fwd",
        out_shape=(jax.ShapeDtypeStruct((rows, MIX_B), F32), jax.ShapeDtypeStruct((rows, SWA_Q_HEADS), F32)),
        grid=(rows // w,),
        in_specs=[pl.BlockSpec((w, MIX_B), lambda i: (i, 0))] + _swa_band_specs(rows),
        out_specs=(pl.BlockSpec((w, MIX_B), lambda i: (i, 0)), pl.BlockSpec((w, SWA_Q_HEADS), lambda i: (i, 0))),
        compiler_params=_params("parallel"),
    )(q, k, k, v, v, pos_col, pos_row, pos_row, sinks)


def _swa_bwd(q, k, v, pos_col, pos_row, sinks, do, lse, delta):
    rows = q.shape[0]
    w, hd = SWA_WINDOW, SWA_HEAD_DIM
    kvw = SWA_KV_HEADS * hd
    scale = hd ** -0.5

    def body(q_ref, kp_ref, kc_ref, vp_ref, vc_ref, pc_ref, pp_ref, pn_ref, sink_ref, do_ref, lse_ref, dl_ref,
             dq_ref, dkc_ref, dkp_ref, dvc_ref, dvp_ref, dsink_ref):
        dist, valid = _swa_bias(pl.program_id(0), pc_ref, pp_ref, pn_ref)

        @pl.when(pl.program_id(0) == 0)
        def _():
            dsink_ref[...] = jnp.zeros_like(dsink_ref)

        for kvh in range(SWA_KV_HEADS):
            cols = slice(kvh * hd, (kvh + 1) * hd)
            kb = jnp.concatenate([kp_ref[:, cols], kc_ref[:, cols]], axis=0)
            vb = jnp.concatenate([vp_ref[:, cols], vc_ref[:, cols]], axis=0)
            dk = jnp.zeros((2 * w, hd), F32)
            dv = jnp.zeros((2 * w, hd), F32)
            for g in range(SWA_GROUP):
                h = kvh * SWA_GROUP + g
                hc = slice(h * hd, (h + 1) * hd)
                qb, dob = q_ref[:, hc], do_ref[:, hc]
                lse_h, dl_h = lse_ref[:, h:h + 1], dl_ref[:, h:h + 1]
                s = _dot(qb, kb, 1, 1) * scale - _swa_slope(h) * dist
                s = jnp.where(valid, s, MASK_VALUE)
                p = jnp.exp(s - lse_h)
                ds = (p * (_dot(dob, vb, 1, 1) - dl_h) * scale).astype(BF16)
                dq_ref[:, hc] = _dot(ds, kb, 1, 0).astype(BF16)
                dk = dk + _dot(ds, qb, 0, 0)
                dv = dv + _dot(p.astype(BF16), dob, 0, 0)
                dsink_ref[:, h:h + 1] += -jnp.sum(jnp.exp(sink_ref[h] - lse_h) * dl_h, axis=0, keepdims=True)
            dkp_ref[:, cols] = dk[:w]
            dkc_ref[:, cols] = dk[w:]
            dvp_ref[:, cols] = dv[:w]
            dvc_ref[:, cols] = dv[w:]

    def blk(n):
        return pl.BlockSpec((w, n), lambda i: (i, 0))

    return pl.pallas_call(
        body, name="swa_bwd",
        out_shape=(jax.ShapeDtypeStruct((rows, MIX_B), BF16),) + (jax.ShapeDtypeStruct((rows, kvw), F32),) * 4
        + (jax.ShapeDtypeStruct((1, SWA_Q_HEADS), F32),),
        grid=(rows // w,),
        in_specs=[blk(MIX_B)] + _swa_band_specs(rows) + [blk(MIX_B), blk(SWA_Q_HEADS), blk(SWA_Q_HEADS)],
        out_specs=(blk(MIX_B), blk(kvw), blk(kvw), blk(kvw), blk(kvw), pl.BlockSpec((1, SWA_Q_HEADS), lambda i: (0, 0))),
        compiler_params=_params("arbitrary"),
    )(q, k, k, v, v, pos_col, pos_row, pos_row, sinks, do, lse, delta)


def _band_merge(cur, prev, name):
    rows, n = cur.shape
    w = SWA_WINDOW
    last = rows // w - 1

    def body(c_ref, p_ref, o_ref):
        nxt = jnp.where(pl.program_id(0) < last, p_ref[...], 0.0)
        o_ref[...] = (c_ref[...] + nxt).astype(BF16)

    return pl.pallas_call(
        body, name=name, out_shape=jax.ShapeDtypeStruct((rows, n), BF16), grid=(rows // w,),
        in_specs=[pl.BlockSpec((w, n), lambda j: (j, 0)), pl.BlockSpec((w, n), lambda j: (jnp.minimum(j + 1, last), 0))],
        out_specs=pl.BlockSpec((w, n), lambda j: (j, 0)), compiler_params=_params("parallel"),
    )(cur, prev)


def _adamw(w, g, m, v, name, emit_grad=False, comm=None):
    rows, cols = w.shape
    tr, tc = _fit(rows, 352, 8), _fit(cols, MM_TILE)
    c1 = 1.0 - ADAM_B1 ** ADAM_STEP
    c2 = 1.0 - ADAM_B2 ** ADAM_STEP
    n_out = 4 if emit_grad else 3
    grid = (rows // tr, cols // tc)

    def body(ins, outs):
        w_ref, g_ref, m_ref, v_ref = ins
        gr = g_ref[...]
        m_new = ADAM_B1 * m_ref[...] + (1.0 - ADAM_B1) * gr
        v_new = ADAM_B2 * v_ref[...] + (1.0 - ADAM_B2) * jnp.square(gr)
        outs[1][...] = m_new
        outs[2][...] = v_new
        outs[0][...] = -ADAM_LR * ((m_new / c1) / (jnp.sqrt(v_new / c2) + ADAM_EPS) + ADAM_WD * w_ref[...])
        if emit_grad:
            outs[3][...] = gr

    spec = pl.BlockSpec((tr, tc), lambda i, j: (i, j))
    full, extra = _carry(body, comm, 4, n_out, grid)
    res = pl.pallas_call(
        full, name=name, out_shape=(jax.ShapeDtypeStruct(w.shape, F32),) * n_out + extra["out_shape"], grid=grid,
        in_specs=[spec] * 4 + extra["in_specs"], out_specs=(spec,) * n_out + extra["out_specs"],
        input_output_aliases=extra["aliases"], scratch_shapes=extra["scratch"],
        compiler_params=_params("arbitrary", "arbitrary"),
    )(w, g, m, v, *extra["operands"])
    return res[:n_out] + ((res[n_out:],) if comm else ())


OTHER_CHIPS = ((1, 0), (0, 1), (1, 1))


def _place():
    x, y, c = lax.axis_index("x"), lax.axis_index("y"), lax.axis_index("c")
    return x, y, c


def _flip(v, f):
    return 1 - v if f else v


class _Comm:
    def __init__(self, ins, out_shapes, aliases, sem_sizes, start, finish):
        self.ins, self.out_shapes, self.aliases, self.sem_sizes = list(ins), list(out_shapes), dict(aliases), list(sem_sizes)
        self.start, self.finish = start, finish

    @property
    def scratch(self):
        return [pltpu.SemaphoreType.DMA((n,)) for n in self.sem_sizes]


def _join(first, second):
    n_i, n_o, n_s = len(first.ins), len(first.out_shapes), len(first.sem_sizes)

    def both(method):
        def run(ins, outs, sems):
            getattr(first, method)(ins[:n_i], outs[:n_o], sems[:n_s])
            getattr(second, method)(ins[n_i:], outs[n_o:], sems[n_s:])
        return run

    aliases = {**first.aliases, **{n_i + i: n_o + o for i, o in second.aliases.items()}}
    return _Comm(first.ins + second.ins, first.out_shapes + second.out_shapes, aliases, first.sem_sizes + second.sem_sizes,
                 both("start"), both("finish"))


def _run_comm(comm, name):
    n_in, n_out = len(comm.ins), len(comm.out_shapes)

    def body(*refs):
        ins, outs, sems = refs[:n_in], refs[n_in:n_in + n_out], refs[n_in + n_out:]
        comm.start(ins, outs, sems)
        comm.finish(ins, outs, sems)

    return pl.pallas_call(
        body, name=name, out_shape=tuple(comm.out_shapes), in_specs=[ANY] * n_in, out_specs=(ANY,) * n_out,
        input_output_aliases=comm.aliases, scratch_shapes=comm.scratch,
    )(*comm.ins)


def _grid_ends(grid):
    first = last = None
    for axis, n in enumerate(grid):
        pid = pl.program_id(axis)
        first = (pid == 0) if first is None else first & (pid == 0)
        last = (pid == n - 1) if last is None else last & (pid == n - 1)
    return first, last


def _cast_into_slot(w, name, comm=None):
    rows, cols = w.shape
    tr, tc = _fit(rows, 704, 16), _fit(cols, MM_TILE)
    grid = (rows // tr, cols // tc)

    def body(ins, outs):
        outs[0][...] = ins[0][...].astype(BF16)

    full, extra = _carry(body, comm, 1, 1, grid)
    res = pl.pallas_call(
        full, name=name, out_shape=(jax.ShapeDtypeStruct((N_CHIPS, rows, cols), BF16),) + extra["out_shape"], grid=grid,
        in_specs=[pl.BlockSpec((tr, tc), lambda i, j: (i, j))] + extra["in_specs"],
        out_specs=(pl.BlockSpec((None, tr, tc), lambda i, j: (2 * lax.axis_index("x") + lax.axis_index("y"), i, j)),)
        + extra["out_specs"],
        input_output_aliases=extra["aliases"], scratch_shapes=extra["scratch"],
        compiler_params=_params("arbitrary", "arbitrary"),
    )(w, *extra["operands"])
    return (res[0], res[1:]) if comm else res[0]


def _gather_comm(slots, js=(0, 1, 2)):
    n = len(slots)
    pairs = [(a, j) for a in range(n) for j in js]

    def copies(src, dst, sems):
        send, recv, fsend, frecv = sems
        x, y, c = _place()
        me = 2 * x + y

        def rows(a, core):
            half = slots[a].shape[1] // 2
            return pl.ds(core * half, half)

        def chip(j):
            return _flip(x, OTHER_CHIPS[j][0]), _flip(y, OTHER_CHIPS[j][1])

        def out(a, j):
            px, py = chip(j)
            return pltpu.make_async_remote_copy(
                src_ref=src[a].at[me, rows(a, c)], dst_ref=dst[a].at[me, rows(a, c)], send_sem=send.at[a * 3 + j],
                recv_sem=recv.at[a * 3 + j], device_id=(px, py, c), device_id_type=MESH)

        def landed(a, j):
            px, py = chip(j)
            blk = 2 * px + py
            return pltpu.make_async_remote_copy(
                src_ref=dst[a].at[blk, rows(a, c)], dst_ref=dst[a].at[blk, rows(a, c)], send_sem=send.at[a * 3 + j],
                recv_sem=recv.at[a * 3 + j], device_id=(x, y, c), device_id_type=MESH)

        def passed(a, j, core):
            px, py = chip(j)
            blk = 2 * px + py
            return pltpu.make_async_remote_copy(
                src_ref=dst[a].at[blk, rows(a, core)], dst_ref=dst[a].at[blk, rows(a, core)], send_sem=fsend.at[a * 3 + j],
                recv_sem=frecv.at[a * 3 + j], device_id=(x, y, 1 - c), device_id_type=MESH)

        return c, out, landed, passed

    def start(src, dst, sems):
        _, out, _, _ = copies(src, dst, sems)
        for a, j in pairs:
            out(a, j).start()

    def finish(src, dst, sems):
        c, out, landed, passed = copies(src, dst, sems)
        for a, j in pairs:
            landed(a, j).wait_recv()
            passed(a, j, c).start()
        for a, j in pairs:
            passed(a, j, 1 - c).wait_recv()
        for a, j in pairs:
            out(a, j).wait_send()
            passed(a, j, c).wait_send()

    shapes = [jax.ShapeDtypeStruct(s.shape, s.dtype) for s in slots]
    return _Comm(slots, shapes, {a: a for a in range(n)}, [3 * n] * 4, start, finish)


def _pair_exchange_comm(grads):
    n = len(grads)

    def copy(src, dst, sems, a):
        x, y, c = _place()
        half = grads[a].shape[1] // 2
        return pltpu.make_async_remote_copy(
            src_ref=src[a].at[:, pl.ds((1 - c) * half, half)], dst_ref=dst[a], send_sem=sems[0].at[a],
            recv_sem=sems[1].at[a], device_id=(x, y, 1 - c), device_id_type=MESH)

    def start(src, dst, sems):
        for a in range(n):
            copy(src, dst, sems, a).start()

    def finish(src, dst, sems):
        for a in range(n):
            copy(src, dst, sems, a).wait()

    shapes = [jax.ShapeDtypeStruct((g.shape[0], g.shape[1] // 2, g.shape[2]), g.dtype) for g in grads]
    return _Comm(grads, shapes, {}, [n, n], start, finish)


def _pair_add(mine, theirs, name):
    nb, rows, cols = theirs.shape
    tr, tc = _fit(rows, 704, 16), _fit(cols, MM_TILE)
    n_r = rows // tr

    def body(a_ref, b_ref, lo_ref, own_ref):
        s = a_ref[...] + b_ref[...]
        lo_ref[...] = s.astype(BF16)

        @pl.when(pl.program_id(2) == 2 * lax.axis_index("x") + lax.axis_index("y"))
        def _():
            own_ref[...] = s

    return pl.pallas_call(
        body, name=name, grid=(n_r, cols // tc, nb),
        in_specs=[pl.BlockSpec((None, tr, tc), lambda i, j, b: (b, lax.axis_index("c") * n_r + i, j)),
                  pl.BlockSpec((None, tr, tc), lambda i, j, b: (b, i, j))],
        out_specs=(pl.BlockSpec((None, tr, tc), lambda i, j, b: (b, i, j)), pl.BlockSpec((tr, tc), lambda i, j, b: (i, j))),
        out_shape=(jax.ShapeDtypeStruct(theirs.shape, BF16), jax.ShapeDtypeStruct((rows, cols), F32)),
        compiler_params=_params("parallel", "parallel", "arbitrary"),
    )(mine, theirs)


def _chip_exchange_comm(sums, js=(0, 1, 2), into=None):
    n = len(sums)
    pairs = [(a, j) for a in range(n) for j in js]

    def copy(src, dst, sems, a, j):
        x, y, c = _place()
        px, py = _flip(x, OTHER_CHIPS[j][0]), _flip(y, OTHER_CHIPS[j][1])
        return pltpu.make_async_remote_copy(
            src_ref=src[a].at[2 * px + py], dst_ref=dst[a].at[j], send_sem=sems[0].at[a * 3 + j],
            recv_sem=sems[1].at[a * 3 + j], device_id=(px, py, c), device_id_type=MESH)

    def start(src, dst, sems):
        for a, j in pairs:
            copy(src, dst, sems, a, j).start()

    def finish(src, dst, sems):
        for a, j in pairs:
            copy(src, dst, sems, a, j).wait()

    shapes = [jax.ShapeDtypeStruct((3,) + s.shape[1:], s.dtype) for s in sums]
    return _Comm(list(sums) + list(into or []), shapes, {n + a: a for a in range(n)} if into else {}, [3 * n, 3 * n],
                 start, finish)


def _chip_add(own, got, name):
    rows, cols = own.shape
    tr, tc = _fit(rows, 704, 16), _fit(cols, MM_TILE)
    n_r = rows // tr

    def body(o_ref, g_ref, out_ref):
        out_ref[...] = ((o_ref[...] + g_ref[0].astype(F32)) + g_ref[1].astype(F32)) + g_ref[2].astype(F32)

    return pl.pallas_call(
        body, name=name, out_shape=jax.ShapeDtypeStruct((2 * rows, cols), F32), grid=(n_r, cols // tc),
        in_specs=[pl.BlockSpec((tr, tc), lambda i, j: (i, j)), pl.BlockSpec((3, tr, tc), lambda i, j: (0, i, j))],
        out_specs=pl.BlockSpec((tr, tc), lambda i, j: (lax.axis_index("c") * n_r + i, j)),
        compiler_params=_params("parallel", "parallel"),
    )(own, got)


def _pair_share_comm(grads):
    n = len(grads)

    def copy(src, dst, sems, a, mine):
        x, y, c = _place()
        half = grads[a].shape[0] // 2
        rows = pl.ds((c if mine else 1 - c) * half, half)
        return pltpu.make_async_remote_copy(
            src_ref=src[a].at[rows], dst_ref=dst[a].at[rows], send_sem=sems[0].at[a], recv_sem=sems[1].at[a],
            device_id=(x, y, 1 - c), device_id_type=MESH)

    def start(src, dst, sems):
        for a in range(n):
            copy(src, dst, sems, a, True).start()

    def finish(src, dst, sems):
        for a in range(n):
            copy(src, dst, sems, a, False).wait_recv()
            copy(src, dst, sems, a, True).wait_send()

    shapes = [jax.ShapeDtypeStruct(g.shape, g.dtype) for g in grads]
    return _Comm(grads, shapes, {a: a for a in range(n)}, [n, n], start, finish)


def _all_sum_small(vec):
    r, n = vec.shape
    flips = [(a, b, d) for a in (0, 1) for b in (0, 1) for d in (0, 1)][1:]

    def body(v_ref, o_ref, buf, send, recv):
        x, y, c = _place()
        me = 4 * x + 2 * y + c
        cps = []
        for k, (fx, fy, fc) in enumerate(flips):
            cp = pltpu.make_async_remote_copy(
                src_ref=v_ref, dst_ref=buf.at[me], send_sem=send.at[k], recv_sem=recv.at[k],
                device_id=(_flip(x, fx), _flip(y, fy), _flip(c, fc)), device_id_type=MESH)
            cp.start()
            cps.append(cp)
        buf[me] = v_ref[...]
        for k, (fx, fy, fc) in enumerate(flips):
            peer = 4 * _flip(x, fx) + 2 * _flip(y, fy) + _flip(c, fc)
            pltpu.make_async_remote_copy(
                src_ref=v_ref, dst_ref=buf.at[peer], send_sem=send.at[k], recv_sem=recv.at[k],
                device_id=(x, y, c), device_id_type=MESH).wait_recv()
        for cp in cps:
            cp.wait_send()
        acc = buf[0]
        for d in range(1, 8):
            acc = acc + buf[d]
        o_ref[...] = acc

    return pl.pallas_call(
        body, name="all_sum_small", out_shape=jax.ShapeDtypeStruct((r, n), F32),
        in_specs=[pl.BlockSpec(memory_space=pltpu.VMEM)], out_specs=pl.BlockSpec(memory_space=pltpu.VMEM),
        scratch_shapes=[pltpu.VMEM((8, r, n), F32), pltpu.SemaphoreType.DMA((7,)), pltpu.SemaphoreType.DMA((7,))],
    )(vec)


SMALL = ("attn_pre_g", "q_norm_g", "kv_norm_g", "swa_sinks", "grp_a_g", "grp_b_g", "attn_post_g", "ffn_pre_g", "ffn_post_g")
BIG = ("w_in", "w_uq", "w_ukv", "w_o", "w_gate", "w_up", "w_down")
ORDER = ("attn_pre_g", "w_in", "q_norm_g", "w_uq", "kv_norm_g", "w_ukv", "swa_sinks", "grp_a_g", "grp_b_g", "w_o",
         "attn_post_g", "ffn_pre_g", "w_gate", "w_up", "w_down", "ffn_post_g")


def _pad_lanes(v):
    n = v.shape[1]
    return jnp.pad(v, ((0, 0), (0, -n % LANES)))


def kernel(x, positions, attn_pre_g, w_in, q_norm_g, w_uq, kv_norm_g, w_ukv, swa_sinks, grp_a_g, grp_b_g, w_o, attn_post_g, ffn_pre_g, w_gate, w_up, w_down, ffn_post_g, loss_target, m_attn_pre_g, m_w_in, m_q_norm_g, m_w_uq, m_kv_norm_g, m_w_ukv, m_swa_sinks, m_grp_a_g, m_grp_b_g, m_w_o, m_attn_post_g, m_ffn_pre_g, m_w_gate, m_w_up, m_w_down, m_ffn_post_g, v_attn_pre_g, v_w_in, v_q_norm_g, v_w_uq, v_kv_norm_g, v_w_ukv, v_swa_sinks, v_grp_a_g, v_grp_b_g, v_w_o, v_attn_post_g, v_ffn_pre_g, v_w_gate, v_w_up, v_w_down, v_ffn_post_g):
    given = dict(locals())
    w32 = {k: given[k][0] for k in BIG}
    gains = {k: given[k] for k in SMALL}
    xs, tgt = x[0], loss_target[0]
    seq, d_model = xs.shape
    q_rank, kv_rank = q_norm_g.shape[1], kv_norm_g.shape[1]
    kvw = SWA_KV_HEADS * SWA_HEAD_DIM

    slot = {k: _cast_into_slot(w32[k], "cast_" + k) for k in ("w_in", "w_uq", "w_ukv", "w_o")}
    full = {}
    slot["w_gate"], (part,) = _cast_into_slot(w32["w_gate"], "cast_w_gate", comm=_gather_comm([slot["w_in"]], js=(0, 1)))
    slot["w_up"], (full["w_in"],) = _cast_into_slot(w32["w_up"], "cast_w_up", comm=_gather_comm([part], js=(2,)))
    slot["w_down"] = _cast_into_slot(w32["w_down"], "cast_w_down")

    pos = positions[0]
    inv = 1.0 / (ROPE_THETA ** (jnp.arange(0, MLA_ROPE, 2, dtype=F32) / MLA_ROPE))
    ang = pos.astype(F32)[:, None] * inv
    cos, sin = jnp.cos(ang), jnp.sin(ang)
    pos_col, pos_row = pos[:, None], pos[None, :]
    sinks = swa_sinks[0]

    a = _norm_fwd(xs, attn_pre_g, BF16, "attn_pre_norm")
    proj4, (full["w_uq"], full["w_ukv"], full["w_o"]) = _matmul(
        a, full["w_in"], name="in_proj", comm=_gather_comm([slot["w_uq"], slot["w_ukv"], slot["w_o"]]))
    proj = jnp.concatenate([proj4[b] for b in range(N_CHIPS)], axis=1)
    cuts = (0, q_rank, q_rank + kv_rank, q_rank + kv_rank + MLA_ROPE)
    cuts = cuts + (cuts[3] + MIX_B, cuts[3] + MIX_B + kvw, cuts[3] + MIX_B + 2 * kvw)
    c_q, c_kv, k_rope, q_s, k_s, v_s = (proj[:, lo:hi] for lo, hi in zip(cuts[:-1], cuts[1:]))
    cqn = _norm_fwd(c_q, q_norm_g, BF16, "q_norm")
    ckvn = _norm_fwd(c_kv, kv_norm_g, BF16, "kv_norm")
    q4 = _matmul(cqn, full["w_uq"], name="q_up")
    kv4 = _matmul(ckvn, full["w_ukv"], name="kv_up")
    qh, kh, vh = _mla_prep(q4, kv4, k_rope, cos, sin)
    o_a, lse_a, (full["w_gate"],) = _mla_fwd(qh, kh, vh, comm=_gather_comm([slot["w_gate"]]))
    q_sb, k_sb, v_sb = q_s.astype(BF16), k_s.astype(BF16), v_s.astype(BF16)
    o_b, lse_b = _swa_fwd(q_sb, k_sb, v_sb, pos_col, pos_row, sinks)
    mix = _mix_fwd(o_a, o_b, grp_a_g, grp_b_g)
    w_o_full = full["w_o"].reshape(N_CHIPS * full["w_o"].shape[1], d_model)
    ao = _matmul(mix, w_o_full, name="out_proj")
    h1, f = _post_pre_fwd(xs, ao, attn_post_g, ffn_pre_g)
    gate, (full["w_up"],) = _matmul(f, full["w_gate"], name="ffn_gate", comm=_gather_comm([slot["w_up"]]))
    (up, act), (full["w_down"],) = _matmul(
        f, full["w_up"], name="ffn_up", comm=_gather_comm([slot["w_down"]]), epilogue=_swiglu_fwd_tile, extras=(gate,),
        out_dtype=(F32, BF16), tm=512)
    dn = _matmul(act, full["w_down"], reduce_b=True, name="ffn_down")
    loss_row, dy, ddn, d_ffn_post = _loss_bwd(h1, dn, ffn_post_g, tgt)

    dgate, dup = _matmul(ddn, full["w_down"], tb=True, name="ffn_down_dx", epilogue=_swiglu_bwd_tile, extras=(gate, up),
                         out_dtype=(BF16, BF16), tm=512, tk=512)
    dw_down = _matmul(act, ddn, ta=True, name="ffn_down_dw", tn=1024, tk=512)
    dw_gate = _matmul(f, dgate, ta=True, name="ffn_gate_dw", tk=512)
    dw_up = _matmul(f, dup, ta=True, name="ffn_up_dw", tk=512)
    ffn = ("w_down", "w_gate", "w_up")
    grads4 = dict(w_down=dw_down, w_gate=dw_gate, w_up=dw_up)
    df_g, theirs = _matmul(dgate, full["w_gate"], tb=True, reduce_b=True, name="ffn_gate_dx",
                           comm=_pair_exchange_comm([grads4[k] for k in ffn]))
    low, own, got = {}, {}, {}
    for k, t in zip(ffn, theirs):
        low[k], own[k] = _pair_add(grads4[k], t, "pair_add_" + k)
    df_u, (got["w_down"],) = _matmul(dup, full["w_up"], tb=True, reduce_b=True, name="ffn_up_dx",
                                     comm=_chip_exchange_comm([low["w_down"]]))
    dh1, d_ffn_pre = _norm_bwd(h1, ffn_pre_g, [df_g, df_u], [dy], F32, "ffn_pre_norm_bwd")
    dao, d_attn_post = _norm_bwd(ao, attn_post_g, [dh1], [], BF16, "attn_post_norm_bwd")
    dmix = _matmul(dao, w_o_full, tb=True, name="out_proj_dx")
    dw_o = _matmul(mix, dao, ta=True, name="out_proj_dw")
    do_a, do_b, d_grp_a, d_grp_b, dl_a, dl_b = _mix_bwd(o_a, o_b, grp_a_g, grp_b_g, dmix)
    dl_a = dl_a.T[:, :, None]
    dqh, dkh, dvh, (got["w_gate"], got["w_up"]) = _mla_bwd(
        qh, kh, vh, do_a, lse_a, dl_a, comm=_chip_exchange_comm([low["w_gate"], low["w_up"]]))
    dq4, dkv4, dk_rope = _mla_unprep(dqh, dkh, dvh, cos, sin)
    dw_uq = _matmul(cqn, dq4, ta=True, name="q_up_dw")
    dcqn = _matmul(dq4, full["w_uq"], tb=True, reduce_b=True, name="q_up_dx")
    dw_ukv = _matmul(ckvn, dkv4, ta=True, name="kv_up_dw")
    dckvn = _matmul(dkv4, full["w_ukv"], tb=True, reduce_b=True, name="kv_up_dx")
    dc_q, d_q_norm = _norm_bwd(c_q, q_norm_g, [dcqn], [], BF16, "q_norm_bwd")
    dc_kv, d_kv_norm = _norm_bwd(c_kv, kv_norm_g, [dckvn], [], BF16, "kv_norm_bwd")
    dq_s, dk_cur, dk_prev, dv_cur, dv_prev, d_sinks = _swa_bwd(q_sb, k_sb, v_sb, pos_col, pos_row, sinks, do_b, lse_b, dl_b)
    dk_s = _band_merge(dk_cur, dk_prev, "swa_dk_merge")
    dv_s = _band_merge(dv_cur, dv_prev, "swa_dv_merge")
    dproj = jnp.concatenate([dc_q, dc_kv, dk_rope.astype(BF16), dq_s, dk_s, dv_s], axis=1)
    blk_w = dproj.shape[1] // N_CHIPS
    dproj4 = jnp.stack([dproj[:, b * blk_w:(b + 1) * blk_w] for b in range(N_CHIPS)])
    mid = ("w_o", "w_uq", "w_ukv")
    grads4.update(w_o=dw_o.reshape(N_CHIPS, -1, d_model), w_uq=dw_uq, w_ukv=dw_ukv)
    dw_in, theirs = _matmul(a, dproj4, ta=True, name="in_proj_dw", tm=512,
                            comm=_pair_exchange_comm([grads4[k] for k in mid]))
    for k, t in zip(mid, theirs):
        low[k], own[k] = _pair_add(grads4[k], t, "pair_add_" + k)
    da, (*got_mid, theirs_in) = _matmul(
        dproj4, full["w_in"], tb=True, reduce_b=True, name="in_proj_dx",
        comm=_join(_chip_exchange_comm([low[k] for k in mid]), _pair_exchange_comm([dw_in])))
    got.update(zip(mid, got_mid))
    six = tuple(k for k in BIG if k != "w_in")
    halves = [_chip_add(own[k], got[k], "chip_add_" + k) for k in six]
    dx, d_attn_pre, shared = _norm_bwd(xs, attn_pre_g, [da], [dh1], F32, "attn_pre_norm_bwd", comm=_pair_share_comm(halves))
    g_big = dict(zip(six, shared))

    small_grads = dict(attn_pre_g=d_attn_pre, q_norm_g=d_q_norm, kv_norm_g=d_kv_norm, swa_sinks=d_sinks, grp_a_g=d_grp_a,
                       grp_b_g=d_grp_b, attn_post_g=d_attn_post, ffn_pre_g=d_ffn_pre, ffn_post_g=d_ffn_post)
    parts = [loss_row] + [_pad_lanes(small_grads[k]) for k in SMALL]
    packed = jnp.concatenate(parts, axis=1)
    n_packed = packed.shape[1]
    packed = jnp.pad(packed, ((0, 0), (0, -n_packed % (8 * LANES)))).reshape(8, -1)
    total8 = _all_sum_small(packed)
    total = total8.reshape(1, -1)
    loss = total[0, 0]
    g_small, off = {}, LANES
    for k in SMALL:
        n = gains[k].shape[1]
        g_small[k] = total[:, off:off + n]
        off += n + (-n % LANES)

    def pack_small(prefix):
        flat = jnp.concatenate([jnp.zeros((1, LANES), F32)] + [_pad_lanes(given[prefix + k]) for k in SMALL], axis=1)
        return jnp.pad(flat, ((0, 0), (0, -n_packed % (8 * LANES)))).reshape(8, -1)

    d_sm, m_sm, v_sm = (r.reshape(1, -1) for r in
                        _adamw(pack_small(""), total8, pack_small("m_"), pack_small("v_"), "adamw_small"))
    delta, new_m, new_v, off = {}, {}, {}, LANES
    for k in SMALL:
        n = gains[k].shape[1]
        delta[k], new_m[k], new_v[k] = d_sm[:, off:off + n], m_sm[:, off:off + n], v_sm[:, off:off + n]
        off += n + (-n % LANES)

    low["w_in"], own["w_in"] = _pair_add(dw_in, theirs_in, "pair_add_w_in")
    g_out = {}

    def adam(k, comm=None):
        res = _adamw(w32[k], g_big[k], given["m_" + k][0], given["v_" + k][0], "adamw_" + k, emit_grad=True, comm=comm)
        delta[k], new_m[k], new_v[k], g_out[k] = res[:4]
        return res[4] if comm else None

    (part,) = adam("w_gate", _chip_exchange_comm([low["w_in"]], js=(0, 1)))
    (got["w_in"],) = adam("w_up", _chip_exchange_comm([low["w_in"]], js=(2,), into=[part]))
    for k in ("w_down", "w_o", "w_uq", "w_ukv"):
        adam(k)
    half_in = _chip_add(own["w_in"], got["w_in"], "chip_add_w_in")
    (g_big["w_in"],) = _run_comm(_pair_share_comm([half_in]), "grad_pair_share")
    adam("w_in")

    def out(d, k):
        return d[k][None] if k in BIG else d[k]

    grads = {**g_small, **g_out}
    return (loss, dx[None], *[out(grads, k) for k in ORDER], *[out(delta, k) for k in ORDER],
            *[out(new_m, k) for k in ORDER], *[out(new_v, k) for k in ORDER])
```

```python
import functools
import math

import jax
import jax.numpy as jnp
from jax import lax
from jax.experimental import pallas as pl
from jax.experimental.pallas import tpu as pltpu

F32, BF16 = jnp.float32, jnp.bfloat16
MESH = pl.DeviceIdType.MESH
ANY = pl.BlockSpec(memory_space=pl.ANY)

N_CHIPS = 4
EPS = 1e-6
MLA_HEADS, MLA_NOPE, MLA_ROPE, MLA_V = 16, 128, 64, 128
MLA_QK = MLA_NOPE + MLA_ROPE
HEADS_PER_CHIP = MLA_HEADS // N_CHIPS
ROPE_THETA = 10000.0
SWA_Q_HEADS, SWA_KV_HEADS, SWA_HEAD_DIM, SWA_WINDOW = 32, 8, 64, 128
SWA_GROUP = SWA_Q_HEADS // SWA_KV_HEADS
MIX_A, MIX_B = MLA_HEADS * MLA_V, SWA_Q_HEADS * SWA_HEAD_DIM
MASK_VALUE = float(jnp.finfo(jnp.float32).min)
ADAM_LR, ADAM_B1, ADAM_B2, ADAM_EPS, ADAM_WD, ADAM_STEP = 0.001, 0.9, 0.999, 1e-08, 0.01, 10

LANES = 128
VMEM_LIMIT = 56 << 20
ATTN_TILE = 512
ROW_TILE = 256
MM_TILE = 1024


def _fit(dim, pref, mult=LANES):
    if dim <= pref:
        return dim
    for t in range(pref - pref % mult, 0, -mult):
        if dim % t == 0:
            return t
    return dim


def _params(*semantics):
    return pltpu.CompilerParams(dimension_semantics=semantics, vmem_limit_bytes=VMEM_LIMIT)


def _dot(a, b, ca, cb):
    return lax.dot_general(a, b, (((ca,), (cb,)), ((), ())), preferred_element_type=F32)


def _matmul(a, b, *, name, ta=False, tb=False, reduce_b=False, out_dtype=F32, tm=MM_TILE, tn=MM_TILE, tk=MM_TILE, comm=None,
            epilogue=None, extras=()):
    a3, b3 = a.ndim == 3, b.ndim == 3
    nb = a.shape[0] if a3 else (b.shape[0] if b3 else 1)
    (K, M) = a.shape[-2:] if ta else a.shape[-2:][::-1]
    (N, K2) = b.shape[-2:] if tb else b.shape[-2:][::-1]
    assert K == K2, (a.shape, b.shape)
    tm, tn, tk = _fit(M, tm), _fit(N, tn), _fit(K, tk)
    batched_out = (a3 or b3) and not reduce_b
    n_bo = nb if batched_out else 1
    n_br = nb if reduce_b else 1
    nk = K // tk

    def sel(bo, br):
        return br if reduce_b else bo

    def a_map(bo, i, j, br, k):
        t = (k, i) if ta else (i, k)
        return (sel(bo, br),) + t if a3 else t

    def b_map(bo, i, j, br, k):
        t = (j, k) if tb else (k, j)
        return (sel(bo, br),) + t if b3 else t

    def o_map(bo, i, j, br, k):
        return (bo, i, j) if batched_out else (i, j)

    a_blk = (tk, tm) if ta else (tm, tk)
    b_blk = (tn, tk) if tb else (tk, tn)
    a_blk = (None,) + a_blk if a3 else a_blk
    b_blk = (None,) + b_blk if b3 else b_blk
    o_blk = (None, tm, tn) if batched_out else (tm, tn)
    o_shape = (nb, M, N) if batched_out else (M, N)

    grid = (n_bo, M // tm, N // tn, n_br, nk)
    n_ci, n_co = (len(comm.ins), len(comm.out_shapes)) if comm else (0, 0)
    n_x = len(extras)
    out_dtypes = tuple(out_dtype) if epilogue else (out_dtype,)
    n_o = len(out_dtypes)

    def body(*refs):
        a_ref, b_ref = refs[:2]
        x_refs, c_in = refs[2:2 + n_x], refs[2 + n_x:2 + n_x + n_ci]
        refs = refs[2 + n_x + n_ci:]
        o_refs, c_out, acc_ref, sems = refs[:n_o], refs[n_o:n_o + n_co], refs[n_o + n_co], refs[n_o + n_co + 1:]
        br, k = pl.program_id(3), pl.program_id(4)
        first, last = _grid_ends(grid)
        if comm:
            pl.when(first)(lambda: comm.start(c_in, c_out, sems))

        @pl.when((br == 0) & (k == 0))
        def _():
            acc_ref[...] = jnp.zeros_like(acc_ref)

        acc_ref[...] += _dot(a_ref[...], b_ref[...], 0 if ta else 1, 1 if tb else 0)

        @pl.when((br == n_br - 1) & (k == nk - 1))
        def _():
            vals = epilogue(acc_ref[...], *[r[...] for r in x_refs]) if epilogue else (acc_ref[...],)
            for o_ref, v in zip(o_refs, vals):
                o_ref[...] = v.astype(o_ref.dtype)

        if comm:
            pl.when(last)(lambda: comm.finish(c_in, c_out, sems))

    o_spec = pl.BlockSpec(o_blk, o_map)
    res = pl.pallas_call(
        body, name=name,
        out_shape=tuple(jax.ShapeDtypeStruct(o_shape, d) for d in out_dtypes) + tuple(comm.out_shapes if comm else ()),
        grid=grid,
        in_specs=[pl.BlockSpec(a_blk, a_map), pl.BlockSpec(b_blk, b_map)] + [o_spec] * n_x + [ANY] * n_ci,
        out_specs=(o_spec,) * n_o + (ANY,) * n_co,
        input_output_aliases={2 + n_x + i: n_o + o for i, o in comm.aliases.items()} if comm else {},
        scratch_shapes=[pltpu.VMEM((tm, tn), F32)] + (comm.scratch if comm else []),
        compiler_params=_params(*(["parallel"] * 3 + ["arbitrary"] * 2 if not comm else ["arbitrary"] * 5)),
    )(a, b, *extras, *(comm.ins if comm else ()))
    main = res[:n_o] if epilogue else res[0]
    return (main, res[n_o:]) if comm else main


def _inv_rms(u):
    return lax.rsqrt(jnp.mean(u * u, axis=-1, keepdims=True) + EPS)


def _norm_bwd_math(u, g, dz):
    r = _inv_rms(u)
    w = dz * g
    du = r * w - u * (r * r * r * jnp.mean(w * u, axis=-1, keepdims=True))
    dg = jnp.sum(dz * (u * r), axis=0, keepdims=True)
    return du, dg


def _row_spec(tr, n):
    return pl.BlockSpec((tr, n), lambda i: (i, 0))


def _gain_spec(n):
    return pl.BlockSpec((1, n), lambda i: (0, 0))


def _norm_fwd(u, g, out_dtype, name):
    rows, n = u.shape
    tr = _fit(rows, ROW_TILE, 16)

    def body(u_ref, g_ref, o_ref):
        x = u_ref[...]
        o_ref[...] = (x * _inv_rms(x) * g_ref[...]).astype(o_ref.dtype)

    return pl.pallas_call(
        body, name=name, out_shape=jax.ShapeDtypeStruct((rows, n), out_dtype), grid=(rows // tr,),
        in_specs=[_row_spec(tr, n), _gain_spec(n)], out_specs=_row_spec(tr, n), compiler_params=_params("parallel"),
    )(u, g)


def _norm_bwd(u, g, dzs, adds, out_dtype, name, comm=None):
    rows, n = u.shape
    tr = _fit(rows, ROW_TILE, 16)
    n_dz, n_add = len(dzs), len(adds)
    grid = (rows // tr,)

    def body(ins, outs):
        u_ref, g_ref = ins[:2]
        dz_refs, add_refs = ins[2:2 + n_dz], ins[2 + n_dz:]
        du_ref, dg_ref = outs
        dz = dz_refs[0][...].astype(F32)
        for r in dz_refs[1:]:
            dz = dz + r[...].astype(F32)
        du, dg = _norm_bwd_math(u_ref[...], g_ref[...], dz)
        for r in add_refs:
            du = du + r[...]
        du_ref[...] = du.astype(du_ref.dtype)

        @pl.when(pl.program_id(0) == 0)
        def _():
            dg_ref[...] = jnp.zeros_like(dg_ref)

        dg_ref[...] += dg

    full, extra = _carry(body, comm, 2 + n_dz + n_add, 2, grid)
    res = pl.pallas_call(
        full, name=name,
        out_shape=(jax.ShapeDtypeStruct((rows, n), out_dtype), jax.ShapeDtypeStruct((1, n), F32)) + extra["out_shape"],
        grid=grid,
        in_specs=[_row_spec(tr, n), _gain_spec(n)] + [_row_spec(tr, n)] * (n_dz + n_add) + extra["in_specs"],
        out_specs=(_row_spec(tr, n), _gain_spec(n)) + extra["out_specs"],
        input_output_aliases=extra["aliases"], scratch_shapes=extra["scratch"], compiler_params=_params("arbitrary"),
    )(u, g, *dzs, *adds, *extra["operands"])
    return (res[0], res[1], res[2:]) if comm else (res[0], res[1])


def _mix_fwd(o_a, o_b, g_a, g_b):
    rows = o_a.shape[0]
    tr = _fit(rows, ROW_TILE, 16)

    def body(a_ref, b_ref, ga_ref, gb_ref, o_ref):
        a, b = a_ref[...], b_ref[...]
        o_ref[:, :MIX_A] = (a * _inv_rms(a) * ga_ref[...]).astype(BF16)
        o_ref[:, MIX_A:] = (b * _inv_rms(b) * gb_ref[...]).astype(BF16)

    return pl.pallas_call(
        body, name="mix_fwd", out_shape=jax.ShapeDtypeStruct((rows, MIX_A + MIX_B), BF16), grid=(rows // tr,),
        in_specs=[_row_spec(tr, MIX_A), _row_spec(tr, MIX_B), _gain_spec(MIX_A), _gain_spec(MIX_B)],
        out_specs=_row_spec(tr, MIX_A + MIX_B), compiler_params=_params("parallel"),
    )(o_a, o_b, g_a, g_b)


def _mix_bwd(o_a, o_b, g_a, g_b, dmix):
    rows = o_a.shape[0]
    tr = _fit(rows, ROW_TILE, 16)

    def body(a_ref, b_ref, ga_ref, gb_ref, dm_ref, doa_ref, dob_ref, dga_ref, dgb_ref, dla_ref, dlb_ref):
        a, b = a_ref[...], b_ref[...]
        doa, dga = _norm_bwd_math(a, ga_ref[...], dm_ref[:, :MIX_A])
        dob, dgb = _norm_bwd_math(b, gb_ref[...], dm_ref[:, MIX_A:])
        doa_ref[...] = doa.astype(BF16)
        dob_ref[...] = dob.astype(BF16)
        pa, pb = doa * a, dob * b
        for h in range(MLA_HEADS):
            dla_ref[:, h:h + 1] = jnp.sum(pa[:, h * MLA_V:(h + 1) * MLA_V], axis=-1, keepdims=True)
        for h in range(SWA_Q_HEADS):
            dlb_ref[:, h:h + 1] = jnp.sum(pb[:, h * SWA_HEAD_DIM:(h + 1) * SWA_HEAD_DIM], axis=-1, keepdims=True)

        @pl.when(pl.program_id(0) == 0)
        def _():
            dga_ref[...] = jnp.zeros_like(dga_ref)
            dgb_ref[...] = jnp.zeros_like(dgb_ref)

        dga_ref[...] += dga
        dgb_ref[...] += dgb

    return pl.pallas_call(
        body, name="mix_bwd",
        out_shape=(jax.ShapeDtypeStruct((rows, MIX_A), BF16), jax.ShapeDtypeStruct((rows, MIX_B), BF16),
                   jax.ShapeDtypeStruct((1, MIX_A), F32), jax.ShapeDtypeStruct((1, MIX_B), F32),
                   jax.ShapeDtypeStruct((rows, MLA_HEADS), F32), jax.ShapeDtypeStruct((rows, SWA_Q_HEADS), F32)),
        grid=(rows // tr,),
        in_specs=[_row_spec(tr, MIX_A), _row_spec(tr, MIX_B), _gain_spec(MIX_A), _gain_spec(MIX_B),
                  _row_spec(tr, MIX_A + MIX_B)],
        out_specs=(_row_spec(tr, MIX_A), _row_spec(tr, MIX_B), _gain_spec(MIX_A), _gain_spec(MIX_B),
                   _row_spec(tr, MLA_HEADS), _row_spec(tr, SWA_Q_HEADS)),
        compiler_params=_params("arbitrary"),
    )(o_a, o_b, g_a, g_b, dmix)


def _post_pre_fwd(x, ao, g_post, g_pre):
    rows, n = x.shape
    tr = _fit(rows, ROW_TILE, 16)

    def body(x_ref, ao_ref, g1_ref, g2_ref, h_ref, f_ref):
        u = ao_ref[...]
        h = x_ref[...] + u * _inv_rms(u) * g1_ref[...]
        h_ref[...] = h
        f_ref[...] = (h * _inv_rms(h) * g2_ref[...]).astype(BF16)

    return pl.pallas_call(
        body, name="post_pre_fwd",
        out_shape=(jax.ShapeDtypeStruct((rows, n), F32), jax.ShapeDtypeStruct((rows, n), BF16)), grid=(rows // tr,),
        in_specs=[_row_spec(tr, n), _row_spec(tr, n), _gain_spec(n), _gain_spec(n)],
        out_specs=(_row_spec(tr, n), _row_spec(tr, n)), compiler_params=_params("parallel"),
    )(x, ao, g_post, g_pre)


def _loss_bwd(h1, dn, g_post, target):
    rows, n = h1.shape
    tr = _fit(rows, ROW_TILE, 16)

    def body(h_ref, u_ref, g_ref, t_ref, loss_ref, dy_ref, du_ref, dg_ref):
        u, g = u_ref[...], g_ref[...]
        err = h_ref[...] + u * _inv_rms(u) * g - t_ref[...]
        dy = err / n
        dy_ref[...] = dy
        du, dg = _norm_bwd_math(u, g, dy)
        du_ref[...] = du.astype(BF16)

        @pl.when(pl.program_id(0) == 0)
        def _():
            loss_ref[...] = jnp.zeros_like(loss_ref)
            dg_ref[...] = jnp.zeros_like(dg_ref)

        loss_ref[...] += jnp.full((1, LANES), 0.5 * jnp.sum(jnp.mean(err * err, axis=-1)), F32)
        dg_ref[...] += dg

    return pl.pallas_call(
        body, name="loss_bwd",
        out_shape=(jax.ShapeDtypeStruct((1, LANES), F32), jax.ShapeDtypeStruct((rows, n), F32),
                   jax.ShapeDtypeStruct((rows, n), BF16), jax.ShapeDtypeStruct((1, n), F32)),
        grid=(rows // tr,),
        in_specs=[_row_spec(tr, n), _row_spec(tr, n), _gain_spec(n), _row_spec(tr, n)],
        out_specs=(_gain_spec(LANES), _row_spec(tr, n), _row_spec(tr, n), _gain_spec(n)),
        compiler_params=_params("arbitrary"),
    )(h1, dn, g_post, target)


def _blk3_spec(tr, n):
    return pl.BlockSpec((None, tr, n), lambda b, i: (b, i, 0))


def _swiglu_fwd_tile(up, gate):
    return up, gate * jax.nn.sigmoid(gate) * up


def _swiglu_bwd_tile(dact, gate, up):
    sig = jax.nn.sigmoid(gate)
    return dact * up * (sig * (1.0 + gate * (1.0 - sig))), dact * (gate * sig)


def _rope(x, cos, sin):
    half = MLA_ROPE // 2
    x1, x2 = x[:, :half], x[:, half:]
    return jnp.concatenate([x1 * cos - x2 * sin, x2 * cos + x1 * sin], axis=-1)


def _rope_t(d, cos, sin):
    half = MLA_ROPE // 2
    d1, d2 = d[:, :half], d[:, half:]
    return jnp.concatenate([d1 * cos + d2 * sin, d2 * cos - d1 * sin], axis=-1)


def _mla_prep(q4, kv4, k_rope, cos, sin):
    nb, rows, _ = q4.shape
    tr = _fit(rows, ROW_TILE, 16)
    hpc, half = HEADS_PER_CHIP, MLA_ROPE // 2

    def body(q_ref, kv_ref, kr_ref, c_ref, s_ref, qo_ref, ko_ref, vo_ref):
        cos, sin = c_ref[...], s_ref[...]
        k_pe = _rope(kr_ref[...], cos, sin)
        q_all, kv_all = q_ref[...], kv_ref[...]
        for h in range(hpc):
            q = q_all[:, h * MLA_QK:(h + 1) * MLA_QK]
            qo_ref[h] = jnp.concatenate([q[:, :MLA_NOPE], _rope(q[:, MLA_NOPE:], cos, sin)], axis=-1).astype(BF16)
            kv = kv_all[:, h * (MLA_NOPE + MLA_V):(h + 1) * (MLA_NOPE + MLA_V)]
            ko_ref[h] = jnp.concatenate([kv[:, :MLA_NOPE], k_pe], axis=-1).astype(BF16)
            vo_ref[h] = kv[:, MLA_NOPE:].astype(BF16)

    def head_spec(n):
        return pl.BlockSpec((hpc, tr, n), lambda b, i: (b, i, 0))

    def row_spec(n):
        return pl.BlockSpec((tr, n), lambda b, i: (i, 0))

    return pl.pallas_call(
        body, name="mla_prep",
        out_shape=(jax.ShapeDtypeStruct((MLA_HEADS, rows, MLA_QK), BF16), jax.ShapeDtypeStruct((MLA_HEADS, rows, MLA_QK), BF16),
                   jax.ShapeDtypeStruct((MLA_HEADS, rows, MLA_V), BF16)),
        grid=(nb, rows // tr),
        in_specs=[_blk3_spec(tr, hpc * MLA_QK), _blk3_spec(tr, hpc * (MLA_NOPE + MLA_V)), row_spec(MLA_ROPE),
                  row_spec(half), row_spec(half)],
        out_specs=(head_spec(MLA_QK), head_spec(MLA_QK), head_spec(MLA_V)),
        compiler_params=_params("parallel", "parallel"),
    )(q4, kv4, k_rope, cos, sin)


def _mla_unprep(dq, dk, dv, cos, sin):
    _, rows, _ = dq.shape
    tr = _fit(rows, ROW_TILE, 16)
    hpc, half = HEADS_PER_CHIP, MLA_ROPE // 2

    def body(dq_ref, dk_ref, dv_ref, c_ref, s_ref, q4_ref, kv4_ref, kr_ref):
        cos, sin = c_ref[...], s_ref[...]
        d_pe = jnp.zeros((tr, MLA_ROPE), F32)
        q_parts, kv_parts = [], []
        for h in range(hpc):
            g, gk = dq_ref[h], dk_ref[h]
            q_parts += [g[:, :MLA_NOPE], _rope_t(g[:, MLA_NOPE:], cos, sin)]
            kv_parts += [gk[:, :MLA_NOPE], dv_ref[h]]
            d_pe = d_pe + gk[:, MLA_NOPE:]
        q4_ref[...] = jnp.concatenate(q_parts, axis=-1).astype(BF16)
        kv4_ref[...] = jnp.concatenate(kv_parts, axis=-1).astype(BF16)

        @pl.when(pl.program_id(1) == 0)
        def _():
            kr_ref[...] = jnp.zeros_like(kr_ref)

        kr_ref[...] += _rope_t(d_pe, cos, sin)

    def head_spec(n):
        return pl.BlockSpec((hpc, tr, n), lambda i, b: (b, i, 0))

    def row_spec(n):
        return pl.BlockSpec((tr, n), lambda i, b: (i, 0))

    def blk_spec(n):
        return pl.BlockSpec((None, tr, n), lambda i, b: (b, i, 0))

    return pl.pallas_call(
        body, name="mla_unprep",
        out_shape=(jax.ShapeDtypeStruct((N_CHIPS, rows, hpc * MLA_QK), BF16),
                   jax.ShapeDtypeStruct((N_CHIPS, rows, hpc * (MLA_NOPE + MLA_V)), BF16),
                   jax.ShapeDtypeStruct((rows, MLA_ROPE), F32)),
        grid=(rows // tr, N_CHIPS),
        in_specs=[head_spec(MLA_QK), head_spec(MLA_QK), head_spec(MLA_V), row_spec(half), row_spec(half)],
        out_specs=(blk_spec(hpc * MLA_QK), blk_spec(hpc * (MLA_NOPE + MLA_V)), row_spec(MLA_ROPE)),
        compiler_params=_params("parallel", "arbitrary"),
    )(dq, dk, dv, cos, sin)


def _carry(body, comm, n_in, n_out, grid):
    n_ci, n_co = (len(comm.ins), len(comm.out_shapes)) if comm else (0, 0)

    def full(*refs):
        ins, c_in = refs[:n_in], refs[n_in:n_in + n_ci]
        outs = refs[n_in + n_ci:n_in + n_ci + n_out]
        c_out = refs[n_in + n_ci + n_out:n_in + n_ci + n_out + n_co]
        sems = refs[n_in + n_ci + n_out + n_co:]
        first, last = _grid_ends(grid)
        if comm:
            pl.when(first)(lambda: comm.start(c_in, c_out, sems))
        body(ins, outs)
        if comm:
            pl.when(last)(lambda: comm.finish(c_in, c_out, sems))

    extra = dict(
        operands=list(comm.ins) if comm else [], in_specs=[ANY] * n_ci, out_specs=(ANY,) * n_co,
        out_shape=tuple(comm.out_shapes) if comm else (),
        aliases={n_in + i: n_out + o for i, o in comm.aliases.items()} if comm else {},
        scratch=comm.scratch if comm else [])
    return full, extra


def _causal_keep(t):
    return lax.broadcasted_iota(jnp.int32, (t, t), 1) <= lax.broadcasted_iota(jnp.int32, (t, t), 0)


def _mla_fwd(q, k, v, comm=None):
    nh, rows, _ = q.shape
    t = _fit(rows, ATTN_TILE)
    scale = MLA_QK ** -0.5
    grid = (nh, rows // t)

    def body(ins, outs):
        (q_ref, k_ref, v_ref), (o_ref, lse_ref) = ins, outs
        i = pl.program_id(1)
        qb = q_ref[...]

        def step(j, carry, diagonal):
            m, l, acc = carry
            rows_j = pl.ds(pl.multiple_of(j * t, t), t)
            s = _dot(qb, k_ref[rows_j, :], 1, 1) * scale
            if diagonal:
                s = jnp.where(_causal_keep(t), s, MASK_VALUE)
            m_new = jnp.maximum(m, jnp.max(s, axis=-1, keepdims=True))
            alpha = jnp.exp(m - m_new)
            p = jnp.exp(s - m_new)
            l = alpha * l + jnp.sum(p, axis=-1, keepdims=True)
            acc = alpha * acc + _dot(p.astype(BF16), v_ref[rows_j, :], 1, 0)
            return m_new, l, acc

        init = (jnp.full((t, 1), MASK_VALUE, F32), jnp.zeros((t, 1), F32), jnp.zeros((t, MLA_V), F32))
        carry = lax.fori_loop(0, i, lambda j, c: step(j, c, False), init)
        m, l, acc = step(i, carry, True)
        o_ref[...] = acc / l
        lse_ref[...] = m + jnp.log(l)

    full, extra = _carry(body, comm, 3, 2, grid)
    res = pl.pallas_call(
        full, name="mla_fwd",
        out_shape=(jax.ShapeDtypeStruct((rows, nh * MLA_V), F32), jax.ShapeDtypeStruct((nh, rows, 1), F32)) + extra["out_shape"],
        grid=grid,
        in_specs=[pl.BlockSpec((None, t, MLA_QK), lambda h, i: (h, i, 0)),
                  pl.BlockSpec((None, rows, MLA_QK), lambda h, i: (h, 0, 0)),
                  pl.BlockSpec((None, rows, MLA_V), lambda h, i: (h, 0, 0))] + extra["in_specs"],
        out_specs=(pl.BlockSpec((t, MLA_V), lambda h, i: (i, h)), pl.BlockSpec((None, t, 1), lambda h, i: (h, i, 0))) + extra["out_specs"],
        input_output_aliases=extra["aliases"], scratch_shapes=extra["scratch"],
        compiler_params=_params("arbitrary", "arbitrary"),
    )(q, k, v, *extra["operands"])
    return res[0], res[1], res[2:]


def _mla_bwd(q, k, v, do, lse, delta, comm=None):
    nh, rows, _ = q.shape
    t = _fit(rows, ATTN_TILE)
    n_t = rows // t
    scale = MLA_QK ** -0.5
    grid = (nh, n_t)

    def body(ins, outs):
        (q_ref, k_ref, v_ref, do_ref, lse_ref, dl_ref), (dq_ref, dk_ref, dv_ref) = ins, outs
        j = pl.program_id(1)
        kb, vb = k_ref[...], v_ref[...]

        @pl.when(j == 0)
        def _():
            dq_ref[...] = jnp.zeros_like(dq_ref)

        def step(i, carry, diagonal):
            dk, dv = carry
            rows_i = pl.ds(pl.multiple_of(i * t, t), t)
            qb, dob = q_ref[rows_i, :], do_ref[rows_i, :]
            s = _dot(qb, kb, 1, 1) * scale
            if diagonal:
                s = jnp.where(_causal_keep(t), s, MASK_VALUE)
            p = jnp.exp(s - lse_ref[rows_i, :])
            dv = dv + _dot(p.astype(BF16), dob, 0, 0)
            dp = _dot(dob, vb, 1, 1)
            ds = (p * (dp - dl_ref[rows_i, :]) * scale).astype(BF16)
            dk = dk + _dot(ds, qb, 0, 0)
            dq_ref[rows_i, :] += _dot(ds, kb, 1, 0)
            return dk, dv

        carry = step(j, (jnp.zeros((t, MLA_QK), F32), jnp.zeros((t, MLA_V), F32)), True)
        dk, dv = lax.fori_loop(j + 1, n_t, lambda i, c: step(i, c, False), carry)
        dk_ref[...] = dk
        dv_ref[...] = dv

    def head(n):
        return pl.BlockSpec((None, rows, n), lambda h, j: (h, 0, 0))

    def tile(n):
        return pl.BlockSpec((None, t, n), lambda h, j: (h, j, 0))

    full, extra = _carry(body, comm, 6, 3, grid)
    res = pl.pallas_call(
        full, name="mla_bwd",
        out_shape=(jax.ShapeDtypeStruct((nh, rows, MLA_QK), F32), jax.ShapeDtypeStruct((nh, rows, MLA_QK), F32),
                   jax.ShapeDtypeStruct((nh, rows, MLA_V), F32)) + extra["out_shape"],
        grid=grid,
        in_specs=[head(MLA_QK), tile(MLA_QK), tile(MLA_V), pl.BlockSpec((rows, MLA_V), lambda h, j: (0, h)), head(1), head(1)]
        + extra["in_specs"],
        out_specs=(head(MLA_QK), tile(MLA_QK), tile(MLA_V)) + extra["out_specs"],
        input_output_aliases=extra["aliases"], scratch_shapes=extra["scratch"],
        compiler_params=_params("arbitrary", "arbitrary"),
    )(q, k, v, do, lse, delta, *extra["operands"])
    return res[0], res[1], res[2], res[3:]


def _swa_slope(h):
    return 2.0 ** (-8.0 * (h + 1) / SWA_Q_HEADS)


def _swa_band_specs(rows):
    w = SWA_WINDOW
    kvw = SWA_KV_HEADS * SWA_HEAD_DIM

    def prev(i):
        return jnp.maximum(i - 1, 0)

    return [pl.BlockSpec((w, kvw), lambda i: (prev(i), 0)), pl.BlockSpec((w, kvw), lambda i: (i, 0)),
            pl.BlockSpec((w, kvw), lambda i: (prev(i), 0)), pl.BlockSpec((w, kvw), lambda i: (i, 0)),
            pl.BlockSpec((w, 1), lambda i: (i, 0)),
            pl.BlockSpec((1, w), lambda i: (0, prev(i))), pl.BlockSpec((1, w), lambda i: (0, i)),
            pl.BlockSpec(memory_space=pltpu.SMEM)]


def _swa_bias(i, pc_ref, pp_ref, pn_ref):
    w = SWA_WINDOW
    k_pos = jnp.concatenate([pp_ref[...], pn_ref[...]], axis=1)
    dist = jnp.abs(pc_ref[...] - k_pos).astype(F32)
    r = lax.broadcasted_iota(jnp.int32, (w, 2 * w), 0)
    col = lax.broadcasted_iota(jnp.int32, (w, 2 * w), 1)
    delta = r + w - col
    valid = (delta >= 0) & (delta < w) & ((col >= w) | (i > 0))
    return dist, valid


def _swa_fwd(q, k, v, pos_col, pos_row, sinks):
    rows = q.shape[0]
    w, hd = SWA_WINDOW, SWA_HEAD_DIM
    scale = hd ** -0.5

    def body(q_ref, kp_ref, kc_ref, vp_ref, vc_ref, pc_ref, pp_ref, pn_ref, sink_ref, o_ref, lse_ref):
        dist, valid = _swa_bias(pl.program_id(0), pc_ref, pp_ref, pn_ref)
        for kvh in range(SWA_KV_HEADS):
            cols = slice(kvh * hd, (kvh + 1) * hd)
            kb = jnp.concatenate([kp_ref[:, cols], kc_ref[:, cols]], axis=0)
            vb = jnp.concatenate([vp_ref[:, cols], vc_ref[:, cols]], axis=0)
            for g in range(SWA_GROUP):
                h = kvh * SWA_GROUP + g
                sink = sink_ref[h]
                s = _dot(q_ref[:, h * hd:(h + 1) * hd], kb, 1, 1) * scale - _swa_slope(h) * dist
                s = jnp.where(valid, s, MASK_VALUE)
                m = jnp.maximum(jnp.max(s, axis=-1, keepdims=True), sink)
                e = jnp.exp(s - m)
                den = jnp.sum(e, axis=-1, keepdims=True) + jnp.exp(sink - m)
                o_ref[:, h * hd:(h + 1) * hd] = _dot((e / den).astype(BF16), vb, 1, 0)
                lse_ref[:, h:h + 1] = m + jnp.log(den)

    return pl.pallas_call(
        body, name="swa_fwd",
        out_shape=(jax.ShapeDtypeStruct((rows, MIX_B), F32), jax.ShapeDtypeStruct((rows, SWA_Q_HEADS), F32)),
        grid=(rows // w,),
        in_specs=[pl.BlockSpec((w, MIX_B), lambda i: (i, 0))] + _swa_band_specs(rows),
        out_specs=(pl.BlockSpec((w, MIX_B), lambda i: (i, 0)), pl.BlockSpec((w, SWA_Q_HEADS), lambda i: (i, 0))),
        compiler_params=_params("parallel"),
    )(q, k, k, v, v, pos_col, pos_row, pos_row, sinks)


def _swa_bwd(q, k, v, pos_col, pos_row, sinks, do, lse, delta):
    rows = q.shape[0]
    w, hd = SWA_WINDOW, SWA_HEAD_DIM
    kvw = SWA_KV_HEADS * hd
    scale = hd ** -0.5

    def body(q_ref, kp_ref, kc_ref, vp_ref, vc_ref, pc_ref, pp_ref, pn_ref, sink_ref, do_ref, lse_ref, dl_ref,
             dq_ref, dkc_ref, dkp_ref, dvc_ref, dvp_ref, dsink_ref):
        dist, valid = _swa_bias(pl.program_id(0), pc_ref, pp_ref, pn_ref)

        @pl.when(pl.program_id(0) == 0)
        def _():
            dsink_ref[...] = jnp.zeros_like(dsink_ref)

        for kvh in range(SWA_KV_HEADS):
            cols = slice(kvh * hd, (kvh + 1) * hd)
            kb = jnp.concatenate([kp_ref[:, cols], kc_ref[:, cols]], axis=0)
            vb = jnp.concatenate([vp_ref[:, cols], vc_ref[:, cols]], axis=0)
            dk = jnp.zeros((2 * w, hd), F32)
            dv = jnp.zeros((2 * w, hd), F32)
            for g in range(SWA_GROUP):
                h = kvh * SWA_GROUP + g
                hc = slice(h * hd, (h + 1) * hd)
                qb, dob = q_ref[:, hc], do_ref[:, hc]
                lse_h, dl_h = lse_ref[:, h:h + 1], dl_ref[:, h:h + 1]
                s = _dot(qb, kb, 1, 1) * scale - _swa_slope(h) * dist
                s = jnp.where(valid, s, MASK_VALUE)
                p = jnp.exp(s - lse_h)
                ds = (p * (_dot(dob, vb, 1, 1) - dl_h) * scale).astype(BF16)
                dq_ref[:, hc] = _dot(ds, kb, 1, 0).astype(BF16)
                dk = dk + _dot(ds, qb, 0, 0)
                dv = dv + _dot(p.astype(BF16), dob, 0, 0)
                dsink_ref[:, h:h + 1] += -jnp.sum(jnp.exp(sink_ref[h] - lse_h) * dl_h, axis=0, keepdims=True)
            dkp_ref[:, cols] = dk[:w]
            dkc_ref[:, cols] = dk[w:]
            dvp_ref[:, cols] = dv[:w]
            dvc_ref[:, cols] = dv[w:]

    def blk(n):
        return pl.BlockSpec((w, n), lambda i: (i, 0))

    return pl.pallas_call(
        body, name="swa_bwd",
        out_shape=(jax.ShapeDtypeStruct((rows, MIX_B), BF16),) + (jax.ShapeDtypeStruct((rows, kvw), F32),) * 4
        + (jax.ShapeDtypeStruct((1, SWA_Q_HEADS), F32),),
        grid=(rows // w,),
        in_specs=[blk(MIX_B)] + _swa_band_specs(rows) + [blk(MIX_B), blk(SWA_Q_HEADS), blk(SWA_Q_HEADS)],
        out_specs=(blk(MIX_B), blk(kvw), blk(kvw), blk(kvw), blk(kvw), pl.BlockSpec((1, SWA_Q_HEADS), lambda i: (0, 0))),
        compiler_params=_params("arbitrary"),
    )(q, k, k, v, v, pos_col, pos_row, pos_row, sinks, do, lse, delta)


def _band_merge(cur, prev, name):
    rows, n = cur.shape
    w = SWA_WINDOW
    last = rows // w - 1

    def body(c_ref, p_ref, o_ref):
        nxt = jnp.where(pl.program_id(0) < last, p_ref[...], 0.0)
        o_ref[...] = (c_ref[...] + nxt).astype(BF16)

    return pl.pallas_call(
        body, name=name, out_shape=jax.ShapeDtypeStruct((rows, n), BF16), grid=(rows // w,),
        in_specs=[pl.BlockSpec((w, n), lambda j: (j, 0)), pl.BlockSpec((w, n), lambda j: (jnp.minimum(j + 1, last), 0))],
        out_specs=pl.BlockSpec((w, n), lambda j: (j, 0)), compiler_params=_params("parallel"),
    )(cur, prev)


def _adamw(w, g, m, v, name, emit_grad=False, comm=None):
    rows, cols = w.shape
    tr, tc = _fit(rows, 352, 8), _fit(cols, MM_TILE)
    c1 = 1.0 - ADAM_B1 ** ADAM_STEP
    c2 = 1.0 - ADAM_B2 ** ADAM_STEP
    n_out = 4 if emit_grad else 3
    grid = (rows // tr, cols // tc)

    def body(ins, outs):
        w_ref, g_ref, m_ref, v_ref = ins
        gr = g_ref[...]
        m_new = ADAM_B1 * m_ref[...] + (1.0 - ADAM_B1) * gr
        v_new = ADAM_B2 * v_ref[...] + (1.0 - ADAM_B2) * jnp.square(gr)
        outs[1][...] = m_new
        outs[2][...] = v_new
        outs[0][...] = -ADAM_LR * ((m_new / c1) / (jnp.sqrt(v_new / c2) + ADAM_EPS) + ADAM_WD * w_ref[...])
        if emit_grad:
            outs[3][...] = gr

    spec = pl.BlockSpec((tr, tc), lambda i, j: (i, j))
    full, extra = _carry(body, comm, 4, n_out, grid)
    res = pl.pallas_call(
        full, name=name, out_shape=(jax.ShapeDtypeStruct(w.shape, F32),) * n_out + extra["out_shape"], grid=grid,
        in_specs=[spec] * 4 + extra["in_specs"], out_specs=(spec,) * n_out + extra["out_specs"],
        input_output_aliases=extra["aliases"], scratch_shapes=extra["scratch"],
        compiler_params=_params("arbitrary", "arbitrary"),
    )(w, g, m, v, *extra["operands"])
    return res[:n_out] + ((res[n_out:],) if comm else ())


OTHER_CHIPS = ((1, 0), (0, 1), (1, 1))


def _place():
    x, y, c = lax.axis_index("x"), lax.axis_index("y"), lax.axis_index("c")
    return x, y, c


def _flip(v, f):
    return 1 - v if f else v


class _Comm:
    def __init__(self, ins, out_shapes, aliases, sem_sizes, start, finish):
        self.ins, self.out_shapes, self.aliases, self.sem_sizes = list(ins), list(out_shapes), dict(aliases), list(sem_sizes)
        self.start, self.finish = start, finish

    @property
    def scratch(self):
        return [pltpu.SemaphoreType.DMA((n,)) for n in self.sem_sizes]


def _join(first, second):
    n_i, n_o, n_s = len(first.ins), len(first.out_shapes), len(first.sem_sizes)

    def both(method):
        def run(ins, outs, sems):
            getattr(first, method)(ins[:n_i], outs[:n_o], sems[:n_s])
            getattr(second, method)(ins[n_i:], outs[n_o:], sems[n_s:])
        return run

    aliases = {**first.aliases, **{n_i + i: n_o + o for i, o in second.aliases.items()}}
    return _Comm(first.ins + second.ins, first.out_shapes + second.out_shapes, aliases, first.sem_sizes + second.sem_sizes,
                 both("start"), both("finish"))


def _run_comm(comm, name):
    n_in, n_out = len(comm.ins), len(comm.out_shapes)

    def body(*refs):
        ins, outs, sems = refs[:n_in], refs[n_in:n_in + n_out], refs[n_in + n_out:]
        comm.start(ins, outs, sems)
        comm.finish(ins, outs, sems)

    return pl.pallas_call(
        body, name=name, out_shape=tuple(comm.out_shapes), in_specs=[ANY] * n_in, out_specs=(ANY,) * n_out,
        input_output_aliases=comm.aliases, scratch_shapes=comm.scratch,
    )(*comm.ins)


def _grid_ends(grid):
    first = last = None
    for axis, n in enumerate(grid):
        pid = pl.program_id(axis)
        first = (pid == 0) if first is None else first & (pid == 0)
        last = (pid == n - 1) if last is None else last & (pid == n - 1)
    return first, last


def _cast_into_slot(w, name, comm=None):
    rows, cols = w.shape
    tr, tc = _fit(rows, 704, 16), _fit(cols, MM_TILE)
    grid = (rows // tr, cols // tc)

    def body(ins, outs):
        outs[0][...] = ins[0][...].astype(BF16)

    full, extra = _carry(body, comm, 1, 1, grid)
    res = pl.pallas_call(
        full, name=name, out_shape=(jax.ShapeDtypeStruct((N_CHIPS, rows, cols), BF16),) + extra["out_shape"], grid=grid,
        in_specs=[pl.BlockSpec((tr, tc), lambda i, j: (i, j))] + extra["in_specs"],
        out_specs=(pl.BlockSpec((None, tr, tc), lambda i, j: (2 * lax.axis_index("x") + lax.axis_index("y"), i, j)),)
        + extra["out_specs"],
        input_output_aliases=extra["aliases"], scratch_shapes=extra["scratch"],
        compiler_params=_params("arbitrary", "arbitrary"),
    )(w, *extra["operands"])
    return (res[0], res[1:]) if comm else res[0]


def _gather_comm(slots, js=(0, 1, 2)):
    n = len(slots)
    pairs = [(a, j) for a in range(n) for j in js]

    def copies(src, dst, sems):
        send, recv, fsend, frecv = sems
        x, y, c = _place()
        me = 2 * x + y

        def rows(a, core):
            half = slots[a].shape[1] // 2
            return pl.ds(core * half, half)

        def chip(j):
            return _flip(x, OTHER_CHIPS[j][0]), _flip(y, OTHER_CHIPS[j][1])

        def out(a, j):
            px, py = chip(j)
            return pltpu.make_async_remote_copy(
                src_ref=src[a].at[me, rows(a, c)], dst_ref=dst[a].at[me, rows(a, c)], send_sem=send.at[a * 3 + j],
                recv_sem=recv.at[a * 3 + j], device_id=(px, py, c), device_id_type=MESH)

        def landed(a, j):
            px, py = chip(j)
            blk = 2 * px + py
            return pltpu.make_async_remote_copy(
                src_ref=dst[a].at[blk, rows(a, c)], dst_ref=dst[a].at[blk, rows(a, c)], send_sem=send.at[a * 3 + j],
                recv_sem=recv.at[a * 3 + j], device_id=(x, y, c), device_id_type=MESH)

        def passed(a, j, core):
            px, py = chip(j)
            blk = 2 * px + py
            return pltpu.make_async_remote_copy(
                src_ref=dst[a].at[blk, rows(a, core)], dst_ref=dst[a].at[blk, rows(a, core)], send_sem=fsend.at[a * 3 + j],
                recv_sem=frecv.at[a * 3 + j], device_id=(x, y, 1 - c), device_id_type=MESH)

        return c, out, landed, passed

    def start(src, dst, sems):
        _, out, _, _ = copies(src, dst, sems)
        for a, j in pairs:
            out(a, j).start()

    def finish(src, dst, sems):
        c, out, landed, passed = copies(src, dst, sems)
        for a, j in pairs:
            landed(a, j).wait_recv()
            passed(a, j, c).start()
        for a, j in pairs:
            passed(a, j, 1 - c).wait_recv()
        for a, j in pairs:
            out(a, j).wait_send()
            passed(a, j, c).wait_send()

    shapes = [jax.ShapeDtypeStruct(s.shape, s.dtype) for s in slots]
    return _Comm(slots, shapes, {a: a for a in range(n)}, [3 * n] * 4, start, finish)


def _pair_exchange_comm(grads):
    n = len(grads)

    def copy(src, dst, sems, a):
        x, y, c = _place()
        half = grads[a].shape[1] // 2
        return pltpu.make_async_remote_copy(
            src_ref=src[a].at[:, pl.ds((1 - c) * half, half)], dst_ref=dst[a], send_sem=sems[0].at[a],
            recv_sem=sems[1].at[a], device_id=(x, y, 1 - c), device_id_type=MESH)

    def start(src, dst, sems):
        for a in range(n):
            copy(src, dst, sems, a).start()

    def finish(src, dst, sems):
        for a in range(n):
            copy(src, dst, sems, a).wait()

    shapes = [jax.ShapeDtypeStruct((g.shape[0], g.shape[1] // 2, g.shape[2]), g.dtype) for g in grads]
    return _Comm(grads, shapes, {}, [n, n], start, finish)


def _pair_add(mine, theirs, name):
    nb, rows, cols = theirs.shape
    tr, tc = _fit(rows, 704, 16), _fit(cols, MM_TILE)
    n_r = rows // tr

    def body(a_ref, b_ref, lo_ref, own_ref):
        s = a_ref[...] + b_ref[...]
        lo_ref[...] = s.astype(BF16)

        @pl.when(pl.program_id(2) == 2 * lax.axis_index("x") + lax.axis_index("y"))
        def _():
            own_ref[...] = s

    return pl.pallas_call(
        body, name=name, grid=(n_r, cols // tc, nb),
        in_specs=[pl.BlockSpec((None, tr, tc), lambda i, j, b: (b, lax.axis_index("c") * n_r + i, j)),
                  pl.BlockSpec((None, tr, tc), lambda i, j, b: (b, i, j))],
        out_specs=(pl.BlockSpec((None, tr, tc), lambda i, j, b: (b, i, j)), pl.BlockSpec((tr, tc), lambda i, j, b: (i, j))),
        out_shape=(jax.ShapeDtypeStruct(theirs.shape, BF16), jax.ShapeDtypeStruct((rows, cols), F32)),
        compiler_params=_params("parallel", "parallel", "arbitrary"),
    )(mine, theirs)


def _chip_exchange_comm(sums, js=(0, 1, 2), into=None):
    n = len(sums)
    pairs = [(a, j) for a in range(n) for j in js]

    def copy(src, dst, sems, a, j):
        x, y, c = _place()
        px, py = _flip(x, OTHER_CHIPS[j][0]), _flip(y, OTHER_CHIPS[j][1])
        return pltpu.make_async_remote_copy(
            src_ref=src[a].at[2 * px + py], dst_ref=dst[a].at[j], send_sem=sems[0].at[a * 3 + j],
            recv_sem=sems[1].at[a * 3 + j], device_id=(px, py, c), device_id_type=MESH)

    def start(src, dst, sems):
        for a, j in pairs:
            copy(src, dst, sems, a, j).start()

    def finish(src, dst, sems):
        for a, j in pairs:
            copy(src, dst, sems, a, j).wait()

    shapes = [jax.ShapeDtypeStruct((3,) + s.shape[1:], s.dtype) for s in sums]
    return _Comm(list(sums) + list(into or []), shapes, {n + a: a for a in range(n)} if into else {}, [3 * n, 3 * n],
                 start, finish)


def _chip_add(own, got, name):
    rows, cols = own.shape
    tr, tc = _fit(rows, 704, 16), _fit(cols, MM_TILE)
    n_r = rows // tr

    def body(o_ref, g_ref, out_ref):
        out_ref[...] = ((o_ref[...] + g_ref[0].astype(F32)) + g_ref[1].astype(F32)) + g_ref[2].astype(F32)

    return pl.pallas_call(
        body, name=name, out_shape=jax.ShapeDtypeStruct((2 * rows, cols), F32), grid=(n_r, cols // tc),
        in_specs=[pl.BlockSpec((tr, tc), lambda i, j: (i, j)), pl.BlockSpec((3, tr, tc), lambda i, j: (0, i, j))],
        out_specs=pl.BlockSpec((tr, tc), lambda i, j: (lax.axis_index("c") * n_r + i, j)),
        compiler_params=_params("parallel", "parallel"),
    )(own, got)


def _pair_share_comm(grads):
    n = len(grads)

    def copy(src, dst, sems, a, mine):
        x, y, c = _place()
        half = grads[a].shape[0] // 2
        rows = pl.ds((c if mine else 1 - c) * half, half)
        return pltpu.make_async_remote_copy(
            src_ref=src[a].at[rows], dst_ref=dst[a].at[rows], send_sem=sems[0].at[a], recv_sem=sems[1].at[a],
            device_id=(x, y, 1 - c), device_id_type=MESH)

    def start(src, dst, sems):
        for a in range(n):
            copy(src, dst, sems, a, True).start()

    def finish(src, dst, sems):
        for a in range(n):
            copy(src, dst, sems, a, False).wait_recv()
            copy(src, dst, sems, a, True).wait_send()

    shapes = [jax.ShapeDtypeStruct(g.shape, g.dtype) for g in grads]
    return _Comm(grads, shapes, {a: a for a in range(n)}, [n, n], start, finish)


def _all_sum_small(vec):
    r, n = vec.shape
    flips = [(a, b, d) for a in (0, 1) for b in (0, 1) for d in (0, 1)][1:]

    def body(v_ref, o_ref, buf, send, recv):
        x, y, c = _place()
        me = 4 * x + 2 * y + c
        cps = []
        for k, (fx, fy, fc) in enumerate(flips):
            cp = pltpu.make_async_remote_copy(
                src_ref=v_ref, dst_ref=buf.at[me], send_sem=send.at[k], recv_sem=recv.at[k],
                device_id=(_flip(x, fx), _flip(y, fy), _flip(c, fc)), device_id_type=MESH)
            cp.start()
            cps.append(cp)
        buf[me] = v_ref[...]
        for k, (fx, fy, fc) in enumerate(flips):
            peer = 4 * _flip(x, fx) + 2 * _flip(y, fy) + _flip(c, fc)
            pltpu.make_async_remote_copy(
                src_ref=v_ref, dst_ref=buf.at[peer], send_sem=send.at[k], recv_sem=recv.at[k],
                device_id=(x, y, c), device_id_type=MESH).wait_recv()
        for cp in cps:
            cp.wait_send()
        acc = buf[0]
        for d in range(1, 8):
            acc = acc + buf[d]
        o_ref[...] = acc

    return pl.pallas_call(
        body, name="all_sum_small", out_shape=jax.ShapeDtypeStruct((r, n), F32),
        in_specs=[pl.BlockSpec(memory_space=pltpu.VMEM)], out_specs=pl.BlockSpec(memory_space=pltpu.VMEM),
        scratch_shapes=[pltpu.VMEM((8, r, n), F32), pltpu.SemaphoreType.DMA((7,)), pltpu.SemaphoreType.DMA((7,))],
    )(vec)


SMALL = ("attn_pre_g", "q_norm_g", "kv_norm_g", "swa_sinks", "grp_a_g", "grp_b_g", "attn_post_g", "ffn_pre_g", "ffn_post_g")
BIG = ("w_in", "w_uq", "w_ukv", "w_o", "w_gate", "w_up", "w_down")
ORDER = ("attn_pre_g", "w_in", "q_norm_g", "w_uq", "kv_norm_g", "w_ukv", "swa_sinks", "grp_a_g", "grp_b_g", "w_o",
         "attn_post_g", "ffn_pre_g", "w_gate", "w_up", "w_down", "ffn_post_g")


def _pad_lanes(v):
    n = v.shape[1]
    return jnp.pad(v, ((0, 0), (0, -n % LANES)))


def kernel(x, positions, attn_pre_g, w_in, q_norm_g, w_uq, kv_norm_g, w_ukv, swa_sinks, grp_a_g, grp_b_g, w_o, attn_post_g, ffn_pre_g, w_gate, w_up, w_down, ffn_post_g, loss_target, m_attn_pre_g, m_w_in, m_q_norm_g, m_w_uq, m_kv_norm_g, m_w_ukv, m_swa_sinks, m_grp_a_g, m_grp_b_g, m_w_o, m_attn_post_g, m_ffn_pre_g, m_w_gate, m_w_up, m_w_down, m_ffn_post_g, v_attn_pre_g, v_w_in, v_q_norm_g, v_w_uq, v_kv_norm_g, v_w_ukv, v_swa_sinks, v_grp_a_g, v_grp_b_g, v_w_o, v_attn_post_g, v_ffn_pre_g, v_w_gate, v_w_up, v_w_down, v_ffn_post_g):
    given = dict(locals())
    w32 = {k: given[k][0] for k in BIG}
    gains = {k: given[k] for k in SMALL}
    xs, tgt = x[0], loss_target[0]
    seq, d_model = xs.shape
    q_rank, kv_rank = q_norm_g.shape[1], kv_norm_g.shape[1]
    kvw = SWA_KV_HEADS * SWA_HEAD_DIM

    slot = {k: _cast_into_slot(w32[k], "cast_" + k) for k in ("w_in", "w_uq", "w_ukv", "w_o")}
    full = {}
    slot["w_gate"], (part,) = _cast_into_slot(w32["w_gate"], "cast_w_gate", comm=_gather_comm([slot["w_in"]], js=(0, 1)))
    slot["w_up"], (full["w_in"],) = _cast_into_slot(w32["w_up"], "cast_w_up", comm=_gather_comm([part], js=(2,)))
    slot["w_down"] = _cast_into_slot(w32["w_down"], "cast_w_down")

    pos = positions[0]
    inv = 1.0 / (ROPE_THETA ** (jnp.arange(0, MLA_ROPE, 2, dtype=F32) / MLA_ROPE))
    ang = pos.astype(F32)[:, None] * inv
    cos, sin = jnp.cos(ang), jnp.sin(ang)
    pos_col, pos_row = pos[:, None], pos[None, :]
    sinks = swa_sinks[0]

    a = _norm_fwd(xs, attn_pre_g, BF16, "attn_pre_norm")
    proj4, (full["w_uq"], full["w_ukv"], full["w_o"]) = _matmul(
        a, full["w_in"], name="in_proj", comm=_gather_comm([slot["w_uq"], slot["w_ukv"], slot["w_o"]]))
    proj = jnp.concatenate([proj4[b] for b in range(N_CHIPS)], axis=1)
    cuts = (0, q_rank, q_rank + kv_rank, q_rank + kv_rank + MLA_ROPE)
    cuts = cuts + (cuts[3] + MIX_B, cuts[3] + MIX_B + kvw, cuts[3] + MIX_B + 2 * kvw)
    c_q, c_kv, k_rope, q_s, k_s, v_s = (proj[:, lo:hi] for lo, hi in zip(cuts[:-1], cuts[1:]))
    cqn = _norm_fwd(c_q, q_norm_g, BF16, "q_norm")
    ckvn = _norm_fwd(c_kv, kv_norm_g, BF16, "kv_norm")
    q4 = _matmul(cqn, full["w_uq"], name="q_up")
    kv4 = _matmul(ckvn, full["w_ukv"], name="kv_up")
    qh, kh, vh = _mla_prep(q4, kv4, k_rope, cos, sin)
    o_a, lse_a, (full["w_gate"],) = _mla_fwd(qh, kh, vh, comm=_gather_comm([slot["w_gate"]]))
    q_sb, k_sb, v_sb = q_s.astype(BF16), k_s.astype(BF16), v_s.astype(BF16)
    o_b, lse_b = _swa_fwd(q_sb, k_sb, v_sb, pos_col, pos_row, sinks)
    mix = _mix_fwd(o_a, o_b, grp_a_g, grp_b_g)
    w_o_full = full["w_o"].reshape(N_CHIPS * full["w_o"].shape[1], d_model)
    ao = _matmul(mix, w_o_full, name="out_proj")
    h1, f = _post_pre_fwd(xs, ao, attn_post_g, ffn_pre_g)
    gate, (full["w_up"],) = _matmul(f, full["w_gate"], name="ffn_gate", comm=_gather_comm([slot["w_up"]]))
    (up, act), (full["w_down"],) = _matmul(
        f, full["w_up"], name="ffn_up", comm=_gather_comm([slot["w_down"]]), epilogue=_swiglu_fwd_tile, extras=(gate,),
        out_dtype=(F32, BF16), tm=512)
    dn = _matmul(act, full["w_down"], reduce_b=True, name="ffn_down")
    loss_row, dy, ddn, d_ffn_post = _loss_bwd(h1, dn, ffn_post_g, tgt)

    low, own, got = {}, {}, {}
    dw_down = _matmul(act, ddn, ta=True, name="ffn_down_dw", tn=1024, tk=512)
    (dgate, dup), (theirs,) = _matmul(
        ddn, full["w_down"], tb=True, name="ffn_down_dx", epilogue=_swiglu_bwd_tile, extras=(gate, up),
        out_dtype=(BF16, BF16), tm=512, tk=512, comm=_pair_exchange_comm([dw_down]))
    low["w_down"], own["w_down"] = _pair_add(dw_down, theirs, "pair_add_w_down")
    dw_gate, (got["w_down"],) = _matmul(f, dgate, ta=True, name="ffn_gate_dw", tk=512,
                                        comm=_chip_exchange_comm([low["w_down"]]))
    dw_up, (theirs,) = _matmul(f, dup, ta=True, name="ffn_up_dw", tk=512, comm=_pair_exchange_comm([dw_gate]))
    low["w_gate"], own["w_gate"] = _pair_add(dw_gate, theirs, "pair_add_w_gate")
    grads4 = dict(w_down=dw_down, w_gate=dw_gate, w_up=dw_up)
    df_g, (got["w_gate"], theirs) = _matmul(
        dgate, full["w_gate"], tb=True, reduce_b=True, name="ffn_gate_dx",
        comm=_join(_chip_exchange_comm([low["w_gate"]]), _pair_exchange_comm([dw_up])))
    low["w_up"], own["w_up"] = _pair_add(dw_up, theirs, "pair_add_w_up")
    df_u = _matmul(dup, full["w_up"], tb=True, reduce_b=True, name="ffn_up_dx")
    dh1, d_ffn_pre = _norm_bwd(h1, ffn_pre_g, [df_g, df_u], [dy], F32, "ffn_pre_norm_bwd")
    dao, d_attn_post = _norm_bwd(ao, attn_post_g, [dh1], [], BF16, "attn_post_norm_bwd")
    dmix = _matmul(dao, w_o_full, tb=True, name="out_proj_dx")
    dw_o = _matmul(mix, dao, ta=True, name="out_proj_dw")
    do_a, do_b, d_grp_a, d_grp_b, dl_a, dl_b = _mix_bwd(o_a, o_b, grp_a_g, grp_b_g, dmix)
    dl_a = dl_a.T[:, :, None]
    dqh, dkh, dvh, (got["w_up"],) = _mla_bwd(qh, kh, vh, do_a, lse_a, dl_a, comm=_chip_exchange_comm([low["w_up"]]))
    dq4, dkv4, dk_rope = _mla_unprep(dqh, dkh, dvh, cos, sin)
    dw_uq = _matmul(cqn, dq4, ta=True, name="q_up_dw")
    dcqn = _matmul(dq4, full["w_uq"], tb=True, reduce_b=True, name="q_up_dx")
    dw_ukv = _matmul(ckvn, dkv4, ta=True, name="kv_up_dw")
    dckvn = _matmul(dkv4, full["w_ukv"], tb=True, reduce_b=True, name="kv_up_dx")
    dc_q, d_q_norm = _norm_bwd(c_q, q_norm_g, [dcqn], [], BF16, "q_norm_bwd")
    dc_kv, d_kv_norm = _norm_bwd(c_kv, kv_norm_g, [dckvn], [], BF16, "kv_norm_bwd")
    dq_s, dk_cur, dk_prev, dv_cur, dv_prev, d_sinks = _swa_bwd(q_sb, k_sb, v_sb, pos_col, pos_row, sinks, do_b, lse_b, dl_b)
    dk_s = _band_merge(dk_cur, dk_prev, "swa_dk_merge")
    dv_s = _band_merge(dv_cur, dv_prev, "swa_dv_merge")
    dproj = jnp.concatenate([dc_q, dc_kv, dk_rope.astype(BF16), dq_s, dk_s, dv_s], axis=1)
    blk_w = dproj.shape[1] // N_CHIPS
    dproj4 = jnp.stack([dproj[:, b * blk_w:(b + 1) * blk_w] for b in range(N_CHIPS)])
    mid = ("w_o", "w_uq", "w_ukv")
    grads4.update(w_o=dw_o.reshape(N_CHIPS, -1, d_model), w_uq=dw_uq, w_ukv=dw_ukv)
    dw_in, theirs = _matmul(a, dproj4, ta=True, name="in_proj_dw", tm=512,
                            comm=_pair_exchange_comm([grads4[k] for k in mid]))
    for k, t in zip(mid, theirs):
        low[k], own[k] = _pair_add(grads4[k], t, "pair_add_" + k)
    da, (*got_mid, theirs_in) = _matmul(
        dproj4, full["w_in"], tb=True, reduce_b=True, name="in_proj_dx",
        comm=_join(_chip_exchange_comm([low[k] for k in mid]), _pair_exchange_comm([dw_in])))
    got.update(zip(mid, got_mid))
    six = tuple(k for k in BIG if k != "w_in")
    halves = [_chip_add(own[k], got[k], "chip_add_" + k) for k in six]
    dx, d_attn_pre, shared = _norm_bwd(xs, attn_pre_g, [da], [dh1], F32, "attn_pre_norm_bwd", comm=_pair_share_comm(halves))
    g_big = dict(zip(six, shared))

    small_grads = dict(attn_pre_g=d_attn_pre, q_norm_g=d_q_norm, kv_norm_g=d_kv_norm, swa_sinks=d_sinks, grp_a_g=d_grp_a,
                       grp_b_g=d_grp_b, attn_post_g=d_attn_post, ffn_pre_g=d_ffn_pre, ffn_post_g=d_ffn_post)
    parts = [loss_row] + [_pad_lanes(small_grads[k]) for k in SMALL]
    packed = jnp.concatenate(parts, axis=1)
    n_packed = packed.shape[1]
    packed = jnp.pad(packed, ((0, 0), (0, -n_packed % (8 * LANES)))).reshape(8, -1)
    total8 = _all_sum_small(packed)
    total = total8.reshape(1, -1)
    loss = total[0, 0]
    g_small, off = {}, LANES
    for k in SMALL:
        n = gains[k].shape[1]
        g_small[k] = total[:, off:off + n]
        off += n + (-n % LANES)

    def pack_small(prefix):
        flat = jnp.concatenate([jnp.zeros((1, LANES), F32)] + [_pad_lanes(given[prefix + k]) for k in SMALL], axis=1)
        return jnp.pad(flat, ((0, 0), (0, -n_packed % (8 * LANES)))).reshape(8, -1)

    d_sm, m_sm, v_sm = (r.reshape(1, -1) for r in
                        _adamw(pack_small(""), total8, pack_small("m_"), pack_small("v_"), "adamw_small"))
    delta, new_m, new_v, off = {}, {}, {}, LANES
    for k in SMALL:
        n = gains[k].shape[1]
        delta[k], new_m[k], new_v[k] = d_sm[:, off:off + n], m_sm[:, off:off + n], v_sm[:, off:off + n]
        off += n + (-n % LANES)

    low["w_in"], own["w_in"] = _pair_add(dw_in, theirs_in, "pair_add_w_in")
    g_out = {}

    def adam(k, comm=None):
        res = _adamw(w32[k], g_big[k], given["m_" + k][0], given["v_" + k][0], "adamw_" + k, emit_grad=True, comm=comm)
        delta[k], new_m[k], new_v[k], g_out[k] = res[:4]
        return res[4] if comm else None

    (part,) = adam("w_gate", _chip_exchange_comm([low["w_in"]], js=(0, 1)))
    (got["w_in"],) = adam("w_up", _chip_exchange_comm([low["w_in"]], js=(2,), into=[part]))
    for k in ("w_down", "w_o", "w_uq", "w_ukv"):
        adam(k)
    half_in = _chip_add(own["w_in"], got["w_in"], "chip_add_w_in")
    (g_big["w_in"],) = _run_comm(_pair_share_comm([half_in]), "grad_pair_share")
    adam("w_in")

    def out(d, k):
        return d[k][None] if k in BIG else d[k]

    grads = {**g_small, **g_out}
    return (loss, dx[None], *[out(grads, k) for k in ORDER], *[out(delta, k) for k in ORDER],
            *[out(new_m, k) for k in ORDER], *[out(new_v, k) for k in ORDER])
```

```python
import functools
import math

import jax
import jax.numpy as jnp
from jax import lax
from jax.experimental import pallas as pl
from jax.experimental.pallas import tpu as pltpu

F32, BF16 = jnp.float32, jnp.bfloat16
MESH = pl.DeviceIdType.MESH
ANY = pl.BlockSpec(memory_space=pl.ANY)

N_CHIPS = 4
EPS = 1e-6
MLA_HEADS, MLA_NOPE, MLA_ROPE, MLA_V = 16, 128, 64, 128
MLA_QK = MLA_NOPE + MLA_ROPE
HEADS_PER_CHIP = MLA_HEADS // N_CHIPS
ROPE_THETA = 10000.0
SWA_Q_HEADS, SWA_KV_HEADS, SWA_HEAD_DIM, SWA_WINDOW = 32, 8, 64, 128
SWA_GROUP = SWA_Q_HEADS // SWA_KV_HEADS
MIX_A, MIX_B = MLA_HEADS * MLA_V, SWA_Q_HEADS * SWA_HEAD_DIM
MASK_VALUE = float(jnp.finfo(jnp.float32).min)
ADAM_LR, ADAM_B1, ADAM_B2, ADAM_EPS, ADAM_WD, ADAM_STEP = 0.001, 0.9, 0.999, 1e-08, 0.01, 10

LANES = 128
VMEM_LIMIT = 56 << 20
ATTN_TILE = 512
ROW_TILE = 256
MM_TILE = 1024


def _fit(dim, pref, mult=LANES):
    if dim <= pref:
        return dim
    for t in range(pref - pref % mult, 0, -mult):
        if dim % t == 0:
            return t
    return dim


def _params(*semantics):
    return pltpu.CompilerParams(dimension_semantics=semantics, vmem_limit_bytes=VMEM_LIMIT)


def _dot(a, b, ca, cb):
    return lax.dot_general(a, b, (((ca,), (cb,)), ((), ())), preferred_element_type=F32)


def _matmul(a, b, *, name, ta=False, tb=False, reduce_b=False, out_dtype=F32, tm=MM_TILE, tn=MM_TILE, tk=MM_TILE, comm=None,
            epilogue=None, extras=()):
    a3, b3 = a.ndim == 3, b.ndim == 3
    nb = a.shape[0] if a3 else (b.shape[0] if b3 else 1)
    (K, M) = a.shape[-2:] if ta else a.shape[-2:][::-1]
    (N, K2) = b.shape[-2:] if tb else b.shape[-2:][::-1]
    assert K == K2, (a.shape, b.shape)
    tm, tn, tk = _fit(M, tm), _fit(N, tn), _fit(K, tk)
    batched_out = (a3 or b3) and not reduce_b
    n_bo = nb if batched_out else 1
    n_br = nb if reduce_b else 1
    nk = K // tk

    def sel(bo, br):
        return br if reduce_b else bo

    def a_map(bo, i, j, br, k):
        t = (k, i) if ta else (i, k)
        return (sel(bo, br),) + t if a3 else t

    def b_map(bo, i, j, br, k):
        t = (j, k) if tb else (k, j)
        return (sel(bo, br),) + t if b3 else t

    def o_map(bo, i, j, br, k):
        return (bo, i, j) if batched_out else (i, j)

    a_blk = (tk, tm) if ta else (tm, tk)
    b_blk = (tn, tk) if tb else (tk, tn)
    a_blk = (None,) + a_blk if a3 else a_blk
    b_blk = (None,) + b_blk if b3 else b_blk
    o_blk = (None, tm, tn) if batched_out else (tm, tn)
    o_shape = (nb, M, N) if batched_out else (M, N)

    grid = (n_bo, M // tm, N // tn, n_br, nk)
    n_ci, n_co = (len(comm.ins), len(comm.out_shapes)) if comm else (0, 0)
    n_x = len(extras)
    out_dtypes = tuple(out_dtype) if epilogue else (out_dtype,)
    n_o = len(out_dtypes)

    def body(*refs):
        a_ref, b_ref = refs[:2]
        x_refs, c_in = refs[2:2 + n_x], refs[2 + n_x:2 + n_x + n_ci]
        refs = refs[2 + n_x + n_ci:]
        o_refs, c_out, acc_ref, sems = refs[:n_o], refs[n_o:n_o + n_co], refs[n_o + n_co], refs[n_o + n_co + 1:]
        br, k = pl.program_id(3), pl.program_id(4)
        first, last = _grid_ends(grid)
        if comm:
            pl.when(first)(lambda: comm.start(c_in, c_out, sems))

        @pl.when((br == 0) & (k == 0))
        def _():
            acc_ref[...] = jnp.zeros_like(acc_ref)

        acc_ref[...] += _dot(a_ref[...], b_ref[...], 0 if ta else 1, 1 if tb else 0)

        @pl.when((br == n_br - 1) & (k == nk - 1))
        def _():
            vals = epilogue(acc_ref[...], *[r[...] for r in x_refs]) if epilogue else (acc_ref[...],)
            for o_ref, v in zip(o_refs, vals):
                o_ref[...] = v.astype(o_ref.dtype)

        if comm:
            pl.when(last)(lambda: comm.finish(c_in, c_out, sems))

    o_spec = pl.BlockSpec(o_blk, o_map)
    res = pl.pallas_call(
        body, name=name,
        out_shape=tuple(jax.ShapeDtypeStruct(o_shape, d) for d in out_dtypes) + tuple(comm.out_shapes if comm else ()),
        grid=grid,
        in_specs=[pl.BlockSpec(a_blk, a_map), pl.BlockSpec(b_blk, b_map)] + [o_spec] * n_x + [ANY] * n_ci,
        out_specs=(o_spec,) * n_o + (ANY,) * n_co,
        input_output_aliases={2 + n_x + i: n_o + o for i, o in comm.aliases.items()} if comm else {},
        scratch_shapes=[pltpu.VMEM((tm, tn), F32)] + (comm.scratch if comm else []),
        compiler_params=_params(*(["parallel"] * 3 + ["arbitrary"] * 2 if not comm else ["arbitrary"] * 5)),
    )(a, b, *extras, *(comm.ins if comm else ()))
    main = res[:n_o] if epilogue else res[0]
    return (main, res[n_o:]) if comm else main


def _inv_rms(u):
    return lax.rsqrt(jnp.mean(u * u, axis=-1, keepdims=True) + EPS)


def _norm_bwd_math(u, g, dz):
    r = _inv_rms(u)
    w = dz * g
    du = r * w - u * (r * r * r * jnp.mean(w * u, axis=-1, keepdims=True))
    dg = jnp.sum(dz * (u * r), axis=0, keepdims=True)
    return du, dg


def _row_spec(tr, n):
    return pl.BlockSpec((tr, n), lambda i: (i, 0))


def _gain_spec(n):
    return pl.BlockSpec((1, n), lambda i: (0, 0))


def _norm_fwd(u, g, out_dtype, name):
    rows, n = u.shape
    tr = _fit(rows, ROW_TILE, 16)

    def body(u_ref, g_ref, o_ref):
        x = u_ref[...]
        o_ref[...] = (x * _inv_rms(x) * g_ref[...]).astype(o_ref.dtype)

    return pl.pallas_call(
        body, name=name, out_shape=jax.ShapeDtypeStruct((rows, n), out_dtype), grid=(rows // tr,),
        in_specs=[_row_spec(tr, n), _gain_spec(n)], out_specs=_row_spec(tr, n), compiler_params=_params("parallel"),
    )(u, g)


def _norm_bwd(u, g, dzs, adds, out_dtype, name, comm=None):
    rows, n = u.shape
    tr = _fit(rows, ROW_TILE, 16)
    n_dz, n_add = len(dzs), len(adds)
    grid = (rows // tr,)

    def body(ins, outs):
        u_ref, g_ref = ins[:2]
        dz_refs, add_refs = ins[2:2 + n_dz], ins[2 + n_dz:]
        du_ref, dg_ref = outs
        dz = dz_refs[0][...].astype(F32)
        for r in dz_refs[1:]:
            dz = dz + r[...].astype(F32)
        du, dg = _norm_bwd_math(u_ref[...], g_ref[...], dz)
        for r in add_refs:
            du = du + r[...]
        du_ref[...] = du.astype(du_ref.dtype)

        @pl.when(pl.program_id(0) == 0)
        def _():
            dg_ref[...] = jnp.zeros_like(dg_ref)

        dg_ref[...] += dg

    full, extra = _carry(body, comm, 2 + n_dz + n_add, 2, grid)
    res = pl.pallas_call(
        full, name=name,
        out_shape=(jax.ShapeDtypeStruct((rows, n), out_dtype), jax.ShapeDtypeStruct((1, n), F32)) + extra["out_shape"],
        grid=grid,
        in_specs=[_row_spec(tr, n), _gain_spec(n)] + [_row_spec(tr, n)] * (n_dz + n_add) + extra["in_specs"],
        out_specs=(_row_spec(tr, n), _gain_spec(n)) + extra["out_specs"],
        input_output_aliases=extra["aliases"], scratch_shapes=extra["scratch"], compiler_params=_params("arbitrary"),
    )(u, g, *dzs, *adds, *extra["operands"])
    return (res[0], res[1], res[2:]) if comm else (res[0], res[1])


def _mix_fwd(o_a, o_b, g_a, g_b):
    rows = o_a.shape[0]
    tr = _fit(rows, ROW_TILE, 16)

    def body(a_ref, b_ref, ga_ref, gb_ref, o_ref):
        a, b = a_ref[...], b_ref[...]
        o_ref[:, :MIX_A] = (a * _inv_rms(a) * ga_ref[...]).astype(BF16)
        o_ref[:, MIX_A:] = (b * _inv_rms(b) * gb_ref[...]).astype(BF16)

    return pl.pallas_call(
        body, name="mix_fwd", out_shape=jax.ShapeDtypeStruct((rows, MIX_A + MIX_B), BF16), grid=(rows // tr,),
        in_specs=[_row_spec(tr, MIX_A), _row_spec(tr, MIX_B), _gain_spec(MIX_A), _gain_spec(MIX_B)],
        out_specs=_row_spec(tr, MIX_A + MIX_B), compiler_params=_params("parallel"),
    )(o_a, o_b, g_a, g_b)


def _mix_bwd(o_a, o_b, g_a, g_b, dmix):
    rows = o_a.shape[0]
    tr = _fit(rows, ROW_TILE, 16)

    def body(a_ref, b_ref, ga_ref, gb_ref, dm_ref, doa_ref, dob_ref, dga_ref, dgb_ref, dla_ref, dlb_ref):
        a, b = a_ref[...], b_ref[...]
        doa, dga = _norm_bwd_math(a, ga_ref[...], dm_ref[:, :MIX_A])
        dob, dgb = _norm_bwd_math(b, gb_ref[...], dm_ref[:, MIX_A:])
        doa_ref[...] = doa.astype(BF16)
        dob_ref[...] = dob.astype(BF16)
        pa, pb = doa * a, dob * b
        for h in range(MLA_HEADS):
            dla_ref[:, h:h + 1] = jnp.sum(pa[:, h * MLA_V:(h + 1) * MLA_V], axis=-1, keepdims=True)
        for h in range(SWA_Q_HEADS):
            dlb_ref[:, h:h + 1] = jnp.sum(pb[:, h * SWA_HEAD_DIM:(h + 1) * SWA_HEAD_DIM], axis=-1, keepdims=True)

        @pl.when(pl.program_id(0) == 0)
        def _():
            dga_ref[...] = jnp.zeros_like(dga_ref)
            dgb_ref[...] = jnp.zeros_like(dgb_ref)

        dga_ref[...] += dga
        dgb_ref[...] += dgb

    return pl.pallas_call(
        body, name="mix_bwd",
        out_shape=(jax.ShapeDtypeStruct((rows, MIX_A), BF16), jax.ShapeDtypeStruct((rows, MIX_B), BF16),
                   jax.ShapeDtypeStruct((1, MIX_A), F32), jax.ShapeDtypeStruct((1, MIX_B), F32),
                   jax.ShapeDtypeStruct((rows, MLA_HEADS), F32), jax.ShapeDtypeStruct((rows, SWA_Q_HEADS), F32)),
        grid=(rows // tr,),
        in_specs=[_row_spec(tr, MIX_A), _row_spec(tr, MIX_B), _gain_spec(MIX_A), _gain_spec(MIX_B),
                  _row_spec(tr, MIX_A + MIX_B)],
        out_specs=(_row_spec(tr, MIX_A), _row_spec(tr, MIX_B), _gain_spec(MIX_A), _gain_spec(MIX_B),
                   _row_spec(tr, MLA_HEADS), _row_spec(tr, SWA_Q_HEADS)),
        compiler_params=_params("arbitrary"),
    )(o_a, o_b, g_a, g_b, dmix)


def _post_pre_fwd(x, ao, g_post, g_pre):
    rows, n = x.shape
    tr = _fit(rows, ROW_TILE, 16)

    def body(x_ref, ao_ref, g1_ref, g2_ref, h_ref, f_ref):
        u = ao_ref[...]
        h = x_ref[...] + u * _inv_rms(u) * g1_ref[...]
        h_ref[...] = h
        f_ref[...] = (h * _inv_rms(h) * g2_ref[...]).astype(BF16)

    return pl.pallas_call(
        body, name="post_pre_fwd",
        out_shape=(jax.ShapeDtypeStruct((rows, n), F32), jax.ShapeDtypeStruct((rows, n), BF16)), grid=(rows // tr,),
        in_specs=[_row_spec(tr, n), _row_spec(tr, n), _gain_spec(n), _gain_spec(n)],
        out_specs=(_row_spec(tr, n), _row_spec(tr, n)), compiler_params=_params("parallel"),
    )(x, ao, g_post, g_pre)


def _loss_bwd(h1, dn, g_post, target):
    rows, n = h1.shape
    tr = _fit(rows, ROW_TILE, 16)

    def body(h_ref, u_ref, g_ref, t_ref, loss_ref, dy_ref, du_ref, dg_ref):
        u, g = u_ref[...], g_ref[...]
        err = h_ref[...] + u * _inv_rms(u) * g - t_ref[...]
        dy = err / n
        dy_ref[...] = dy
        du, dg = _norm_bwd_math(u, g, dy)
        du_ref[...] = du.astype(BF16)

        @pl.when(pl.program_id(0) == 0)
        def _():
            loss_ref[...] = jnp.zeros_like(loss_ref)
            dg_ref[...] = jnp.zeros_like(dg_ref)

        loss_ref[...] += jnp.full((1, LANES), 0.5 * jnp.sum(jnp.mean(err * err, axis=-1)), F32)
        dg_ref[...] += dg

    return pl.pallas_call(
        body, name="loss_bwd",
        out_shape=(jax.ShapeDtypeStruct((1, LANES), F32), jax.ShapeDtypeStruct((rows, n), F32),
                   jax.ShapeDtypeStruct((rows, n), BF16), jax.ShapeDtypeStruct((1, n), F32)),
        grid=(rows // tr,),
        in_specs=[_row_spec(tr, n), _row_spec(tr, n), _gain_spec(n), _row_spec(tr, n)],
        out_specs=(_gain_spec(LANES), _row_spec(tr, n), _row_spec(tr, n), _gain_spec(n)),
        compiler_params=_params("arbitrary"),
    )(h1, dn, g_post, target)


def _blk3_spec(tr, n):
    return pl.BlockSpec((None, tr, n), lambda b, i: (b, i, 0))


def _swiglu_fwd_tile(up, gate):
    return up, gate * jax.nn.sigmoid(gate) * up


def _swiglu_bwd_tile(dact, gate, up):
    sig = jax.nn.sigmoid(gate)
    return dact * up * (sig * (1.0 + gate * (1.0 - sig))), dact * (gate * sig)


def _rope(x, cos, sin):
    half = MLA_ROPE // 2
    x1, x2 = x[:, :half], x[:, half:]
    return jnp.concatenate([x1 * cos - x2 * sin, x2 * cos + x1 * sin], axis=-1)


def _rope_t(d, cos, sin):
    half = MLA_ROPE // 2
    d1, d2 = d[:, :half], d[:, half:]
    return jnp.concatenate([d1 * cos + d2 * sin, d2 * cos - d1 * sin], axis=-1)


def _mla_prep(q4, kv4, k_rope, cos, sin):
    nb, rows, _ = q4.shape
    tr = _fit(rows, ROW_TILE, 16)
    hpc, half = HEADS_PER_CHIP, MLA_ROPE // 2

    def body(q_ref, kv_ref, kr_ref, c_ref, s_ref, qo_ref, ko_ref, vo_ref):
        cos, sin = c_ref[...], s_ref[...]
        k_pe = _rope(kr_ref[...], cos, sin)
        q_all, kv_all = q_ref[...], kv_ref[...]
        for h in range(hpc):
            q = q_all[:, h * MLA_QK:(h + 1) * MLA_QK]
            qo_ref[h] = jnp.concatenate([q[:, :MLA_NOPE], _rope(q[:, MLA_NOPE:], cos, sin)], axis=-1).astype(BF16)
            kv = kv_all[:, h * (MLA_NOPE + MLA_V):(h + 1) * (MLA_NOPE + MLA_V)]
            ko_ref[h] = jnp.concatenate([kv[:, :MLA_NOPE], k_pe], axis=-1).astype(BF16)
            vo_ref[h] = kv[:, MLA_NOPE:].astype(BF16)

    def head_spec(n):
        return pl.BlockSpec((hpc, tr, n), lambda b, i: (b, i, 0))

    def row_spec(n):
        return pl.BlockSpec((tr, n), lambda b, i: (i, 0))

    return pl.pallas_call(
        body, name="mla_prep",
        out_shape=(jax.ShapeDtypeStruct((MLA_HEADS, rows, MLA_QK), BF16), jax.ShapeDtypeStruct((MLA_HEADS, rows, MLA_QK), BF16),
                   jax.ShapeDtypeStruct((MLA_HEADS, rows, MLA_V), BF16)),
        grid=(nb, rows // tr),
        in_specs=[_blk3_spec(tr, hpc * MLA_QK), _blk3_spec(tr, hpc * (MLA_NOPE + MLA_V)), row_spec(MLA_ROPE),
                  row_spec(half), row_spec(half)],
        out_specs=(head_spec(MLA_QK), head_spec(MLA_QK), head_spec(MLA_V)),
        compiler_params=_params("parallel", "parallel"),
    )(q4, kv4, k_rope, cos, sin)


def _mla_unprep(dq, dk, dv, cos, sin):
    _, rows, _ = dq.shape
    tr = _fit(rows, ROW_TILE, 16)
    hpc, half = HEADS_PER_CHIP, MLA_ROPE // 2

    def body(dq_ref, dk_ref, dv_ref, c_ref, s_ref, q4_ref, kv4_ref, kr_ref):
        cos, sin = c_ref[...], s_ref[...]
        d_pe = jnp.zeros((tr, MLA_ROPE), F32)
        q_parts, kv_parts = [], []
        for h in range(hpc):
            g, gk = dq_ref[h], dk_ref[h]
            q_parts += [g[:, :MLA_NOPE], _rope_t(g[:, MLA_NOPE:], cos, sin)]
            kv_parts += [gk[:, :MLA_NOPE], dv_ref[h]]
            d_pe = d_pe + gk[:, MLA_NOPE:]
        q4_ref[...] = jnp.concatenate(q_parts, axis=-1).astype(BF16)
        kv4_ref[...] = jnp.concatenate(kv_parts, axis=-1).astype(BF16)

        @pl.when(pl.program_id(1) == 0)
        def _():
            kr_ref[...] = jnp.zeros_like(kr_ref)

        kr_ref[...] += _rope_t(d_pe, cos, sin)

    def head_spec(n):
        return pl.BlockSpec((hpc, tr, n), lambda i, b: (b, i, 0))

    def row_spec(n):
        return pl.BlockSpec((tr, n), lambda i, b: (i, 0))

    def blk_spec(n):
        return pl.BlockSpec((None, tr, n), lambda i, b: (b, i, 0))

    return pl.pallas_call(
        body, name="mla_unprep",
        out_shape=(jax.ShapeDtypeStruct((N_CHIPS, rows, hpc * MLA_QK), BF16),
                   jax.ShapeDtypeStruct((N_CHIPS, rows, hpc * (MLA_NOPE + MLA_V)), BF16),
                   jax.ShapeDtypeStruct((rows, MLA_ROPE), F32)),
        grid=(rows // tr, N_CHIPS),
        in_specs=[head_spec(MLA_QK), head_spec(MLA_QK), head_spec(MLA_V), row_spec(half), row_spec(half)],
        out_specs=(blk_spec(hpc * MLA_QK), blk_spec(hpc * (MLA_NOPE + MLA_V)), row_spec(MLA_ROPE)),
        compiler_params=_params("parallel", "arbitrary"),
    )(dq, dk, dv, cos, sin)


def _carry(body, comm, n_in, n_out, grid):
    n_ci, n_co = (len(comm.ins), len(comm.out_shapes)) if comm else (0, 0)

    def full(*refs):
        ins, c_in = refs[:n_in], refs[n_in:n_in + n_ci]
        outs = refs[n_in + n_ci:n_in + n_ci + n_out]
        c_out = refs[n_in + n_ci + n_out:n_in + n_ci + n_out + n_co]
        sems = refs[n_in + n_ci + n_out + n_co:]
        first, last = _grid_ends(grid)
        if comm:
            pl.when(first)(lambda: comm.start(c_in, c_out, sems))
        body(ins, outs)
        if comm:
            pl.when(last)(lambda: comm.finish(c_in, c_out, sems))

    extra = dict(
        operands=list(comm.ins) if comm else [], in_specs=[ANY] * n_ci, out_specs=(ANY,) * n_co,
        out_shape=tuple(comm.out_shapes) if comm else (),
        aliases={n_in + i: n_out + o for i, o in comm.aliases.items()} if comm else {},
        scratch=comm.scratch if comm else [])
    return full, extra


def _causal_keep(t):
    return lax.broadcasted_iota(jnp.int32, (t, t), 1) <= lax.broadcasted_iota(jnp.int32, (t, t), 0)


def _mla_fwd(q, k, v, comm=None):
    nh, rows, _ = q.shape
    t = _fit(rows, ATTN_TILE)
    scale = MLA_QK ** -0.5
    grid = (nh, rows // t)

    def body(ins, outs):
        (q_ref, k_ref, v_ref), (o_ref, lse_ref) = ins, outs
        i = pl.program_id(1)
        qb = q_ref[...]

        def step(j, carry, diagonal):
            m, l, acc = carry
            rows_j = pl.ds(pl.multiple_of(j * t, t), t)
            s = _dot(qb, k_ref[rows_j, :], 1, 1) * scale
            if diagonal:
                s = jnp.where(_causal_keep(t), s, MASK_VALUE)
            m_new = jnp.maximum(m, jnp.max(s, axis=-1, keepdims=True))
            alpha = jnp.exp(m - m_new)
            p = jnp.exp(s - m_new)
            l = alpha * l + jnp.sum(p, axis=-1, keepdims=True)
            acc = alpha * acc + _dot(p.astype(BF16), v_ref[rows_j, :], 1, 0)
            return m_new, l, acc

        init = (jnp.full((t, 1), MASK_VALUE, F32), jnp.zeros((t, 1), F32), jnp.zeros((t, MLA_V), F32))
        carry = lax.fori_loop(0, i, lambda j, c: step(j, c, False), init)
        m, l, acc = step(i, carry, True)
        o_ref[...] = acc / l
        lse_ref[...] = m + jnp.log(l)

    full, extra = _carry(body, comm, 3, 2, grid)
    res = pl.pallas_call(
        full, name="mla_fwd",
        out_shape=(jax.ShapeDtypeStruct((rows, nh * MLA_V), F32), jax.ShapeDtypeStruct((nh, rows, 1), F32)) + extra["out_shape"],
        grid=grid,
        in_specs=[pl.BlockSpec((None, t, MLA_QK), lambda h, i: (h, i, 0)),
                  pl.BlockSpec((None, rows, MLA_QK), lambda h, i: (h, 0, 0)),
                  pl.BlockSpec((None, rows, MLA_V), lambda h, i: (h, 0, 0))] + extra["in_specs"],
        out_specs=(pl.BlockSpec((t, MLA_V), lambda h, i: (i, h)), pl.BlockSpec((None, t, 1), lambda h, i: (h, i, 0))) + extra["out_specs"],
        input_output_aliases=extra["aliases"], scratch_shapes=extra["scratch"],
        compiler_params=_params("arbitrary", "arbitrary"),
    )(q, k, v, *extra["operands"])
    return res[0], res[1], res[2:]


def _mla_bwd(q, k, v, do, lse, delta, comm=None):
    nh, rows, _ = q.shape
    t = _fit(rows, ATTN_TILE)
    n_t = rows // t
    scale = MLA_QK ** -0.5
    grid = (nh, n_t)

    def body(ins, outs):
        (q_ref, k_ref, v_ref, do_ref, lse_ref, dl_ref), (dq_ref, dk_ref, dv_ref) = ins, outs
        j = pl.program_id(1)
        kb, vb = k_ref[...], v_ref[...]

        @pl.when(j == 0)
        def _():
            dq_ref[...] = jnp.zeros_like(dq_ref)

        def step(i, carry, diagonal):
            dk, dv = carry
            rows_i = pl.ds(pl.multiple_of(i * t, t), t)
            qb, dob = q_ref[rows_i, :], do_ref[rows_i, :]
            s = _dot(qb, kb, 1, 1) * scale
            if diagonal:
                s = jnp.where(_causal_keep(t), s, MASK_VALUE)
            p = jnp.exp(s - lse_ref[rows_i, :])
            dv = dv + _dot(p.astype(BF16), dob, 0, 0)
            dp = _dot(dob, vb, 1, 1)
            ds = (p * (dp - dl_ref[rows_i, :]) * scale).astype(BF16)
            dk = dk + _dot(ds, qb, 0, 0)
            dq_ref[rows_i, :] += _dot(ds, kb, 1, 0)
            return dk, dv

        carry = step(j, (jnp.zeros((t, MLA_QK), F32), jnp.zeros((t, MLA_V), F32)), True)
        dk, dv = lax.fori_loop(j + 1, n_t, lambda i, c: step(i, c, False), carry)
        dk_ref[...] = dk
        dv_ref[...] = dv

    def head(n):
        return pl.BlockSpec((None, rows, n), lambda h, j: (h, 0, 0))

    def tile(n):
        return pl.BlockSpec((None, t, n), lambda h, j: (h, j, 0))

    full, extra = _carry(body, comm, 6, 3, grid)
    res = pl.pallas_call(
        full, name="mla_bwd",
        out_shape=(jax.ShapeDtypeStruct((nh, rows, MLA_QK), F32), jax.ShapeDtypeStruct((nh, rows, MLA_QK), F32),
                   jax.ShapeDtypeStruct((nh, rows, MLA_V), F32)) + extra["out_shape"],
        grid=grid,
        in_specs=[head(MLA_QK), tile(MLA_QK), tile(MLA_V), pl.BlockSpec((rows, MLA_V), lambda h, j: (0, h)), head(1), head(1)]
        + extra["in_specs"],
        out_specs=(head(MLA_QK), tile(MLA_QK), tile(MLA_V)) + extra["out_specs"],
        input_output_aliases=extra["aliases"], scratch_shapes=extra["scratch"],
        compiler_params=_params("arbitrary", "arbitrary"),
    )(q, k, v, do, lse, delta, *extra["operands"])
    return res[0], res[1], res[2], res[3:]


def _swa_slope(h):
    return 2.0 ** (-8.0 * (h + 1) / SWA_Q_HEADS)


def _swa_band_specs(rows):
    w = SWA_WINDOW
    kvw = SWA_KV_HEADS * SWA_HEAD_DIM

    def prev(i):
        return jnp.maximum(i - 1, 0)

    return [pl.BlockSpec((w, kvw), lambda i: (prev(i), 0)), pl.BlockSpec((w, kvw), lambda i: (i, 0)),
            pl.BlockSpec((w, kvw), lambda i: (prev(i), 0)), pl.BlockSpec((w, kvw), lambda i: (i, 0)),
            pl.BlockSpec((w, 1), lambda i: (i, 0)),
            pl.BlockSpec((1, w), lambda i: (0, prev(i))), pl.BlockSpec((1, w), lambda i: (0, i)),
            pl.BlockSpec(memory_space=pltpu.SMEM)]


def _swa_bias(i, pc_ref, pp_ref, pn_ref):
    w = SWA_WINDOW
    k_pos = jnp.concatenate([pp_ref[...], pn_ref[...]], axis=1)
    dist = jnp.abs(pc_ref[...] - k_pos).astype(F32)
    r = lax.broadcasted_iota(jnp.int32, (w, 2 * w), 0)
    col = lax.broadcasted_iota(jnp.int32, (w, 2 * w), 1)
    delta = r + w - col
    valid = (delta >= 0) & (delta < w) & ((col >= w) | (i > 0))
    return dist, valid


def _swa_fwd(q, k, v, pos_col, pos_row, sinks):
    rows = q.shape[0]
    w, hd = SWA_WINDOW, SWA_HEAD_DIM
    scale = hd ** -0.5

    def body(q_ref, kp_ref, kc_ref, vp_ref, vc_ref, pc_ref, pp_ref, pn_ref, sink_ref, o_ref, lse_ref):
        dist, valid = _swa_bias(pl.program_id(0), pc_ref, pp_ref, pn_ref)
        for kvh in range(SWA_KV_HEADS):
            cols = slice(kvh * hd, (kvh + 1) * hd)
            kb = jnp.concatenate([kp_ref[:, cols], kc_ref[:, cols]], axis=0)
            vb = jnp.concatenate([vp_ref[:, cols], vc_ref[:, cols]], axis=0)
            for g in range(SWA_GROUP):
                h = kvh * SWA_GROUP + g
                sink = sink_ref[h]
                s = _dot(q_ref[:, h * hd:(h + 1) * hd], kb, 1, 1) * scale - _swa_slope(h) * dist
                s = jnp.where(valid, s, MASK_VALUE)
                m = jnp.maximum(jnp.max(s, axis=-1, keepdims=True), sink)
                e = jnp.exp(s - m)
                den = jnp.sum(e, axis=-1, keepdims=True) + jnp.exp(sink - m)
                o_ref[:, h * hd:(h + 1) * hd] = _dot((e / den).astype(BF16), vb, 1, 0)
                lse_ref[:, h:h + 1] = m + jnp.log(den)

    return pl.pallas_call(
        body, name="swa_fwd",
        out_shape=(jax.ShapeDtypeStruct((rows, MIX_B), F32), jax.ShapeDtypeStruct((rows, SWA_Q_HEADS), F32)),
        grid=(rows // w,),
        in_specs=[pl.BlockSpec((w, MIX_B), lambda i: (i, 0))] + _swa_band_specs(rows),
        out_specs=(pl.BlockSpec((w, MIX_B), lambda i: (i, 0)), pl.BlockSpec((w, SWA_Q_HEADS), lambda i: (i, 0))),
        compiler_params=_params("parallel"),
    )(q, k, k, v, v, pos_col, pos_row, pos_row, sinks)


def _swa_bwd(q, k, v, pos_col, pos_row, sinks, do, lse, delta):
    rows = q.shape[0]
    w, hd = SWA_WINDOW, SWA_HEAD_DIM
    kvw = SWA_KV_HEADS * hd
    scale = hd ** -0.5

    def body(q_ref, kp_ref, kc_ref, vp_ref, vc_ref, pc_ref, pp_ref, pn_ref, sink_ref, do_ref, lse_ref, dl_ref,
             dq_ref, dkc_ref, dkp_ref, dvc_ref, dvp_ref, dsink_ref):
        dist, valid = _swa_bias(pl.program_id(0), pc_ref, pp_ref, pn_ref)

        @pl.when(pl.program_id(0) == 0)
        def _():
            dsink_ref[...] = jnp.zeros_like(dsink_ref)

        for kvh in range(SWA_KV_HEADS):
            cols = slice(kvh * hd, (kvh + 1) * hd)
            kb = jnp.concatenate([kp_ref[:, cols], kc_ref[:, cols]], axis=0)
            vb = jnp.concatenate([vp_ref[:, cols], vc_ref[:, cols]], axis=0)
            dk = jnp.zeros((2 * w, hd), F32)
            dv = jnp.zeros((2 * w, hd), F32)
            for g in range(SWA_GROUP):
                h = kvh * SWA_GROUP + g
                hc = slice(h * hd, (h + 1) * hd)
                qb, dob = q_ref[:, hc], do_ref[:, hc]
                lse_h, dl_h = lse_ref[:, h:h + 1], dl_ref[:, h:h + 1]
                s = _dot(qb, kb, 1, 1) * scale - _swa_slope(h) * dist
                s = jnp.where(valid, s, MASK_VALUE)
                p = jnp.exp(s - lse_h)
                ds = (p * (_dot(dob, vb, 1, 1) - dl_h) * scale).astype(BF16)
                dq_ref[:, hc] = _dot(ds, kb, 1, 0).astype(BF16)
                dk = dk + _dot(ds, qb, 0, 0)
                dv = dv + _dot(p.astype(BF16), dob, 0, 0)
                dsink_ref[:, h:h + 1] += -jnp.sum(jnp.exp(sink_ref[h] - lse_h) * dl_h, axis=0, keepdims=True)
            dkp_ref[:, cols] = dk[:w]
            dkc_ref[:, cols] = dk[w:]
            dvp_ref[:, cols] = dv[:w]
            dvc_ref[:, cols] = dv[w:]

    def blk(n):
        return pl.BlockSpec((w, n), lambda i: (i, 0))

    return pl.pallas_call(
        body, name="swa_bwd",
        out_shape=(jax.ShapeDtypeStruct((rows, MIX_B), BF16),) + (jax.ShapeDtypeStruct((rows, kvw), F32),) * 4
        + (jax.ShapeDtypeStruct((1, SWA_Q_HEADS), F32),),
        grid=(rows // w,),
        in_specs=[blk(MIX_B)] + _swa_band_specs(rows) + [blk(MIX_B), blk(SWA_Q_HEADS), blk(SWA_Q_HEADS)],
        out_specs=(blk(MIX_B), blk(kvw), blk(kvw), blk(kvw), blk(kvw), pl.BlockSpec((1, SWA_Q_HEADS), lambda i: (0, 0))),
        compiler_params=_params("arbitrary"),
    )(q, k, k, v, v, pos_col, pos_row, pos_row, sinks, do, lse, delta)


def _band_merge(cur, prev, name):
    rows, n = cur.shape
    w = SWA_WINDOW
    last = rows // w - 1

    def body(c_ref, p_ref, o_ref):
        nxt = jnp.where(pl.program_id(0) < last, p_ref[...], 0.0)
        o_ref[...] = (c_ref[...] + nxt).astype(BF16)

    return pl.pallas_call(
        body, name=name, out_shape=jax.ShapeDtypeStruct((rows, n), BF16), grid=(rows // w,),
        in_specs=[pl.BlockSpec((w, n), lambda j: (j, 0)), pl.BlockSpec((w, n), lambda j: (jnp.minimum(j + 1, last), 0))],
        out_specs=pl.BlockSpec((w, n), lambda j: (j, 0)), compiler_params=_params("parallel"),
    )(cur, prev)


def _adamw(w, g, m, v, name, emit_grad=False, comm=None):
    rows, cols = w.shape
    tr, tc = _fit(rows, 352, 8), _fit(cols, MM_TILE)
    c1 = 1.0 - ADAM_B1 ** ADAM_STEP
    c2 = 1.0 - ADAM_B2 ** ADAM_STEP
    n_out = 4 if emit_grad else 3
    grid = (rows // tr, cols // tc)

    def body(ins, outs):
        w_ref, g_ref, m_ref, v_ref = ins
        gr = g_ref[...]
        m_new = ADAM_B1 * m_ref[...] + (1.0 - ADAM_B1) * gr
        v_new = ADAM_B2 * v_ref[...] + (1.0 - ADAM_B2) * jnp.square(gr)
        outs[1][...] = m_new
        outs[2][...] = v_new
        outs[0][...] = -ADAM_LR * ((m_new / c1) / (jnp.sqrt(v_new / c2) + ADAM_EPS) + ADAM_WD * w_ref[...])
        if emit_grad:
            outs[3][...] = gr

    spec = pl.BlockSpec((tr, tc), lambda i, j: (i, j))
    full, extra = _carry(body, comm, 4, n_out, grid)
    res = pl.pallas_call(
        full, name=name, out_shape=(jax.ShapeDtypeStruct(w.shape, F32),) * n_out + extra["out_shape"], grid=grid,
        in_specs=[spec] * 4 + extra["in_specs"], out_specs=(spec,) * n_out + extra["out_specs"],
        input_output_aliases=extra["aliases"], scratch_shapes=extra["scratch"],
        compiler_params=_params("arbitrary", "arbitrary"),
    )(w, g, m, v, *extra["operands"])
    return res[:n_out] + ((res[n_out:],) if comm else ())


OTHER_CHIPS = ((1, 0), (0, 1), (1, 1))


def _place():
    x, y, c = lax.axis_index("x"), lax.axis_index("y"), lax.axis_index("c")
    return x, y, c


def _flip(v, f):
    return 1 - v if f else v


class _Comm:
    def __init__(self, ins, out_shapes, aliases, sem_sizes, start, finish):
        self.ins, self.out_shapes, self.aliases, self.sem_sizes = list(ins), list(out_shapes), dict(aliases), list(sem_sizes)
        self.start, self.finish = start, finish

    @property
    def scratch(self):
        return [pltpu.SemaphoreType.DMA((n,)) for n in self.sem_sizes]


def _join(first, second):
    n_i, n_o, n_s = len(first.ins), len(first.out_shapes), len(first.sem_sizes)

    def both(method):
        def run(ins, outs, sems):
            getattr(first, method)(ins[:n_i], outs[:n_o], sems[:n_s])
            getattr(second, method)(ins[n_i:], outs[n_o:], sems[n_s:])
        return run

    aliases = {**first.aliases, **{n_i + i: n_o + o for i, o in second.aliases.items()}}
    return _Comm(first.ins + second.ins, first.out_shapes + second.out_shapes, aliases, first.sem_sizes + second.sem_sizes,
                 both("start"), both("finish"))


def _run_comm(comm, name):
    n_in, n_out = len(comm.ins), len(comm.out_shapes)

    def body(*refs):
        ins, outs, sems = refs[:n_in], refs[n_in:n_in + n_out], refs[n_in + n_out:]
        comm.start(ins, outs, sems)
        comm.finish(ins, outs, sems)

    return pl.pallas_call(
        body, name=name, out_shape=tuple(comm.out_shapes), in_specs=[ANY] * n_in, out_specs=(ANY,) * n_out,
        input_output_aliases=comm.aliases, scratch_shapes=comm.scratch,
    )(*comm.ins)


def _grid_ends(grid):
    first = last = None
    for axis, n in enumerate(grid):
        pid = pl.program_id(axis)
        first = (pid == 0) if first is None else first & (pid == 0)
        last = (pid == n - 1) if last is None else last & (pid == n - 1)
    return first, last


def _cast_into_slot(w, name, comm=None):
    rows, cols = w.shape
    tr, tc = _fit(rows, 704, 16), _fit(cols, MM_TILE)
    grid = (rows // tr, cols // tc)

    def body(ins, outs):
        outs[0][...] = ins[0][...].astype(BF16)

    full, extra = _carry(body, comm, 1, 1, grid)
    res = pl.pallas_call(
        full, name=name, out_shape=(jax.ShapeDtypeStruct((N_CHIPS, rows, cols), BF16),) + extra["out_shape"], grid=grid,
        in_specs=[pl.BlockSpec((tr, tc), lambda i, j: (i, j))] + extra["in_specs"],
        out_specs=(pl.BlockSpec((None, tr, tc), lambda i, j: (2 * lax.axis_index("x") + lax.axis_index("y"), i, j)),)
        + extra["out_specs"],
        input_output_aliases=extra["aliases"], scratch_shapes=extra["scratch"],
        compiler_params=_params("arbitrary", "arbitrary"),
    )(w, *extra["operands"])
    return (res[0], res[1:]) if comm else res[0]


def _gather_comm(slots, js=(0, 1, 2)):
    n = len(slots)
    pairs = [(a, j) for a in range(n) for j in js]

    def copies(src, dst, sems):
        send, recv, fsend, frecv = sems
        x, y, c = _place()
        me = 2 * x + y

        def rows(a, core):
            half = slots[a].shape[1] // 2
            return pl.ds(core * half, half)

        def chip(j):
            return _flip(x, OTHER_CHIPS[j][0]), _flip(y, OTHER_CHIPS[j][1])

        def out(a, j):
            px, py = chip(j)
            return pltpu.make_async_remote_copy(
                src_ref=src[a].at[me, rows(a, c)], dst_ref=dst[a].at[me, rows(a, c)], send_sem=send.at[a * 3 + j],
                recv_sem=recv.at[a * 3 + j], device_id=(px, py, c), device_id_type=MESH)

        def landed(a, j):
            px, py = chip(j)
            blk = 2 * px + py
            return pltpu.make_async_remote_copy(
                src_ref=dst[a].at[blk, rows(a, c)], dst_ref=dst[a].at[blk, rows(a, c)], send_sem=send.at[a * 3 + j],
                recv_sem=recv.at[a * 3 + j], device_id=(x, y, c), device_id_type=MESH)

        def passed(a, j, core):
            px, py = chip(j)
            blk = 2 * px + py
            return pltpu.make_async_remote_copy(
                src_ref=dst[a].at[blk, rows(a, core)], dst_ref=dst[a].at[blk, rows(a, core)], send_sem=fsend.at[a * 3 + j],
                recv_sem=frecv.at[a * 3 + j], device_id=(x, y, 1 - c), device_id_type=MESH)

        return c, out, landed, passed

    def start(src, dst, sems):
        _, out, _, _ = copies(src, dst, sems)
        for a, j in pairs:
            out(a, j).start()

    def finish(src, dst, sems):
        c, out, landed, passed = copies(src, dst, sems)
        for a, j in pairs:
            landed(a, j).wait_recv()
            passed(a, j, c).start()
        for a, j in pairs:
            passed(a, j, 1 - c).wait_recv()
        for a, j in pairs:
            out(a, j).wait_send()
            passed(a, j, c).wait_send()

    shapes = [jax.ShapeDtypeStruct(s.shape, s.dtype) for s in slots]
    return _Comm(slots, shapes, {a: a for a in range(n)}, [3 * n] * 4, start, finish)


def _pair_exchange_comm(grads):
    n = len(grads)

    def copy(src, dst, sems, a):
        x, y, c = _place()
        half = grads[a].shape[1] // 2
        return pltpu.make_async_remote_copy(
            src_ref=src[a].at[:, pl.ds((1 - c) * half, half)], dst_ref=dst[a], send_sem=sems[0].at[a],
            recv_sem=sems[1].at[a], device_id=(x, y, 1 - c), device_id_type=MESH)

    def start(src, dst, sems):
        for a in range(n):
            copy(src, dst, sems, a).start()

    def finish(src, dst, sems):
        for a in range(n):
            copy(src, dst, sems, a).wait()

    shapes = [jax.ShapeDtypeStruct((g.shape[0], g.shape[1] // 2, g.shape[2]), g.dtype) for g in grads]
    return _Comm(grads, shapes, {}, [n, n], start, finish)


def _pair_add(mine, theirs, name):
    nb, rows, cols = theirs.shape
    tr, tc = _fit(rows, 704, 16), _fit(cols, MM_TILE)
    n_r = rows // tr

    def body(a_ref, b_ref, lo_ref, own_ref):
        s = a_ref[...] + b_ref[...]
        lo_ref[...] = s.astype(BF16)

        @pl.when(pl.program_id(2) == 2 * lax.axis_index("x") + lax.axis_index("y"))
        def _():
            own_ref[...] = s

    return pl.pallas_call(
        body, name=name, grid=(n_r, cols // tc, nb),
        in_specs=[pl.BlockSpec((None, tr, tc), lambda i, j, b: (b, lax.axis_index("c") * n_r + i, j)),
                  pl.BlockSpec((None, tr, tc), lambda i, j, b: (b, i, j))],
        out_specs=(pl.BlockSpec((None, tr, tc), lambda i, j, b: (b, i, j)), pl.BlockSpec((tr, tc), lambda i, j, b: (i, j))),
        out_shape=(jax.ShapeDtypeStruct(theirs.shape, BF16), jax.ShapeDtypeStruct((rows, cols), F32)),
        compiler_params=_params("parallel", "parallel", "arbitrary"),
    )(mine, theirs)


def _chip_exchange_comm(sums, js=(0, 1, 2), into=None):
    n = len(sums)
    pairs = [(a, j) for a in range(n) for j in js]

    def copy(src, dst, sems, a, j):
        x, y, c = _place()
        px, py = _flip(x, OTHER_CHIPS[j][0]), _flip(y, OTHER_CHIPS[j][1])
        return pltpu.make_async_remote_copy(
            src_ref=src[a].at[2 * px + py], dst_ref=dst[a].at[j], send_sem=sems[0].at[a * 3 + j],
            recv_sem=sems[1].at[a * 3 + j], device_id=(px, py, c), device_id_type=MESH)

    def start(src, dst, sems):
        for a, j in pairs:
            copy(src, dst, sems, a, j).start()

    def finish(src, dst, sems):
        for a, j in pairs:
            copy(src, dst, sems, a, j).wait()

    shapes = [jax.ShapeDtypeStruct((3,) + s.shape[1:], s.dtype) for s in sums]
    return _Comm(list(sums) + list(into or []), shapes, {n + a: a for a in range(n)} if into else {}, [3 * n, 3 * n],
                 start, finish)


def _chip_add(own, got, name):
    rows, cols = own.shape
    tr, tc = _fit(rows, 704, 16), _fit(cols, MM_TILE)
    n_r = rows // tr

    def body(o_ref, g_ref, out_ref):
        out_ref[...] = ((o_ref[...] + g_ref[0].astype(F32)) + g_ref[1].astype(F32)) + g_ref[2].astype(F32)

    return pl.pallas_call(
        body, name=name, out_shape=jax.ShapeDtypeStruct((2 * rows, cols), F32), grid=(n_r, cols // tc),
        in_specs=[pl.BlockSpec((tr, tc), lambda i, j: (i, j)), pl.BlockSpec((3, tr, tc), lambda i, j: (0, i, j))],
        out_specs=pl.BlockSpec((tr, tc), lambda i, j: (lax.axis_index("c") * n_r + i, j)),
        compiler_params=_params("parallel", "parallel"),
    )(own, got)


def _pair_share_comm(grads):
    n = len(grads)

    def copy(src, dst, sems, a, mine):
        x, y, c = _place()
        half = grads[a].shape[0] // 2
        rows = pl.ds((c if mine else 1 - c) * half, half)
        return pltpu.make_async_remote_copy(
            src_ref=src[a].at[rows], dst_ref=dst[a].at[rows], send_sem=sems[0].at[a], recv_sem=sems[1].at[a],
            device_id=(x, y, 1 - c), device_id_type=MESH)

    def start(src, dst, sems):
        for a in range(n):
            copy(src, dst, sems, a, True).start()

    def finish(src, dst, sems):
        for a in range(n):
            copy(src, dst, sems, a, False).wait_recv()
            copy(src, dst, sems, a, True).wait_send()

    shapes = [jax.ShapeDtypeStruct(g.shape, g.dtype) for g in grads]
    return _Comm(grads, shapes, {a: a for a in range(n)}, [n, n], start, finish)


def _all_sum_small(vec):
    r, n = vec.shape
    flips = [(a, b, d) for a in (0, 1) for b in (0, 1) for d in (0, 1)][1:]

    def body(v_ref, o_ref, buf, send, recv):
        x, y, c = _place()
        me = 4 * x + 2 * y + c
        cps = []
        for k, (fx, fy, fc) in enumerate(flips):
            cp = pltpu.make_async_remote_copy(
                src_ref=v_ref, dst_ref=buf.at[me], send_sem=send.at[k], recv_sem=recv.at[k],
                device_id=(_flip(x, fx), _flip(y, fy), _flip(c, fc)), device_id_type=MESH)
            cp.start()
            cps.append(cp)
        buf[me] = v_ref[...]
        for k, (fx, fy, fc) in enumerate(flips):
            peer = 4 * _flip(x, fx) + 2 * _flip(y, fy) + _flip(c, fc)
            pltpu.make_async_remote_copy(
                src_ref=v_ref, dst_ref=buf.at[peer], send_sem=send.at[k], recv_sem=recv.at[k],
                device_id=(x, y, c), device_id_type=MESH).wait_recv()
        for cp in cps:
            cp.wait_send()
        acc = buf[0]
        for d in range(1, 8):
            acc = acc + buf[d]
        o_ref[...] = acc

    return pl.pallas_call(
        body, name="all_sum_small", out_shape=jax.ShapeDtypeStruct((r, n), F32),
        in_specs=[pl.BlockSpec(memory_space=pltpu.VMEM)], out_specs=pl.BlockSpec(memory_space=pltpu.VMEM),
        scratch_shapes=[pltpu.VMEM((8, r, n), F32), pltpu.SemaphoreType.DMA((7,)), pltpu.SemaphoreType.DMA((7,))],
    )(vec)


SMALL = ("attn_pre_g", "q_norm_g", "kv_norm_g", "swa_sinks", "grp_a_g", "grp_b_g", "attn_post_g", "ffn_pre_g", "ffn_post_g")
BIG = ("w_in", "w_uq", "w_ukv", "w_o", "w_gate", "w_up", "w_down")
ORDER = ("attn_pre_g", "w_in", "q_norm_g", "w_uq", "kv_norm_g", "w_ukv", "swa_sinks", "grp_a_g", "grp_b_g", "w_o",
         "attn_post_g", "ffn_pre_g", "w_gate", "w_up", "w_down", "ffn_post_g")


def _pad_lanes(v):
    n = v.shape[1]
    return jnp.pad(v, ((0, 0), (0, -n % LANES)))


def kernel(x, positions, attn_pre_g, w_in, q_norm_g, w_uq, kv_norm_g, w_ukv, swa_sinks, grp_a_g, grp_b_g, w_o, attn_post_g, ffn_pre_g, w_gate, w_up, w_down, ffn_post_g, loss_target, m_attn_pre_g, m_w_in, m_q_norm_g, m_w_uq, m_kv_norm_g, m_w_ukv, m_swa_sinks, m_grp_a_g, m_grp_b_g, m_w_o, m_attn_post_g, m_ffn_pre_g, m_w_gate, m_w_up, m_w_down, m_ffn_post_g, v_attn_pre_g, v_w_in, v_q_norm_g, v_w_uq, v_kv_norm_g, v_w_ukv, v_swa_sinks, v_grp_a_g, v_grp_b_g, v_w_o, v_attn_post_g, v_ffn_pre_g, v_w_gate, v_w_up, v_w_down, v_ffn_post_g):
    given = dict(locals())
    w32 = {k: given[k][0] for k in BIG}
    gains = {k: given[k] for k in SMALL}
    xs, tgt = x[0], loss_target[0]
    seq, d_model = xs.shape
    q_rank, kv_rank = q_norm_g.shape[1], kv_norm_g.shape[1]
    kvw = SWA_KV_HEADS * SWA_HEAD_DIM

    slot = {k: _cast_into_slot(w32[k], "cast_" + k) for k in ("w_in", "w_uq", "w_ukv", "w_o")}
    full = {}
    slot["w_gate"], (part,) = _cast_into_slot(w32["w_gate"], "cast_w_gate", comm=_gather_comm([slot["w_in"]], js=(0, 1)))
    slot["w_up"], (full["w_in"],) = _cast_into_slot(w32["w_up"], "cast_w_up", comm=_gather_comm([part], js=(2,)))
    slot["w_down"] = _cast_into_slot(w32["w_down"], "cast_w_down")

    pos = positions[0]
    inv = 1.0 / (ROPE_THETA ** (jnp.arange(0, MLA_ROPE, 2, dtype=F32) / MLA_ROPE))
    ang = pos.astype(F32)[:, None] * inv
    cos, sin = jnp.cos(ang), jnp.sin(ang)
    pos_col, pos_row = pos[:, None], pos[None, :]
    sinks = swa_sinks[0]

    a = _norm_fwd(xs, attn_pre_g, BF16, "attn_pre_norm")
    proj4, (full["w_uq"], full["w_ukv"], full["w_o"]) = _matmul(
        a, full["w_in"], name="in_proj", comm=_gather_comm([slot["w_uq"], slot["w_ukv"], slot["w_o"]]))
    proj = jnp.concatenate([proj4[b] for b in range(N_CHIPS)], axis=1)
    cuts = (0, q_rank, q_rank + kv_rank, q_rank + kv_rank + MLA_ROPE)
    cuts = cuts + (cuts[3] + MIX_B, cuts[3] + MIX_B + kvw, cuts[3] + MIX_B + 2 * kvw)
    c_q, c_kv, k_rope, q_s, k_s, v_s = (proj[:, lo:hi] for lo, hi in zip(cuts[:-1], cuts[1:]))
    cqn = _norm_fwd(c_q, q_norm_g, BF16, "q_norm")
    ckvn = _norm_fwd(c_kv, kv_norm_g, BF16, "kv_norm")
    q4 = _matmul(cqn, full["w_uq"], name="q_up")
    kv4 = _matmul(ckvn, full["w_ukv"], name="kv_up")
    qh, kh, vh = _mla_prep(q4, kv4, k_rope, cos, sin)
    o_a, lse_a, (full["w_gate"],) = _mla_fwd(qh, kh, vh, comm=_gather_comm([slot["w_gate"]]))
    q_sb, k_sb, v_sb = q_s.astype(BF16), k_s.astype(BF16), v_s.astype(BF16)
    o_b, lse_b = _swa_fwd(q_sb, k_sb, v_sb, pos_col, pos_row, sinks)
    mix = _mix_fwd(o_a, o_b, grp_a_g, grp_b_g)
    w_o_full = full["w_o"].reshape(N_CHIPS * full["w_o"].shape[1], d_model)
    ao = _matmul(mix, w_o_full, name="out_proj")
    h1, f = _post_pre_fwd(xs, ao, attn_post_g, ffn_pre_g)
    gate, (full["w_up"],) = _matmul(f, full["w_gate"], name="ffn_gate", comm=_gather_comm([slot["w_up"]]))
    (up, act), (full["w_down"],) = _matmul(
        f, full["w_up"], name="ffn_up", comm=_gather_comm([slot["w_down"]]), epilogue=_swiglu_fwd_tile, extras=(gate,),
        out_dtype=(F32, BF16), tm=512)
    dn = _matmul(act, full["w_down"], reduce_b=True, name="ffn_down")
    loss_row, dy, ddn, d_ffn_post = _loss_bwd(h1, dn, ffn_post_g, tgt)

    low, own, got = {}, {}, {}
    dw_down = _matmul(act, ddn, ta=True, name="ffn_down_dw", tn=1024, tk=512)
    (dgate, dup), (theirs,) = _matmul(
        ddn, full["w_down"], tb=True, name="ffn_down_dx", epilogue=_swiglu_bwd_tile, extras=(gate, up),
        out_dtype=(BF16, BF16), tm=512, tk=512, comm=_pair_exchange_comm([dw_down]))
    low["w_down"], own["w_down"] = _pair_add(dw_down, theirs, "pair_add_w_down")
    dw_gate, (part,) = _matmul(f, dgate, ta=True, name="ffn_gate_dw", tk=512,
                               comm=_chip_exchange_comm([low["w_down"]], js=(0, 1)))
    dw_up, (got["w_down"], theirs) = _matmul(
        f, dup, ta=True, name="ffn_up_dw", tk=512,
        comm=_join(_chip_exchange_comm([low["w_down"]], js=(2,), into=[part]), _pair_exchange_comm([dw_gate])))
    low["w_gate"], own["w_gate"] = _pair_add(dw_gate, theirs, "pair_add_w_gate")
    grads4 = dict(w_down=dw_down, w_gate=dw_gate, w_up=dw_up)
    df_g, (part, theirs) = _matmul(
        dgate, full["w_gate"], tb=True, reduce_b=True, name="ffn_gate_dx",
        comm=_join(_chip_exchange_comm([low["w_gate"]], js=(0, 1)), _pair_exchange_comm([dw_up])))
    low["w_up"], own["w_up"] = _pair_add(dw_up, theirs, "pair_add_w_up")
    df_u, (got["w_gate"],) = _matmul(dup, full["w_up"], tb=True, reduce_b=True, name="ffn_up_dx",
                                     comm=_chip_exchange_comm([low["w_gate"]], js=(2,), into=[part]))
    dh1, d_ffn_pre = _norm_bwd(h1, ffn_pre_g, [df_g, df_u], [dy], F32, "ffn_pre_norm_bwd")
    dao, d_attn_post = _norm_bwd(ao, attn_post_g, [dh1], [], BF16, "attn_post_norm_bwd")
    dmix = _matmul(dao, w_o_full, tb=True, name="out_proj_dx")
    dw_o = _matmul(mix, dao, ta=True, name="out_proj_dw")
    do_a, do_b, d_grp_a, d_grp_b, dl_a, dl_b = _mix_bwd(o_a, o_b, grp_a_g, grp_b_g, dmix)
    dl_a = dl_a.T[:, :, None]
    dqh, dkh, dvh, (got["w_up"],) = _mla_bwd(qh, kh, vh, do_a, lse_a, dl_a, comm=_chip_exchange_comm([low["w_up"]]))
    dq4, dkv4, dk_rope = _mla_unprep(dqh, dkh, dvh, cos, sin)
    dw_uq = _matmul(cqn, dq4, ta=True, name="q_up_dw")
    dcqn = _matmul(dq4, full["w_uq"], tb=True, reduce_b=True, name="q_up_dx")
    dw_ukv = _matmul(ckvn, dkv4, ta=True, name="kv_up_dw")
    dckvn = _matmul(dkv4, full["w_ukv"], tb=True, reduce_b=True, name="kv_up_dx")
    dc_q, d_q_norm = _norm_bwd(c_q, q_norm_g, [dcqn], [], BF16, "q_norm_bwd")
    dc_kv, d_kv_norm = _norm_bwd(c_kv, kv_norm_g, [dckvn], [], BF16, "kv_norm_bwd")
    dq_s, dk_cur, dk_prev, dv_cur, dv_prev, d_sinks = _swa_bwd(q_sb, k_sb, v_sb, pos_col, pos_row, sinks, do_b, lse_b, dl_b)
    dk_s = _band_merge(dk_cur, dk_prev, "swa_dk_merge")
    dv_s = _band_merge(dv_cur, dv_prev, "swa_dv_merge")
    dproj = jnp.concatenate([dc_q, dc_kv, dk_rope.astype(BF16), dq_s, dk_s, dv_s], axis=1)
    blk_w = dproj.shape[1] // N_CHIPS
    dproj4 = jnp.stack([dproj[:, b * blk_w:(b + 1) * blk_w] for b in range(N_CHIPS)])
    mid = ("w_o", "w_uq", "w_ukv")
    grads4.update(w_o=dw_o.reshape(N_CHIPS, -1, d_model), w_uq=dw_uq, w_ukv=dw_ukv)
    dw_in, theirs = _matmul(a, dproj4, ta=True, name="in_proj_dw", tm=512,
                            comm=_pair_exchange_comm([grads4[k] for k in mid]))
    for k, t in zip(mid, theirs):
        low[k], own[k] = _pair_add(grads4[k], t, "pair_add_" + k)
    da, (*got_mid, theirs_in) = _matmul(
        dproj4, full["w_in"], tb=True, reduce_b=True, name="in_proj_dx",
        comm=_join(_chip_exchange_comm([low[k] for k in mid]), _pair_exchange_comm([dw_in])))
    got.update(zip(mid, got_mid))
    six = tuple(k for k in BIG if k != "w_in")
    halves = [_chip_add(own[k], got[k], "chip_add_" + k) for k in six]
    dx, d_attn_pre, shared = _norm_bwd(xs, attn_pre_g, [da], [dh1], F32, "attn_pre_norm_bwd", comm=_pair_share_comm(halves))
    g_big = dict(zip(six, shared))

    small_grads = dict(attn_pre_g=d_attn_pre, q_norm_g=d_q_norm, kv_norm_g=d_kv_norm, swa_sinks=d_sinks, grp_a_g=d_grp_a,
                       grp_b_g=d_grp_b, attn_post_g=d_attn_post, ffn_pre_g=d_ffn_pre, ffn_post_g=d_ffn_post)
    parts = [loss_row] + [_pad_lanes(small_grads[k]) for k in SMALL]
    packed = jnp.concatenate(parts, axis=1)
    n_packed = packed.shape[1]
    packed = jnp.pad(packed, ((0, 0), (0, -n_packed % (8 * LANES)))).reshape(8, -1)
    total8 = _all_sum_small(packed)
    total = total8.reshape(1, -1)
    loss = total[0, 0]
    g_small, off = {}, LANES
    for k in SMALL:
        n = gains[k].shape[1]
        g_small[k] = total[:, off:off + n]
        off += n + (-n % LANES)

    def pack_small(prefix):
        flat = jnp.concatenate([jnp.zeros((1, LANES), F32)] + [_pad_lanes(given[prefix + k]) for k in SMALL], axis=1)
        return jnp.pad(flat, ((0, 0), (0, -n_packed % (8 * LANES)))).reshape(8, -1)

    d_sm, m_sm, v_sm = (r.reshape(1, -1) for r in
                        _adamw(pack_small(""), total8, pack_small("m_"), pack_small("v_"), "adamw_small"))
    delta, new_m, new_v, off = {}, {}, {}, LANES
    for k in SMALL:
        n = gains[k].shape[1]
        delta[k], new_m[k], new_v[k] = d_sm[:, off:off + n], m_sm[:, off:off + n], v_sm[:, off:off + n]
        off += n + (-n % LANES)

    low["w_in"], own["w_in"] = _pair_add(dw_in, theirs_in, "pair_add_w_in")
    g_out = {}

    def adam(k, comm=None):
        res = _adamw(w32[k], g_big[k], given["m_" + k][0], given["v_" + k][0], "adamw_" + k, emit_grad=True, comm=comm)
        delta[k], new_m[k], new_v[k], g_out[k] = res[:4]
        return res[4] if comm else None

    (part,) = adam("w_gate", _chip_exchange_comm([low["w_in"]], js=(0, 1)))
    (got["w_in"],) = adam("w_up", _chip_exchange_comm([low["w_in"]], js=(2,), into=[part]))
    for k in ("w_down", "w_o", "w_uq", "w_ukv"):
        adam(k)
    half_in = _chip_add(own["w_in"], got["w_in"], "chip_add_w_in")
    (g_big["w_in"],) = _run_comm(_pair_share_comm([half_in]), "grad_pair_share")
    adam("w_in")

    def out(d, k):
        return d[k][None] if k in BIG else d[k]

    grads = {**g_small, **g_out}
    return (loss, dx[None], *[out(grads, k) for k in ORDER], *[out(delta, k) for k in ORDER],
            *[out(new_m, k) for k in ORDER], *[out(new_v, k) for k in ORDER])
```
